```python
import math
import jax, jax.numpy as jnp
from jax import lax
import numpy as np

D_MODEL = 1024
BATCH = 8
SEQ = 16384
DEPTH = 2

CONV_DIM = 1024
CONV_KERNEL = 31
D_INNER = 2 * D_MODEL
HEAD_DIM = 64
N_SSM_HEADS = D_INNER // HEAD_DIM
N_GROUPS = 8
D_STATE = 128
SSM_CONV = 5
CHUNK = 128
XBC_DIM = D_INNER + 2 * N_GROUPS * D_STATE
FFN_DIM = int(math.ceil(D_MODEL * 8 / 3 / 256) * 256)
PLE_DIM = 256
N_IN = 2 * CONV_DIM + 2 * D_MODEL + D_INNER + XBC_DIM + 2 * N_SSM_HEADS
DEEPNORM_ALPHA = (2 * DEPTH) ** 0.25
DEEPNORM_BETA = (8 * DEPTH) ** -0.25
LN_EPS = 1e-5
RMS_EPS = 1e-6

kernel_name = "hybrid_conformer_ssd_encoder"


def layer_norm(x, g, b):
    xf = x.astype(jnp.float32)
    mu = jnp.mean(xf, axis=-1, keepdims=True)
    var = jnp.mean(jnp.square(xf - mu), axis=-1, keepdims=True)
    return ((xf - mu) * lax.rsqrt(var + LN_EPS) * g + b).astype(x.dtype)


def rms_norm(x, g):
    xf = x.astype(jnp.float32)
    return (xf * lax.rsqrt(jnp.mean(jnp.square(xf), axis=-1, keepdims=True) + RMS_EPS) * g).astype(x.dtype)


def depthwise_conv_centred(u, w, b):
    pad = (w.shape[0] - 1) // 2
    out = lax.conv_general_dilated(
        u, w[:, None, :].astype(u.dtype), window_strides=(1,), padding=[(pad, pad)],
        dimension_numbers=("NWC", "WIO", "NWC"), feature_group_count=u.shape[-1])
    return out + b


def ssd_chunked(x, dt, A, Bm, Cm):
    b, s, h, p = x.shape
    g, n = Bm.shape[-2:]
    r = h // g
    c, l = s // CHUNK, CHUNK
    x = x.astype(jnp.float32)
    dt = dt.astype(jnp.float32)
    X = (x * dt[..., None]).reshape(b, c, l, g, r, p)
    Ad = (dt * A).reshape(b, c, l, g, r).transpose(0, 1, 3, 4, 2)
    Bc = Bm.astype(jnp.float32).reshape(b, c, l, g, n)
    Cc = Cm.astype(jnp.float32).reshape(b, c, l, g, n)
    A_cs = jnp.cumsum(Ad, axis=-1)
    idx = jnp.arange(l)
    lower = idx[:, None] >= idx[None, :]
    seg = A_cs[..., :, None] - A_cs[..., None, :]
    Lmat = jnp.exp(jnp.where(lower, seg, -jnp.inf))
    CB = jnp.einsum("bclgn,bcsgn->bcgls", Cc, Bc)
    y_diag = jnp.einsum("bcgrls,bcsgrp->bclgrp", CB[:, :, :, None] * Lmat, X)
    decay_states = jnp.exp(A_cs[..., -1:] - A_cs).transpose(0, 1, 4, 2, 3)
    states = jnp.einsum("bclgn,bclgrp->bcgrpn", Bc, X * decay_states[..., None])
    chunk_decay = jnp.exp(A_cs[..., -1])

    def step(carry, inp):
        st, dec = inp
        return carry * dec[..., None, None] + st, carry

    h0 = jnp.zeros((b, g, r, p, n), jnp.float32)
    _, prev = lax.scan(step, h0, (jnp.moveaxis(states, 1, 0), jnp.moveaxis(chunk_decay, 1, 0)))
    prev = jnp.moveaxis(prev, 0, 1)
    decay_out = jnp.exp(A_cs).transpose(0, 1, 4, 2, 3)
    y_off = jnp.einsum("bclgn,bcgrpn->bclgrp", Cc, prev) * decay_out[..., None]
    return (y_diag + y_off).reshape(b, s, h, p)


def conformer_branch(glu_in, conv_w, conv_b, ln_g, ln_b, w_out):
    a, gt = jnp.split(glu_in, 2, axis=-1)
    u = a * jax.nn.sigmoid(gt)
    u = depthwise_conv_centred(u, conv_w, conv_b)
    u = jax.nn.silu(layer_norm(u, ln_g, ln_b))
    return u @ w_out


def mamba2_bidir_branch(z, xbc, dt_raw, conv_w, conv_b, a_log, dt_bias, d_skip, norm_g, w_out):
    b, s, _ = z.shape
    xbc = jax.nn.silu(depthwise_conv_centred(xbc, conv_w, conv_b))
    xs, Bm, Cm = jnp.split(xbc, [D_INNER, D_INNER + N_GROUPS * D_STATE], axis=-1)
    xs = xs.reshape(b, s, N_SSM_HEADS, HEAD_DIM)
    Bm = Bm.reshape(b, s, N_GROUPS, D_STATE)
    Cm = Cm.reshape(b, s, N_GROUPS, D_STATE)
    dt_raw = dt_raw.astype(jnp.float32)
    dt_f = jax.nn.softplus(dt_raw[..., :N_SSM_HEADS] + dt_bias[0])
    dt_b = jax.nn.softplus(dt_raw[..., N_SSM_HEADS:] + dt_bias[1])
    A = -jnp.exp(a_log.astype(jnp.float32))
    y_f = ssd_chunked(xs, dt_f, A[0], Bm, Cm)
    flip = lambda t: jnp.flip(t, axis=1)
    y_b = flip(ssd_chunked(flip(xs), flip(dt_b), A[1], flip(Bm), flip(Cm)))
    y = y_f + y_b + xs.astype(jnp.float32) * d_skip[:, None]
    y = y.reshape(b, s, D_INNER) * jax.nn.silu(z.astype(jnp.float32))
    yg = y.reshape(b, s, N_GROUPS, D_INNER // N_GROUPS)
    yg = yg * lax.rsqrt(jnp.mean(jnp.square(yg), axis=-1, keepdims=True) + RMS_EPS)
    y = (yg.reshape(b, s, D_INNER) * norm_g).astype(z.dtype)
    return y @ w_out


def _fwd_setup_inputs(seed: int = 0) -> dict:
    key = jax.random.key(seed)
    ks = iter(jax.random.split(key, 32))

    def nrm(shape, scale):
        return jax.random.normal(next(ks), shape, jnp.float32) * scale

    L = DEPTH
    dt0 = jnp.exp(jax.random.uniform(next(ks), (L, 2, N_SSM_HEADS)) * (math.log(0.1) - math.log(1e-3)) + math.log(1e-3))
    dt_bias = dt0 + jnp.log(-jnp.expm1(-dt0))
    a_log = jnp.log(jax.random.uniform(next(ks), (L, 2, N_SSM_HEADS), minval=1.0, maxval=16.0))
    return {
        "x": nrm((BATCH, SEQ, D_MODEL), 1.0),
        "p": nrm((DEPTH, BATCH, SEQ, PLE_DIM), 1.0),
        "w_in": nrm((L, D_MODEL, N_IN), D_MODEL ** -0.5),
        "conv_a_w": nrm((L, CONV_KERNEL, CONV_DIM), CONV_KERNEL ** -0.5),
        "conv_a_b": nrm((L, CONV_DIM), 0.02),
        "ln_a_g": 1.0 + nrm((L, CONV_DIM), 0.02),
        "ln_a_b": nrm((L, CONV_DIM), 0.02),
        "w_a_out": nrm((L, CONV_DIM, D_MODEL), CONV_DIM ** -0.5 * DEEPNORM_BETA),
        "ssm_conv_w": nrm((L, SSM_CONV, XBC_DIM), SSM_CONV ** -0.5),
        "ssm_conv_b": nrm((L, XBC_DIM), 0.02),
        "a_log": a_log,
        "dt_bias": dt_bias,
        "d_skip": 1.0 + nrm((L, N_SSM_HEADS), 0.02),
        "ssm_norm_g": 1.0 + nrm((L, D_INNER), 0.02),
        "w_b_out": nrm((L, D_INNER, D_MODEL), D_INNER ** -0.5 * DEEPNORM_BETA),
        "w_o": nrm((L, D_MODEL, D_MODEL), D_MODEL ** -0.5 * DEEPNORM_BETA),
        "ln1_g": 1.0 + nrm((L, D_MODEL), 0.02),
        "ln1_b": nrm((L, D_MODEL), 0.02),
        "w_gate_up": nrm((L, D_MODEL, 2 * FFN_DIM), D_MODEL ** -0.5),
        "w_down": nrm((L, FFN_DIM, D_MODEL), FFN_DIM ** -0.5 * DEEPNORM_BETA),
        "ln2_g": 1.0 + nrm((L, D_MODEL), 0.02),
        "ln2_b": nrm((L, D_MODEL), 0.02),
        "w_ple": nrm((L, PLE_DIM, D_MODEL), PLE_DIM ** -0.5 * DEEPNORM_BETA),
        "ple_norm_g": 1.0 + nrm((L, D_MODEL), 0.02),
        "w_ple_gate": nrm((L, D_MODEL, D_MODEL), D_MODEL ** -0.5),
    }


def _fwd_reference(x, p, w_in, conv_a_w, conv_a_b, ln_a_g, ln_a_b, w_a_out, ssm_conv_w, ssm_conv_b,
              a_log, dt_bias, d_skip, ssm_norm_g, w_b_out, w_o, ln1_g, ln1_b, w_gate_up, w_down,
              ln2_g, ln2_b, w_ple, ple_norm_g, w_ple_gate):
    cuts = [2 * CONV_DIM, 2 * CONV_DIM + 2 * D_MODEL, 2 * CONV_DIM + 2 * D_MODEL + D_INNER,
            2 * CONV_DIM + 2 * D_MODEL + D_INNER + XBC_DIM]
    for i in range(DEPTH):
        proj = x @ w_in[i]
        glu_in, gates, z, xbc, dt_raw = jnp.split(proj, cuts, axis=-1)
        gate_a, gate_b = jnp.split(gates, 2, axis=-1)
        y_a = conformer_branch(glu_in, conv_a_w[i], conv_a_b[i], ln_a_g[i], ln_a_b[i], w_a_out[i])
        y_b = mamba2_bidir_branch(z, xbc, dt_raw, ssm_conv_w[i], ssm_conv_b[i], a_log[i],
                                  dt_bias[i], d_skip[i], ssm_norm_g[i], w_b_out[i])
        merged = jax.nn.sigmoid(gate_a) * y_a + jax.nn.sigmoid(gate_b) * y_b
        h = layer_norm(DEEPNORM_ALPHA * x + merged @ w_o[i], ln1_g[i], ln1_b[i])
        g_, u_ = jnp.split(h @ w_gate_up[i], 2, axis=-1)
        h2 = layer_norm(DEEPNORM_ALPHA * h + (jax.nn.silu(g_) * u_) @ w_down[i], ln2_g[i], ln2_b[i])
        e = rms_norm(p[i] @ w_ple[i], ple_norm_g[i])
        x = h2 + e * jax.nn.sigmoid(h2 @ w_ple_gate[i])
    return x


import jax as _jax
import jax.numpy as _jnp

TWIN_FORMAT = 'train_step'
FWD_PARAMS = ['x', 'p', 'w_in', 'conv_a_w', 'conv_a_b', 'ln_a_g', 'ln_a_b', 'w_a_out', 'ssm_conv_w', 'ssm_conv_b', 'a_log', 'dt_bias', 'd_skip', 'ssm_norm_g', 'w_b_out', 'w_o', 'ln1_g', 'ln1_b', 'w_gate_up', 'w_down', 'ln2_g', 'ln2_b', 'w_ple', 'ple_norm_g', 'w_ple_gate']
TWIN_WEIGHTS = ['w_in', 'conv_a_w', 'conv_a_b', 'ln_a_g', 'ln_a_b', 'w_a_out', 'ssm_conv_w', 'ssm_conv_b', 'a_log', 'dt_bias', 'd_skip', 'ssm_norm_g', 'w_b_out', 'w_o', 'ln1_g', 'ln1_b', 'w_gate_up', 'w_down', 'ln2_g', 'ln2_b', 'w_ple', 'ple_norm_g', 'w_ple_gate']
TWIN_DIFF_INPUT = 'x'
TWIN_INPUTS = ['x', 'p', 'w_in', 'conv_a_w', 'conv_a_b', 'ln_a_g', 'ln_a_b', 'w_a_out', 'ssm_conv_w', 'ssm_conv_b', 'a_log', 'dt_bias', 'd_skip', 'ssm_norm_g', 'w_b_out', 'w_o', 'ln1_g', 'ln1_b', 'w_gate_up', 'w_down', 'ln2_g', 'ln2_b', 'w_ple', 'ple_norm_g', 'w_ple_gate', 'loss_target', 'm_w_in', 'm_conv_a_w', 'm_conv_a_b', 'm_ln_a_g', 'm_ln_a_b', 'm_w_a_out', 'm_ssm_conv_w', 'm_ssm_conv_b', 'm_a_log', 'm_dt_bias', 'm_d_skip', 'm_ssm_norm_g', 'm_w_b_out', 'm_w_o', 'm_ln1_g', 'm_ln1_b', 'm_w_gate_up', 'm_w_down', 'm_ln2_g', 'm_ln2_b', 'm_w_ple', 'm_ple_norm_g', 'm_w_ple_gate', 'v_w_in', 'v_conv_a_w', 'v_conv_a_b', 'v_ln_a_g', 'v_ln_a_b', 'v_w_a_out', 'v_ssm_conv_w', 'v_ssm_conv_b', 'v_a_log', 'v_dt_bias', 'v_d_skip', 'v_ssm_norm_g', 'v_w_b_out', 'v_w_o', 'v_ln1_g', 'v_ln1_b', 'v_w_gate_up', 'v_w_down', 'v_ln2_g', 'v_ln2_b', 'v_w_ple', 'v_ple_norm_g', 'v_w_ple_gate']
TWIN_OUTPUTS = ['loss', 'grad_x', 'grad_w_in', 'grad_conv_a_w', 'grad_conv_a_b', 'grad_ln_a_g', 'grad_ln_a_b', 'grad_w_a_out', 'grad_ssm_conv_w', 'grad_ssm_conv_b', 'grad_a_log', 'grad_dt_bias', 'grad_d_skip', 'grad_ssm_norm_g', 'grad_w_b_out', 'grad_w_o', 'grad_ln1_g', 'grad_ln1_b', 'grad_w_gate_up', 'grad_w_down', 'grad_ln2_g', 'grad_ln2_b', 'grad_w_ple', 'grad_ple_norm_g', 'grad_w_ple_gate', 'delta_w_in', 'delta_conv_a_w', 'delta_conv_a_b', 'delta_ln_a_g', 'delta_ln_a_b', 'delta_w_a_out', 'delta_ssm_conv_w', 'delta_ssm_conv_b', 'delta_a_log', 'delta_dt_bias', 'delta_d_skip', 'delta_ssm_norm_g', 'delta_w_b_out', 'delta_w_o', 'delta_ln1_g', 'delta_ln1_b', 'delta_w_gate_up', 'delta_w_down', 'delta_ln2_g', 'delta_ln2_b', 'delta_w_ple', 'delta_ple_norm_g', 'delta_w_ple_gate', 'new_m_w_in', 'new_m_conv_a_w', 'new_m_conv_a_b', 'new_m_ln_a_g', 'new_m_ln_a_b', 'new_m_w_a_out', 'new_m_ssm_conv_w', 'new_m_ssm_conv_b', 'new_m_a_log', 'new_m_dt_bias', 'new_m_d_skip', 'new_m_ssm_norm_g', 'new_m_w_b_out', 'new_m_w_o', 'new_m_ln1_g', 'new_m_ln1_b', 'new_m_w_gate_up', 'new_m_w_down', 'new_m_ln2_g', 'new_m_ln2_b', 'new_m_w_ple', 'new_m_ple_norm_g', 'new_m_w_ple_gate', 'new_v_w_in', 'new_v_conv_a_w', 'new_v_conv_a_b', 'new_v_ln_a_g', 'new_v_ln_a_b', 'new_v_w_a_out', 'new_v_ssm_conv_w', 'new_v_ssm_conv_b', 'new_v_a_log', 'new_v_dt_bias', 'new_v_d_skip', 'new_v_ssm_norm_g', 'new_v_w_b_out', 'new_v_w_o', 'new_v_ln1_g', 'new_v_ln1_b', 'new_v_w_gate_up', 'new_v_w_down', 'new_v_ln2_g', 'new_v_ln2_b', 'new_v_w_ple', 'new_v_ple_norm_g', 'new_v_w_ple_gate']
TWIN_LEAF_KINDS = {'loss': 'loss', 'grad_x': 'grad_x', 'grad_w_in': 'grad_w', 'grad_conv_a_w': 'grad_w', 'grad_conv_a_b': 'grad_w', 'grad_ln_a_g': 'grad_w', 'grad_ln_a_b': 'grad_w', 'grad_w_a_out': 'grad_w', 'grad_ssm_conv_w': 'grad_w', 'grad_ssm_conv_b': 'grad_w', 'grad_a_log': 'grad_w', 'grad_dt_bias': 'grad_w', 'grad_d_skip': 'grad_w', 'grad_ssm_norm_g': 'grad_w', 'grad_w_b_out': 'grad_w', 'grad_w_o': 'grad_w', 'grad_ln1_g': 'grad_w', 'grad_ln1_b': 'grad_w', 'grad_w_gate_up': 'grad_w', 'grad_w_down': 'grad_w', 'grad_ln2_g': 'grad_w', 'grad_ln2_b': 'grad_w', 'grad_w_ple': 'grad_w', 'grad_ple_norm_g': 'grad_w', 'grad_w_ple_gate': 'grad_w', 'delta_w_in': 'delta_w', 'delta_conv_a_w': 'delta_w', 'delta_conv_a_b': 'delta_w', 'delta_ln_a_g': 'delta_w', 'delta_ln_a_b': 'delta_w', 'delta_w_a_out': 'delta_w', 'delta_ssm_conv_w': 'delta_w', 'delta_ssm_conv_b': 'delta_w', 'delta_a_log': 'delta_w', 'delta_dt_bias': 'delta_w', 'delta_d_skip': 'delta_w', 'delta_ssm_norm_g': 'delta_w', 'delta_w_b_out': 'delta_w', 'delta_w_o': 'delta_w', 'delta_ln1_g': 'delta_w', 'delta_ln1_b': 'delta_w', 'delta_w_gate_up': 'delta_w', 'delta_w_down': 'delta_w', 'delta_ln2_g': 'delta_w', 'delta_ln2_b': 'delta_w', 'delta_w_ple': 'delta_w', 'delta_ple_norm_g': 'delta_w', 'delta_w_ple_gate': 'delta_w', 'new_m_w_in': 'new_m', 'new_m_conv_a_w': 'new_m', 'new_m_conv_a_b': 'new_m', 'new_m_ln_a_g': 'new_m', 'new_m_ln_a_b': 'new_m', 'new_m_w_a_out': 'new_m', 'new_m_ssm_conv_w': 'new_m', 'new_m_ssm_conv_b': 'new_m', 'new_m_a_log': 'new_m', 'new_m_dt_bias': 'new_m', 'new_m_d_skip': 'new_m', 'new_m_ssm_norm_g': 'new_m', 'new_m_w_b_out': 'new_m', 'new_m_w_o': 'new_m', 'new_m_ln1_g': 'new_m', 'new_m_ln1_b': 'new_m', 'new_m_w_gate_up': 'new_m', 'new_m_w_down': 'new_m', 'new_m_ln2_g': 'new_m', 'new_m_ln2_b': 'new_m', 'new_m_w_ple': 'new_m', 'new_m_ple_norm_g': 'new_m', 'new_m_w_ple_gate': 'new_m', 'new_v_w_in': 'new_v', 'new_v_conv_a_w': 'new_v', 'new_v_conv_a_b': 'new_v', 'new_v_ln_a_g': 'new_v', 'new_v_ln_a_b': 'new_v', 'new_v_w_a_out': 'new_v', 'new_v_ssm_conv_w': 'new_v', 'new_v_ssm_conv_b': 'new_v', 'new_v_a_log': 'new_v', 'new_v_dt_bias': 'new_v', 'new_v_d_skip': 'new_v', 'new_v_ssm_norm_g': 'new_v', 'new_v_w_b_out': 'new_v', 'new_v_w_o': 'new_v', 'new_v_ln1_g': 'new_v', 'new_v_ln1_b': 'new_v', 'new_v_w_gate_up': 'new_v', 'new_v_w_down': 'new_v', 'new_v_ln2_g': 'new_v', 'new_v_ln2_b': 'new_v', 'new_v_w_ple': 'new_v', 'new_v_ple_norm_g': 'new_v', 'new_v_w_ple_gate': 'new_v'}


def _forward(args):
    return _fwd_reference(*[args[k] for k in FWD_PARAMS])


def _output_shape():
    def fwd():
        inp = _fwd_setup_inputs(0)
        return _fwd_reference(*[inp[k] for k in FWD_PARAMS])
    out = _jax.eval_shape(fwd)
    return out.shape, out.dtype

N_MICROBATCH = 1
ADAM_LR = 0.001
ADAM_B1 = 0.9
ADAM_B2 = 0.999
ADAM_EPS = 1e-08
ADAM_WD = 0.01
ADAM_STEP = 10
PER_EXAMPLE_BATCH_AXIS = {'x': 0, 'p': 1, 'loss_target': 0}
SHARED_INPUTS = []
_WEIGHT_DTYPES = {'w_in': _jnp.float32, 'conv_a_w': _jnp.float32, 'conv_a_b': _jnp.float32, 'ln_a_g': _jnp.float32, 'ln_a_b': _jnp.float32, 'w_a_out': _jnp.float32, 'ssm_conv_w': _jnp.float32, 'ssm_conv_b': _jnp.float32, 'a_log': _jnp.float32, 'dt_bias': _jnp.float32, 'd_skip': _jnp.float32, 'ssm_norm_g': _jnp.float32, 'w_b_out': _jnp.float32, 'w_o': _jnp.float32, 'ln1_g': _jnp.float32, 'ln1_b': _jnp.float32, 'w_gate_up': _jnp.float32, 'w_down': _jnp.float32, 'ln2_g': _jnp.float32, 'ln2_b': _jnp.float32, 'w_ple': _jnp.float32, 'ple_norm_g': _jnp.float32, 'w_ple_gate': _jnp.float32}
MOMENT_SCALE = {'w_in': 2.037708e-02, 'conv_a_w': 3.850679e-02, 'conv_a_b': 8.615009e-01, 'ln_a_g': 3.159956e-01, 'ln_a_b': 5.118918e-01, 'w_a_out': 3.336807e-01, 'ssm_conv_w': 4.188951e-02, 'ssm_conv_b': 1.869977e-01, 'a_log': 3.146235e-01, 'dt_bias': 4.600882e-02, 'd_skip': 1.039572e-01, 'ssm_norm_g': 1.039647e-01, 'w_b_out': 3.303611e-01, 'w_o': 4.785334e-01, 'ln1_g': 3.373928e+00, 'ln1_b': 1.266763e+01, 'w_gate_up': 5.487654e-02, 'w_down': 1.820383e-01, 'ln2_g': 9.330581e+01, 'ln2_b': 1.377338e+01, 'w_ple': 5.037971e-01, 'ple_norm_g': 2.649365e+01, 'w_ple_gate': 2.793818e-01}


def _to_microbatches(a, axis):
    t = _jnp.moveaxis(a, axis, 0)
    t = t.reshape((N_MICROBATCH, t.shape[0] // N_MICROBATCH) + t.shape[1:])
    return _jnp.moveaxis(t, 1, axis + 1)


def setup_inputs(seed: int = 0) -> dict:
    inp = _fwd_setup_inputs(seed)
    key = _jax.random.fold_in(_jax.random.key(seed), 7919)
    shape, _ = _output_shape()
    out = dict(inp)
    out["loss_target"] = _jax.random.normal(_jax.random.fold_in(key, 0), shape, _jnp.float32)
    for i, name in enumerate(TWIN_WEIGHTS):
        w = inp[name].astype(_jnp.float32)
        if MOMENT_SCALE is None:
            s = _jnp.sqrt(_jnp.mean(_jnp.square(w)) + 1e-30)
        else:
            s = MOMENT_SCALE[name]
        km, kv = _jax.random.split(_jax.random.fold_in(key, i + 1))
        out[name] = w
        out["m_" + name] = s * _jax.random.normal(km, w.shape, _jnp.float32)
        out["v_" + name] = (s * s) * _jax.random.uniform(kv, w.shape, _jnp.float32, 0.5, 1.5)
    if N_MICROBATCH > 1:
        for name, axis in PER_EXAMPLE_BATCH_AXIS.items():
            out[name] = _to_microbatches(out[name], axis)
    return {'x': out['x'], 'p': out['p'], 'w_in': out['w_in'], 'conv_a_w': out['conv_a_w'], 'conv_a_b': out['conv_a_b'], 'ln_a_g': out['ln_a_g'], 'ln_a_b': out['ln_a_b'], 'w_a_out': out['w_a_out'], 'ssm_conv_w': out['ssm_conv_w'], 'ssm_conv_b': out['ssm_conv_b'], 'a_log': out['a_log'], 'dt_bias': out['dt_bias'], 'd_skip': out['d_skip'], 'ssm_norm_g': out['ssm_norm_g'], 'w_b_out': out['w_b_out'], 'w_o': out['w_o'], 'ln1_g': out['ln1_g'], 'ln1_b': out['ln1_b'], 'w_gate_up': out['w_gate_up'], 'w_down': out['w_down'], 'ln2_g': out['ln2_g'], 'ln2_b': out['ln2_b'], 'w_ple': out['w_ple'], 'ple_norm_g': out['ple_norm_g'], 'w_ple_gate': out['w_ple_gate'], 'loss_target': out['loss_target'], 'm_w_in': out['m_w_in'], 'm_conv_a_w': out['m_conv_a_w'], 'm_conv_a_b': out['m_conv_a_b'], 'm_ln_a_g': out['m_ln_a_g'], 'm_ln_a_b': out['m_ln_a_b'], 'm_w_a_out': out['m_w_a_out'], 'm_ssm_conv_w': out['m_ssm_conv_w'], 'm_ssm_conv_b': out['m_ssm_conv_b'], 'm_a_log': out['m_a_log'], 'm_dt_bias': out['m_dt_bias'], 'm_d_skip': out['m_d_skip'], 'm_ssm_norm_g': out['m_ssm_norm_g'], 'm_w_b_out': out['m_w_b_out'], 'm_w_o': out['m_w_o'], 'm_ln1_g': out['m_ln1_g'], 'm_ln1_b': out['m_ln1_b'], 'm_w_gate_up': out['m_w_gate_up'], 'm_w_down': out['m_w_down'], 'm_ln2_g': out['m_ln2_g'], 'm_ln2_b': out['m_ln2_b'], 'm_w_ple': out['m_w_ple'], 'm_ple_norm_g': out['m_ple_norm_g'], 'm_w_ple_gate': out['m_w_ple_gate'], 'v_w_in': out['v_w_in'], 'v_conv_a_w': out['v_conv_a_w'], 'v_conv_a_b': out['v_conv_a_b'], 'v_ln_a_g': out['v_ln_a_g'], 'v_ln_a_b': out['v_ln_a_b'], 'v_w_a_out': out['v_w_a_out'], 'v_ssm_conv_w': out['v_ssm_conv_w'], 'v_ssm_conv_b': out['v_ssm_conv_b'], 'v_a_log': out['v_a_log'], 'v_dt_bias': out['v_dt_bias'], 'v_d_skip': out['v_d_skip'], 'v_ssm_norm_g': out['v_ssm_norm_g'], 'v_w_b_out': out['v_w_b_out'], 'v_w_o': out['v_w_o'], 'v_ln1_g': out['v_ln1_g'], 'v_ln1_b': out['v_ln1_b'], 'v_w_gate_up': out['v_w_gate_up'], 'v_w_down': out['v_w_down'], 'v_ln2_g': out['v_ln2_g'], 'v_ln2_b': out['v_ln2_b'], 'v_w_ple': out['v_w_ple'], 'v_ple_norm_g': out['v_ple_norm_g'], 'v_w_ple_gate': out['v_w_ple_gate']}


def _loss(weights, diff, rest, loss_target):
    with _jax.named_scope("forward"):
        args = {**rest, TWIN_DIFF_INPUT: diff, **{k: w.astype(_WEIGHT_DTYPES[k]) for k, w in weights.items()}}
        y = _forward(args)
    with _jax.named_scope("loss_head"):
        err = _jnp.square(y.astype(_jnp.float32) - loss_target)
        return 0.5 * _jnp.sum(_jnp.mean(err, axis=-1)) if err.ndim else 0.5 * err


def _adamw(w, g, m, v):
    m = ADAM_B1 * m + (1.0 - ADAM_B1) * g
    v = ADAM_B2 * v + (1.0 - ADAM_B2) * _jnp.square(g)
    m_hat = m / (1.0 - ADAM_B1 ** ADAM_STEP)
    v_hat = v / (1.0 - ADAM_B2 ** ADAM_STEP)
    delta = -ADAM_LR * (m_hat / (_jnp.sqrt(v_hat) + ADAM_EPS) + ADAM_WD * w)
    return delta, m, v


def reference(x, p, w_in, conv_a_w, conv_a_b, ln_a_g, ln_a_b, w_a_out, ssm_conv_w, ssm_conv_b, a_log, dt_bias, d_skip, ssm_norm_g, w_b_out, w_o, ln1_g, ln1_b, w_gate_up, w_down, ln2_g, ln2_b, w_ple, ple_norm_g, w_ple_gate, loss_target, m_w_in, m_conv_a_w, m_conv_a_b, m_ln_a_g, m_ln_a_b, m_w_a_out, m_ssm_conv_w, m_ssm_conv_b, m_a_log, m_dt_bias, m_d_skip, m_ssm_norm_g, m_w_b_out, m_w_o, m_ln1_g, m_ln1_b, m_w_gate_up, m_w_down, m_ln2_g, m_ln2_b, m_w_ple, m_ple_norm_g, m_w_ple_gate, v_w_in, v_conv_a_w, v_conv_a_b, v_ln_a_g, v_ln_a_b, v_w_a_out, v_ssm_conv_w, v_ssm_conv_b, v_a_log, v_dt_bias, v_d_skip, v_ssm_norm_g, v_w_b_out, v_w_o, v_ln1_g, v_ln1_b, v_w_gate_up, v_w_down, v_ln2_g, v_ln2_b, v_w_ple, v_ple_norm_g, v_w_ple_gate):
    given = dict(x=x, p=p, w_in=w_in, conv_a_w=conv_a_w, conv_a_b=conv_a_b, ln_a_g=ln_a_g, ln_a_b=ln_a_b, w_a_out=w_a_out, ssm_conv_w=ssm_conv_w, ssm_conv_b=ssm_conv_b, a_log=a_log, dt_bias=dt_bias, d_skip=d_skip, ssm_norm_g=ssm_norm_g, w_b_out=w_b_out, w_o=w_o, ln1_g=ln1_g, ln1_b=ln1_b, w_gate_up=w_gate_up, w_down=w_down, ln2_g=ln2_g, ln2_b=ln2_b, w_ple=w_ple, ple_norm_g=ple_norm_g, w_ple_gate=w_ple_gate, loss_target=loss_target, m_w_in=m_w_in, m_conv_a_w=m_conv_a_w, m_conv_a_b=m_conv_a_b, m_ln_a_g=m_ln_a_g, m_ln_a_b=m_ln_a_b, m_w_a_out=m_w_a_out, m_ssm_conv_w=m_ssm_conv_w, m_ssm_conv_b=m_ssm_conv_b, m_a_log=m_a_log, m_dt_bias=m_dt_bias, m_d_skip=m_d_skip, m_ssm_norm_g=m_ssm_norm_g, m_w_b_out=m_w_b_out, m_w_o=m_w_o, m_ln1_g=m_ln1_g, m_ln1_b=m_ln1_b, m_w_gate_up=m_w_gate_up, m_w_down=m_w_down, m_ln2_g=m_ln2_g, m_ln2_b=m_ln2_b, m_w_ple=m_w_ple, m_ple_norm_g=m_ple_norm_g, m_w_ple_gate=m_w_ple_gate, v_w_in=v_w_in, v_conv_a_w=v_conv_a_w, v_conv_a_b=v_conv_a_b, v_ln_a_g=v_ln_a_g, v_ln_a_b=v_ln_a_b, v_w_a_out=v_w_a_out, v_ssm_conv_w=v_ssm_conv_w, v_ssm_conv_b=v_ssm_conv_b, v_a_log=v_a_log, v_dt_bias=v_dt_bias, v_d_skip=v_d_skip, v_ssm_norm_g=v_ssm_norm_g, v_w_b_out=v_w_b_out, v_w_o=v_w_o, v_ln1_g=v_ln1_g, v_ln1_b=v_ln1_b, v_w_gate_up=v_w_gate_up, v_w_down=v_w_down, v_ln2_g=v_ln2_g, v_ln2_b=v_ln2_b, v_w_ple=v_w_ple, v_ple_norm_g=v_ple_norm_g, v_w_ple_gate=v_w_ple_gate)
    weights = {n: given[n] for n in TWIN_WEIGHTS}
    shared = {n: given[n] for n in SHARED_INPUTS}
    per_example = {n: given[n] for n in ['x', 'p']}
    grad_fn = _jax.value_and_grad(_loss, argnums=(0, 1))

    def one_microbatch(ex, loss_target):
        ex = dict(ex)
        diff = ex.pop(TWIN_DIFF_INPUT)
        return grad_fn(weights, diff, {**shared, **ex}, loss_target)

    if N_MICROBATCH == 1:
        loss, (grad_w, grad_x) = one_microbatch(per_example, given["loss_target"])
    else:
        def body(carry, xs):
            loss_sum, grad_sum = carry
            l_k, (gw_k, gx_k) = one_microbatch(xs[0], xs[1])
            with _jax.named_scope("update"):
                return (loss_sum + l_k, _jax.tree.map(_jnp.add, grad_sum, gw_k)), gx_k

        init = (_jnp.zeros((), _jnp.float32), _jax.tree.map(_jnp.zeros_like, weights))
        (loss, grad_w), grad_x = _jax.lax.scan(body, init, (per_example, given["loss_target"]))
    with _jax.named_scope("update"):
        delta_w, new_m, new_v = {}, {}, {}
        for n in TWIN_WEIGHTS:
            delta_w[n], new_m[n], new_v[n] = _adamw(weights[n], grad_w[n], given["m_" + n], given["v_" + n])
    return (loss, grad_x, *[grad_w[n] for n in TWIN_WEIGHTS], *[delta_w[n] for n in TWIN_WEIGHTS],
            *[new_m[n] for n in TWIN_WEIGHTS], *[new_v[n] for n in TWIN_WEIGHTS])
```

```python
import math

import jax
import jax.numpy as jnp
from jax import lax
from jax.experimental import pallas as pl
from jax.experimental.pallas import tpu as pltpu

F32 = jnp.float32
BF16 = jnp.bfloat16

D_MODEL = 1024
CONV_DIM = 1024
CONV_KERNEL = 31
D_INNER = 2048
HEAD_DIM = 64
N_HEADS = 32
N_GROUPS = 8
D_STATE = 128
SSM_CONV = 5
CHUNK = 128
XBC_DIM = D_INNER + 2 * N_GROUPS * D_STATE
FFN_DIM = 2816
PLE_DIM = 256
N_IN = 2 * CONV_DIM + 2 * D_MODEL + D_INNER + XBC_DIM + 2 * N_HEADS
N_IN_PAD = 10368
DEPTH = 2
N_DEV = 8
ALPHA = (2 * DEPTH) ** 0.25
LN_EPS = 1e-5
RMS_EPS = 1e-6
ADAM_LR, ADAM_B1, ADAM_B2, ADAM_EPS, ADAM_WD, ADAM_STEP = 0.001, 0.9, 0.999, 1e-08, 0.01, 10

LANE = 128
SUBLANE = 8
HALO = 16
VMEM_LIMIT = 52 * 1024 * 1024
NEG = -1e30

BIG = ["w_in", "conv_a_w", "w_a_out", "ssm_conv_w", "w_b_out", "w_o", "w_gate_up", "w_down", "w_ple", "w_ple_gate"]
BIG_AXIS = {"w_in": 2, "conv_a_w": 2, "w_a_out": 1, "ssm_conv_w": 2, "w_b_out": 1, "w_o": 1, "w_gate_up": 2,
            "w_down": 1, "w_ple": 2, "w_ple_gate": 1}
CONV_W = ["conv_a_w", "ssm_conv_w"]
SMALL = ["conv_a_b", "ln_a_g", "ln_a_b", "ssm_conv_b", "a_log", "dt_bias", "d_skip", "ssm_norm_g", "ln1_g", "ln1_b",
         "ln2_g", "ln2_b", "ple_norm_g"]
WEIGHTS = ["w_in", "conv_a_w", "conv_a_b", "ln_a_g", "ln_a_b", "w_a_out", "ssm_conv_w", "ssm_conv_b", "a_log", "dt_bias",
           "d_skip", "ssm_norm_g", "w_b_out", "w_o", "ln1_g", "ln1_b", "w_gate_up", "w_down", "ln2_g", "ln2_b", "w_ple",
           "ple_norm_g", "w_ple_gate"]


def _cparams(sem):
    return pltpu.CompilerParams(dimension_semantics=sem, vmem_limit_bytes=VMEM_LIMIT)


def _pick(n, cap):
    if n <= cap:
        return n
    best = None
    for d in range(LANE, cap + 1, LANE):
        if n % d == 0:
            best = d
    assert best is not None, (n, cap)
    return best


def matmul(name, a, b, mode, out_dtype=F32, add=None):
    if mode == "nn":
        (M, K), (K2, N) = a.shape, b.shape
    elif mode == "nt":
        (M, K), (N, K2) = a.shape, b.shape
    else:
        (K, M), (K2, N) = a.shape, b.shape
    assert K == K2, (name, a.shape, b.shape)
    tm = _pick(M, 1024) if mode != "tn" else _pick(M, 1408)
    tn = _pick(N, 1408)
    tk = _pick(K, 512) if mode == "tn" else _pick(K, 2816 if mode == "nn" else 1408)
    nk = K // tk
    grid = (M // tm, N // tn, nk)
    if mode == "tn":
        a_spec = pl.BlockSpec((tk, tm), lambda i, j, k: (k, i))
    else:
        a_spec = pl.BlockSpec((tm, tk), lambda i, j, k: (i, k))
    if mode == "nt":
        b_spec = pl.BlockSpec((tn, tk), lambda i, j, k: (j, k))
    else:
        b_spec = pl.BlockSpec((tk, tn), lambda i, j, k: (k, j))
    o_spec = pl.BlockSpec((tm, tn), lambda i, j, k: (i, j))
    dims = {"nn": ((1,), (0,)), "nt": ((1,), (1,)), "tn": ((0,), (0,))}[mode]
    has_add = add is not None

    def body(a_ref, b_ref, *rest):
        if has_add:
            add_ref, o_ref, *scr = rest
        else:
            o_ref, *scr = rest
        part = lax.dot_general(a_ref[...].astype(BF16), b_ref[...].astype(BF16), (dims, ((), ())),
                               preferred_element_type=F32)

        def finish(v):
            if has_add:
                v = v + add_ref[...].astype(F32)
            o_ref[...] = v.astype(o_ref.dtype)

        if nk == 1:
            finish(part)
        else:
            acc = scr[0]
            k = pl.program_id(2)

            @pl.when(k == 0)
            def _():
                acc[...] = part

            @pl.when(k > 0)
            def _():
                acc[...] += part

            @pl.when(k == nk - 1)
            def _():
                finish(acc[...])

    in_specs = [a_spec, b_spec] + ([o_spec] if has_add else [])
    args = (a, b) + ((add,) if has_add else ())
    return pl.pallas_call(
        body, out_shape=jax.ShapeDtypeStruct((M, N), out_dtype), grid=grid, in_specs=in_specs, out_specs=o_spec,
        scratch_shapes=[pltpu.VMEM((tm, tn), F32)] if nk > 1 else [], name=name,
        compiler_params=_cparams(("parallel", "parallel", "arbitrary")))(*args)


def _row_specs(items, tT, groups):
    return [pl.BlockSpec((tT, w), (lambda g, i, b=blk: (i, b + g))) for (_, w, blk) in items]


def _par_specs(pars, groups):
    specs = []
    for p in pars:
        if groups > 1:
            specs.append(pl.BlockSpec((p.shape[0], p.shape[1] // groups), lambda g, i: (0, g)))
        else:
            specs.append(pl.BlockSpec(p.shape, lambda g, i: (0, 0)))
    return specs


def rowcall(name, fn, ins, pars, outs, T, tT=256, groups=1):
    n_in, n_par = len(ins), len(pars)
    intos = [o[2] for o in outs if o[2] is not None]
    in_specs = _row_specs(ins, tT, groups) + _par_specs(pars, groups) + [pl.BlockSpec(memory_space=pl.ANY)] * len(intos)
    out_specs, out_shapes, aliases = [], [], {}
    n_alias = 0
    for oi, (w, dt, into) in enumerate(outs):
        if into is None:
            out_specs.append(pl.BlockSpec((tT, w), lambda g, i: (i, g)))
            out_shapes.append(jax.ShapeDtypeStruct((T, w * groups), dt))
        else:
            arr, blk = into
            out_specs.append(pl.BlockSpec((tT, w), lambda g, i, b=blk: (i, b + g)))
            out_shapes.append(jax.ShapeDtypeStruct(arr.shape, arr.dtype))
            aliases[n_in + n_par + n_alias] = oi
            n_alias += 1

    def body(*refs):
        xs = [r[...].astype(F32) for r in refs[:n_in]]
        ps = [r[...] for r in refs[n_in:n_in + n_par]]
        o_refs = refs[n_in + n_par + n_alias:]
        res = fn(*xs, *ps)
        for r, v in zip(o_refs, res):
            r[...] = v.astype(r.dtype)

    res = pl.pallas_call(
        body, out_shape=out_shapes, grid=(groups, T // tT), in_specs=in_specs, out_specs=out_specs,
        input_output_aliases=aliases, name=name, compiler_params=_cparams(("parallel", "parallel")))(
            *[a for (a, _, _) in ins], *pars, *[a for (a, _) in intos])
    return list(res)


def rowvjp(name, fn, ins, pars, cts, douts, T, tT=256, groups=1, par_grads=True):
    n_in, n_par, n_ct = len(ins), len(pars), len(cts)
    intos = [o[2] for o in douts if o[2] is not None]
    in_specs = (_row_specs(ins, tT, groups) + _par_specs(pars, groups) + _row_specs(cts, tT, groups)
                + [pl.BlockSpec(memory_space=pl.ANY)] * len(intos))
    out_specs, out_shapes, aliases = [], [], {}
    n_alias = 0
    for oi, (idxs, dt, into) in enumerate(douts):
        w = sum(ins[k][1] for k in idxs)
        if into is None:
            out_specs.append(pl.BlockSpec((tT, w), lambda g, i: (i, g)))
            out_shapes.append(jax.ShapeDtypeStruct((T, w * groups), dt))
        else:
            arr, blk = into
            out_specs.append(pl.BlockSpec((tT, w), lambda g, i, b=blk: (i, b + g)))
            out_shapes.append(jax.ShapeDtypeStruct(arr.shape, arr.dtype))
            aliases[n_in + n_par + n_ct + n_alias] = oi
            n_alias += 1
    n_dout = len(douts)
    if par_grads:
        for p in pars:
            if groups > 1:
                out_specs.append(pl.BlockSpec((p.shape[0], p.shape[1] // groups), lambda g, i: (0, g)))
            else:
                out_specs.append(pl.BlockSpec(p.shape, lambda g, i: (0, 0)))
            out_shapes.append(jax.ShapeDtypeStruct(p.shape, F32))

    def body(*refs):
        xs = [r[...].astype(F32) for r in refs[:n_in]]
        ps = [r[...] for r in refs[n_in:n_in + n_par]]
        ct_refs = refs[n_in + n_par:n_in + n_par + n_ct]
        o_refs = refs[n_in + n_par + n_ct + n_alias:]
        res, vjp_fn = jax.vjp(fn, *xs, *ps)
        grads = vjp_fn(tuple(r[...].astype(F32) for r in ct_refs))
        for r, (idxs, _, _) in zip(o_refs[:n_dout], douts):
            parts = [grads[k] for k in idxs]
            v = parts[0] if len(parts) == 1 else jnp.concatenate(parts, axis=1)
            r[...] = v.astype(r.dtype)
        if par_grads:
            i = pl.program_id(1)
            first = (i == 0) if groups > 1 else jnp.logical_and(i == 0, pl.program_id(0) == 0)
            for r, gp in zip(o_refs[n_dout:], grads[n_in:]):
                @pl.when(first)
                def _(r=r):
                    r[...] = jnp.zeros(r.shape, F32)
                r[...] += gp

    res = pl.pallas_call(
        body, out_shape=out_shapes, grid=(groups, T // tT), in_specs=in_specs, out_specs=out_specs,
        input_output_aliases=aliases, name=name, compiler_params=_cparams(("arbitrary", "arbitrary")))(
            *[a for (a, _, _) in ins], *pars, *[a for (a, _, _) in cts], *[a for (a, _) in intos])
    res = list(res)
    return res[:n_dout], res[n_dout:]


def _sigmoid(x):
    return 1.0 / (1.0 + jnp.exp(-x))


def _silu(x):
    return x * _sigmoid(x)


def _softplus(x):
    return jnp.maximum(x, 0.0) + jnp.log(1.0 + jnp.exp(-jnp.abs(x)))


def _ln(x, g, b):
    mu = jnp.mean(x, axis=-1, keepdims=True)
    xc = x - mu
    var = jnp.mean(xc * xc, axis=-1, keepdims=True)
    return xc * lax.rsqrt(var + LN_EPS) * g + b


def glu_fn(a, gt):
    return (a * _sigmoid(gt),)


def lnsilu_fn(u, g, b):
    return (_silu(_ln(u, g, b)),)


def gnorm_fn(yf, yb, xp, z, dsk, ng):
    y = (yf + yb + _silu(xp) * dsk) * _silu(z)
    return (y * lax.rsqrt(jnp.mean(y * y, axis=-1, keepdims=True) + RMS_EPS) * ng,)


def merge_fn(ga, gb, ya, yb):
    return (_sigmoid(ga) * ya + _sigmoid(gb) * yb,)


def resln_fn(x, r, g, b):
    return (_ln(ALPHA * x + r, g, b),)


def swiglu_fn(g, u):
    return (_silu(g) * u,)


def ple_fn(h2, pe, gl, g):
    e = pe * lax.rsqrt(jnp.mean(pe * pe, axis=-1, keepdims=True) + RMS_EPS) * g
    return (h2 + e * _sigmoid(gl),)


def ident_fn(v):
    return (v,)


CONV_CB = 512
CONV_TT = 512
CONV_RB = 16


def _conv_specs(blk0, T, tT, cb):
    nh = tT // HALO
    cur = pl.BlockSpec((tT, cb), lambda j, i: (i, blk0 + j))
    prev = pl.BlockSpec((HALO, cb), lambda j, i: (jnp.maximum(i * nh - 1, 0), blk0 + j))
    nxt = pl.BlockSpec((HALO, cb), lambda j, i: (jnp.minimum((i + 1) * nh, T // HALO - 1), blk0 + j))
    return [prev, cur, nxt]


def _fill_shifted(sh_ref, pad_ref, prev_ref, cur_ref, next_ref, i, n_t, tT):
    pad_ref[pl.ds(0, HALO), :] = prev_ref[...].astype(F32) * (i > 0).astype(F32)
    pad_ref[pl.ds(HALO, tT), :] = cur_ref[...].astype(F32)
    pad_ref[pl.ds(HALO + tT, HALO), :] = next_ref[...].astype(F32) * (i < n_t - 1).astype(F32)
    for ph in range(SUBLANE):
        sh_ref[ph] = pad_ref[pl.ds(ph, tT + 3 * SUBLANE), :]


def conv_fwd(name, u, blk0, w, b, T):
    K, C = w.shape
    P = (K - 1) // 2
    tT, cb = min(CONV_TT, T), CONV_CB
    n_t = T // tT

    def body(prev_ref, cur_ref, next_ref, w_ref, b_ref, o_ref, pad_ref, sh_ref):
        i = pl.program_id(1)
        _fill_shifted(sh_ref, pad_ref, prev_ref, cur_ref, next_ref, i, n_t, tT)

        def rows(r, carry):
            base = pl.multiple_of(r * CONV_RB, CONV_RB)
            acc = jnp.zeros((CONV_RB, cb), F32) + b_ref[...]
            for k in range(K):
                q, ph = divmod(HALO - P + k, SUBLANE)
                acc = acc + sh_ref[ph, pl.ds(base + SUBLANE * q, CONV_RB), :] * w_ref[k:k + 1, :]
            o_ref[pl.ds(base, CONV_RB), :] = acc
            return carry

        lax.fori_loop(0, tT // CONV_RB, rows, 0)

    return pl.pallas_call(
        body, out_shape=jax.ShapeDtypeStruct((T, C), F32), grid=(C // cb, n_t),
        in_specs=_conv_specs(blk0, T, tT, cb) + [pl.BlockSpec((K, cb), lambda j, i: (0, j)),
                                                  pl.BlockSpec((1, cb), lambda j, i: (0, j))],
        out_specs=pl.BlockSpec((tT, cb), lambda j, i: (i, j)),
        scratch_shapes=[pltpu.VMEM((tT + 2 * HALO, cb), F32), pltpu.VMEM((SUBLANE, tT + 3 * SUBLANE, cb), F32)],
        name=name, compiler_params=_cparams(("parallel", "arbitrary")))(u, u, u, w, b)


def conv_bwd(name, dy, u, blk0, w, T, into=None):
    K, C = w.shape
    P = (K - 1) // 2
    tT, cb = min(CONV_TT, T), CONV_CB
    n_t = T // tT

    def body(dprev, dcur, dnext, uprev, ucur, unext, w_ref, *rest):
        if into is not None:
            rest = rest[1:]
        du_ref, dw_ref, db_ref, pad_ref, shd_ref, shu_ref = rest
        i = pl.program_id(1)
        _fill_shifted(shd_ref, pad_ref, dprev, dcur, dnext, i, n_t, tT)
        _fill_shifted(shu_ref, pad_ref, uprev, ucur, unext, i, n_t, tT)

        def rows(r, carry):
            base = pl.multiple_of(r * CONV_RB, CONV_RB)
            acc = jnp.zeros((CONV_RB, cb), F32)
            for k in range(K):
                q, ph = divmod(HALO + P - k, SUBLANE)
                acc = acc + shd_ref[ph, pl.ds(base + SUBLANE * q, CONV_RB), :] * w_ref[k:k + 1, :]
            du_ref[pl.ds(base, CONV_RB), :] = acc.astype(du_ref.dtype)
            return carry

        lax.fori_loop(0, tT // CONV_RB, rows, 0)

        @pl.when(i == 0)
        def _():
            dw_ref[...] = jnp.zeros(dw_ref.shape, F32)
            db_ref[...] = jnp.zeros(db_ref.shape, F32)

        q0, ph0 = divmod(HALO, SUBLANE)
        for k in range(K + 1):
            q, ph = divmod(HALO - P + k, SUBLANE) if k < K else (q0, ph0)

            def red(r, acc, q=q, ph=ph, k=k):
                base = pl.multiple_of(r * CONV_RB, CONV_RB)
                d = shd_ref[ph0, pl.ds(base + SUBLANE * q0, CONV_RB), :]
                if k == K:
                    return acc + d
                return acc + d * shu_ref[ph, pl.ds(base + SUBLANE * q, CONV_RB), :]

            tot = jnp.sum(lax.fori_loop(0, tT // CONV_RB, red, jnp.zeros((CONV_RB, cb), F32)), axis=0, keepdims=True)
            if k == K:
                db_ref[...] += tot
            else:
                dw_ref[k:k + 1, :] += tot

    dspecs = _conv_specs(0, T, tT, cb)
    uspecs = _conv_specs(blk0, T, tT, cb)
    in_specs = dspecs + uspecs + [pl.BlockSpec((K, cb), lambda j, i: (0, j))]
    args = [dy, dy, dy, u, u, u, w]
    aliases = {}
    if into is None:
        du_spec = pl.BlockSpec((tT, cb), lambda j, i: (i, j))
        du_shape = jax.ShapeDtypeStruct((T, C), F32)
    else:
        arr, oblk = into
        in_specs.append(pl.BlockSpec(memory_space=pl.ANY))
        args.append(arr)
        aliases = {7: 0}
        du_spec = pl.BlockSpec((tT, cb), lambda j, i: (i, oblk + j))
        du_shape = jax.ShapeDtypeStruct(arr.shape, arr.dtype)
    return pl.pallas_call(
        body, out_shape=[du_shape, jax.ShapeDtypeStruct((K, C), F32), jax.ShapeDtypeStruct((1, C), F32)],
        grid=(C // cb, n_t), in_specs=in_specs,
        out_specs=[du_spec, pl.BlockSpec((K, cb), lambda j, i: (0, j)), pl.BlockSpec((1, cb), lambda j, i: (0, j))],
        scratch_shapes=[pltpu.VMEM((tT + 2 * HALO, cb), F32), pltpu.VMEM((SUBLANE, tT + 3 * SUBLANE, cb), F32),
                        pltpu.VMEM((SUBLANE, tT + 3 * SUBLANE, cb), F32)],
        input_output_aliases=aliases, name=name, compiler_params=_cparams(("arbitrary", "arbitrary")))(*args)


def _dot(a, b, dims):
    return lax.dot_general(a.astype(BF16), b.astype(BF16), (dims, ((), ())), preferred_element_type=F32)


def _dnn(a, b):
    return _dot(a, b, ((1,), (0,)))


def _dnt(a, b):
    return _dot(a, b, ((1,), (1,)))


def _dtn(a, b):
    return _dot(a.T, b, ((1,), (0,)))


@jax.custom_vjp
def _nn(a, b):
    return _dnn(a, b)


_nn.defvjp(lambda a, b: (_dnn(a, b), (a, b)), lambda r, g: (_dnt(g, r[1]), _dtn(r[0], g)))


@jax.custom_vjp
def _nt(a, b):
    return _dnt(a, b)


_nt.defvjp(lambda a, b: (_dnt(a, b), (a, b)), lambda r, g: (_dnn(g, r[1]), _dtn(g, r[0])))


@jax.custom_vjp
def _tn(a, b):
    return _dtn(a, b)


_tn.defvjp(lambda a, b: (_dtn(a, b), (a, b)), lambda r, g: (_dnt(r[1], g), _dnn(r[0], g)))


def _split_dot(m, v):
    hi = v.astype(BF16)
    r1 = v - hi.astype(F32)
    mid = r1.astype(BF16)
    lo = (r1 - mid.astype(F32)).astype(BF16)
    mb = m.astype(BF16)
    d = lambda x: lax.dot_general(mb, x, (((1,), (0,)), ((), ())), preferred_element_type=F32)
    return d(hi) + d(mid) + d(lo)


@jax.custom_vjp
def _tri_dot(tri, tri_t, v):
    return _split_dot(tri, v)


_tri_dot.defvjp(lambda tri, tri_t, v: (_split_dot(tri, v), (tri, tri_t)),
                lambda r, g: (jnp.zeros_like(r[0]), jnp.zeros_like(r[1]), _split_dot(r[1], g)))


def _ssd_consts(dirn):
    ri = lax.broadcasted_iota(jnp.int32, (CHUNK, CHUNK), 0)
    ci = lax.broadcasted_iota(jnp.int32, (CHUNK, CHUNK), 1)
    keep = (ci <= ri) if dirn == 0 else (ci >= ri)
    tri = keep.astype(F32)
    tri_t = (~keep | (ci == ri)).astype(F32)
    lane = lax.broadcasted_iota(jnp.int32, (1, LANE), 1)
    sub = lax.broadcasted_iota(jnp.int32, (CHUNK, 1), 0)
    end = (sub == (CHUNK - 1 if dirn == 0 else 0)).astype(F32)
    lo_half = lane < HEAD_DIM
    oh_lane = [(lane == N_HEADS * dirn + h).astype(F32) for h in range(N_HEADS)]
    oh_sub = [(sub == N_HEADS * dirn + h).astype(F32) for h in range(N_HEADS)]
    return keep, tri, tri_t, end, lo_half, oh_lane, oh_sub


def _ssd_chunk(consts, x_t, b_t, c_t, dtr, bias, alog, h_t):
    keep, tri, tri_t, end, lo_half, oh_lane, oh_sub = consts
    dt = _softplus(dtr + bias)
    a = dt * (-jnp.exp(alog))
    cs = _tri_dot(tri, tri_t, a)
    cs_t = cs.T
    tot = jnp.sum(cs * end, axis=0, keepdims=True)
    ys, hn = [], []
    for g in range(N_GROUPS):
        bm, cm = _silu(b_t[g]), _silu(c_t[g])
        gm = _nt(cm, bm)
        for jj in range(2):
            j = 2 * g + jj
            h0, h1 = 2 * j, 2 * j + 1
            col = [jnp.sum(cs * oh_lane[h], axis=1, keepdims=True) for h in (h0, h1)]
            row = [jnp.sum(cs_t * oh_sub[h], axis=0, keepdims=True) for h in (h0, h1)]
            dth = [jnp.sum(dt * oh_lane[h], axis=1, keepdims=True) for h in (h0, h1)]
            toth = [jnp.sum(tot * oh_lane[h], axis=1, keepdims=True) for h in (h0, h1)]
            xd = _silu(x_t[j]) * jnp.where(lo_half, dth[0], dth[1])
            yd = [_nn(gm * jnp.exp(jnp.where(keep, col[k] - row[k], NEG)), xd) for k in range(2)]
            cp = jnp.where(lo_half, col[0], col[1])
            tp = jnp.where(lo_half, toth[0], toth[1])
            ys.append(jnp.where(lo_half, yd[0], yd[1]) + _nn(cm, h_t[j]) * jnp.exp(cp))
            hn.append(h_t[j] * jnp.exp(tp) + _tn(bm, xd * jnp.exp(tp - cp)))
    return ys, hn


N_PAIR = D_INNER // LANE


def _tiles(ref, n):
    return [ref[:, LANE * j:LANE * (j + 1)].astype(F32) for j in range(n)]


def _ssd_in_specs(cmap):
    return [pl.BlockSpec((CHUNK, D_INNER), lambda i: (cmap(i), 0)),
            pl.BlockSpec((CHUNK, N_GROUPS * D_STATE), lambda i: (cmap(i), 2)),
            pl.BlockSpec((CHUNK, N_GROUPS * D_STATE), lambda i: (cmap(i), 3)),
            pl.BlockSpec((CHUNK, LANE), lambda i: (cmap(i), (N_IN_PAD - LANE) // LANE)),
            pl.BlockSpec((1, LANE), lambda i: (0, 0)), pl.BlockSpec((1, LANE), lambda i: (0, 0))]


def ssd_fwd(name, xbc, proj, bias_row, alog_row, dirn, T):
    nc = T // CHUNK
    cmap = (lambda i: i) if dirn == 0 else (lambda i: nc - 1 - i)

    def body(x_ref, b_ref, c_ref, dt_ref, bias_ref, alog_ref, y_ref, hs_ref, h_scr):
        @pl.when(pl.program_id(0) == 0)
        def _():
            h_scr[...] = jnp.zeros(h_scr.shape, F32)

        hs_ref[0] = h_scr[...]
        ys, hn = _ssd_chunk(_ssd_consts(dirn), _tiles(x_ref, N_PAIR), _tiles(b_ref, N_GROUPS), _tiles(c_ref, N_GROUPS),
                            dt_ref[...], bias_ref[...], alog_ref[...], _tiles(h_scr, N_PAIR))
        for j in range(N_PAIR):
            y_ref[:, LANE * j:LANE * (j + 1)] = ys[j]
            h_scr[:, LANE * j:LANE * (j + 1)] = hn[j]

    return pl.pallas_call(
        body, out_shape=[jax.ShapeDtypeStruct((T, D_INNER), F32), jax.ShapeDtypeStruct((nc, D_STATE, D_INNER), F32)],
        grid=(nc,), in_specs=_ssd_in_specs(cmap),
        out_specs=[pl.BlockSpec((CHUNK, D_INNER), lambda i: (cmap(i), 0)),
                   pl.BlockSpec((1, D_STATE, D_INNER), lambda i: (cmap(i), 0, 0))],
        scratch_shapes=[pltpu.VMEM((D_STATE, D_INNER), F32)], name=name,
        compiler_params=_cparams(("arbitrary",)))(xbc, xbc, xbc, proj, bias_row, alog_row)


def ssd_bwd(name, xbc, proj, bias_row, alog_row, hs, dy, adds, dirn, T):
    nc = T // CHUNK
    cmap = (lambda i: nc - 1 - i) if dirn == 0 else (lambda i: i)
    GS = N_GROUPS * D_STATE

    def body(x_ref, b_ref, c_ref, dt_ref, bias_ref, alog_ref, hs_ref, dy_ref, ax_ref, ab_ref, ac_ref, adt_ref,
             dx_ref, db_ref, dc_ref, ddt_ref, dbias_ref, dalog_ref, dh_scr):
        first = pl.program_id(0) == 0

        @pl.when(first)
        def _():
            dh_scr[...] = jnp.zeros(dh_scr.shape, F32)
            dbias_ref[...] = jnp.zeros(dbias_ref.shape, F32)
            dalog_ref[...] = jnp.zeros(dalog_ref.shape, F32)

        consts = _ssd_consts(dirn)
        fn = lambda *a: _ssd_chunk(consts, *a)
        _, vjp_fn = jax.vjp(fn, _tiles(x_ref, N_PAIR), _tiles(b_ref, N_GROUPS), _tiles(c_ref, N_GROUPS), dt_ref[...],
                            bias_ref[...], alog_ref[...], [hs_ref[0, :, LANE * j:LANE * (j + 1)] for j in range(N_PAIR)])
        dx, db, dc, ddt, dbias, dalog, dh = vjp_fn((_tiles(dy_ref, N_PAIR), _tiles(dh_scr, N_PAIR)))
        for j in range(N_PAIR):
            s = slice(LANE * j, LANE * (j + 1))
            dx_ref[:, s] = dx[j] + ax_ref[:, s]
            dh_scr[:, s] = dh[j]
        for g in range(N_GROUPS):
            s = slice(LANE * g, LANE * (g + 1))
            db_ref[:, s] = db[g] + ab_ref[:, s]
            dc_ref[:, s] = dc[g] + ac_ref[:, s]
        ddt_ref[...] = ddt + adt_ref[...]
        dbias_ref[...] += dbias
        dalog_ref[...] += dalog

    blk = lambda w: pl.BlockSpec((CHUNK, w), lambda i: (cmap(i), 0))
    row = pl.BlockSpec((1, LANE), lambda i: (0, 0))
    return pl.pallas_call(
        body,
        out_shape=[jax.ShapeDtypeStruct((T, D_INNER), F32), jax.ShapeDtypeStruct((T, GS), F32),
                   jax.ShapeDtypeStruct((T, GS), F32), jax.ShapeDtypeStruct((T, LANE), F32),
                   jax.ShapeDtypeStruct((1, LANE), F32), jax.ShapeDtypeStruct((1, LANE), F32)],
        grid=(nc,),
        in_specs=_ssd_in_specs(cmap) + [pl.BlockSpec((1, D_STATE, D_INNER), lambda i: (cmap(i), 0, 0)), blk(D_INNER),
                                        blk(D_INNER), blk(GS), blk(GS), blk(LANE)],
        out_specs=[blk(D_INNER), blk(GS), blk(GS), blk(LANE), row, row],
        scratch_shapes=[pltpu.VMEM((D_STATE, D_INNER), F32)], name=name,
        compiler_params=_cparams(("arbitrary",)))(xbc, xbc, xbc, proj, bias_row, alog_row, hs, dy, *adds)


def loss_head(y, target, T, tT=256):
    def body(y_ref, t_ref, dy_ref, sq_ref):
        @pl.when(pl.program_id(0) == 0)
        def _():
            sq_ref[...] = jnp.zeros(sq_ref.shape, F32)
        e = y_ref[...] - t_ref[...]
        dy_ref[...] = e * (1.0 / D_MODEL)
        sq_ref[...] += jnp.sum(e * e, axis=0, keepdims=True)

    spec = pl.BlockSpec((tT, D_MODEL), lambda i: (i, 0))
    return pl.pallas_call(
        body, out_shape=[jax.ShapeDtypeStruct((T, D_MODEL), F32), jax.ShapeDtypeStruct((1, D_MODEL), F32)],
        grid=(T // tT,), in_specs=[spec, spec], out_specs=[spec, pl.BlockSpec((1, D_MODEL), lambda i: (0, 0))],
        name="loss_head", compiler_params=_cparams(("arbitrary",)))(y, target)


MESH_ID = pl.DeviceIdType.MESH


def all_gather(name, v):
    R, W = v.shape

    def body(v_ref, out_ref, send_sems, recv_sems, local_sem):
        x, y, c = lax.axis_index("x"), lax.axis_index("y"), lax.axis_index("c")
        me, sibling = (x, y, c), (x, y, 1 - c)
        chips = [(1 - x, y), (x, 1 - y), (1 - x, 1 - y)]

        def slot(px, py, pc):
            return out_ref.at[4 * px + 2 * py + pc]

        def copy(k, block, to, src=None):
            return pltpu.make_async_remote_copy(
                src_ref=slot(*block) if src is None else src, dst_ref=slot(*block), send_sem=send_sems.at[k],
                recv_sem=recv_sems.at[k], device_id=to, device_id_type=MESH_ID)

        mine = pltpu.make_async_copy(v_ref, slot(*me), local_sem)
        mine.start()
        first = [copy(0, me, sibling, src=v_ref)]
        first += [copy(1 + j, me, (*chip, c), src=v_ref) for j, chip in enumerate(chips)]
        for cp in first:
            cp.start()
        passed = [copy(4 + j, (*chip, c), sibling) for j, chip in enumerate(chips)]
        for j, chip in enumerate(chips):
            copy(1 + j, (*chip, c), me).wait_recv()
            passed[j].start()
        copy(0, sibling, me).wait_recv()
        for j, chip in enumerate(chips):
            copy(4 + j, (*chip, 1 - c), me).wait_recv()
        for cp in first + passed:
            cp.wait_send()
        mine.wait()

    return pl.pallas_call(
        body, out_shape=jax.ShapeDtypeStruct((N_DEV, R, W), v.dtype),
        in_specs=[pl.BlockSpec(memory_space=pl.ANY)], out_specs=pl.BlockSpec(memory_space=pl.ANY),
        scratch_shapes=[pltpu.SemaphoreType.DMA((7,)), pltpu.SemaphoreType.DMA((7,)), pltpu.SemaphoreType.DMA],
        name=name, compiler_params=pltpu.CompilerParams(has_side_effects=True))(v)


def grad_exchange(name, g):
    _, R, W = g.shape

    def body(g_ref, out_ref, send_sems, recv_sems, local_sem):
        x, y, c = lax.axis_index("x"), lax.axis_index("y"), lax.axis_index("c")
        me = 4 * x + 2 * y + c
        mine = pltpu.make_async_copy(g_ref.at[me], out_ref.at[me], local_sem)
        mine.start()
        copies = []
        for k in range(1, N_DEV):
            kx, ky, kc = (k >> 2) & 1, (k >> 1) & 1, k & 1
            px = 1 - x if kx else x
            py = 1 - y if ky else y
            pc = 1 - c if kc else c
            copies.append(pltpu.make_async_remote_copy(
                src_ref=g_ref.at[4 * px + 2 * py + pc], dst_ref=out_ref.at[me], send_sem=send_sems.at[k - 1],
                recv_sem=recv_sems.at[k - 1], device_id=(px, py, pc), device_id_type=MESH_ID))
        for cp in copies:
            cp.start()
        for cp in copies:
            cp.wait_recv()
        for cp in copies:
            cp.wait_send()
        mine.wait()

    return pl.pallas_call(
        body, out_shape=jax.ShapeDtypeStruct(g.shape, g.dtype),
        in_specs=[pl.BlockSpec(memory_space=pl.ANY)], out_specs=pl.BlockSpec(memory_space=pl.ANY),
        scratch_shapes=[pltpu.SemaphoreType.DMA((7,)), pltpu.SemaphoreType.DMA((7,)), pltpu.SemaphoreType.DMA],
        name=name, compiler_params=pltpu.CompilerParams(has_side_effects=True))(g)


def adamw(name, parts, w, m, v, tr):
    R, W = w.shape
    c1 = 1.0 / (1.0 - ADAM_B1 ** ADAM_STEP)
    c2 = 1.0 / (1.0 - ADAM_B2 ** ADAM_STEP)

    def body(p_ref, w_ref, m_ref, v_ref, g_ref, d_ref, nm_ref, nv_ref):
        g = p_ref[0]
        for s in range(1, N_DEV):
            g = g + p_ref[s]
        nm = ADAM_B1 * m_ref[...] + (1.0 - ADAM_B1) * g
        nv = ADAM_B2 * v_ref[...] + (1.0 - ADAM_B2) * (g * g)
        g_ref[...] = g
        nm_ref[...] = nm
        nv_ref[...] = nv
        d_ref[...] = -ADAM_LR * ((nm * c1) / (jnp.sqrt(nv * c2) + ADAM_EPS) + ADAM_WD * w_ref[...])

    spec = pl.BlockSpec((tr, W), lambda i: (i, 0))
    return pl.pallas_call(
        body, out_shape=[jax.ShapeDtypeStruct((R, W), F32)] * 4, grid=(R // tr,),
        in_specs=[pl.BlockSpec((N_DEV, tr, W), lambda i: (0, i, 0)), spec, spec, spec], out_specs=[spec] * 4,
        name=name, compiler_params=_cparams(("parallel",)))(parts, w, m, v)


def _pack(arrs, row_mult):
    flat = jnp.concatenate([a.reshape(-1) for a in arrs])
    n = flat.shape[0]
    rows = -(-n // LANE)
    rows = -(-rows // row_mult) * row_mult
    return jnp.pad(flat, (0, rows * LANE - n)).reshape(rows, LANE)


def _unpack(buf, shapes, lead=()):
    flat = buf.reshape(lead + (-1,))
    out, off = [], 0
    for s in shapes:
        n = math.prod(s)
        out.append(flat[..., off:off + n].reshape(lead + tuple(s)))
        off += n
    return out


def _full_from_gathered(name, g):
    if BIG_AXIS[name] == 1:
        return jnp.transpose(g, (1, 0, 2, 3)).reshape(g.shape[1], N_DEV * g.shape[2], g.shape[3])
    return jnp.transpose(g, (1, 2, 0, 3)).reshape(g.shape[1], g.shape[2], N_DEV * g.shape[3])


def _pieces_from_full(name, f):
    L, A, B = f.shape
    if BIG_AXIS[name] == 1:
        return jnp.transpose(f.reshape(L, N_DEV, A // N_DEV, B), (1, 0, 2, 3))
    return jnp.transpose(f.reshape(L, A, N_DEV, B // N_DEV), (2, 0, 1, 3))


def _rows(v):
    return v.reshape(1, -1).astype(F32)


def _head_rows(W):
    bias = jnp.pad(W["dt_bias"].reshape(1, -1), ((0, 0), (0, LANE - 2 * N_HEADS)))
    alog = jnp.pad(W["a_log"].reshape(1, -1), ((0, 0), (0, LANE - 2 * N_HEADS)))
    return bias, alog


def layer_fwd(li, x, p_l, W, T):
    n = lambda s: f"l{li}_{s}"
    S = {"x": x}
    proj = matmul(n("mm_in"), x, W["w_in"], "nn")
    (u0,) = rowcall(n("glu"), glu_fn, [(proj, 1024, 0), (proj, 1024, 1)], [], [(1024, F32, None)], T)
    u1 = conv_fwd(n("conv_a"), u0, 0, W["conv_a_w"], _rows(W["conv_a_b"]), T)
    (u3,) = rowcall(n("lnsilu"), lnsilu_fn, [(u1, 1024, 0)], [_rows(W["ln_a_g"]), _rows(W["ln_a_b"])],
                    [(1024, BF16, None)], T)
    y_a = matmul(n("mm_aout"), u3, W["w_a_out"], "nn")
    xbc = conv_fwd(n("conv_s"), proj, 6144 // CONV_CB, W["ssm_conv_w"], _rows(W["ssm_conv_b"]), T)
    bias_row, alog_row = _head_rows(W)
    y_f, hs_f = ssd_fwd(n("ssd_f"), xbc, proj, bias_row, alog_row, 0, T)
    y_b, hs_b = ssd_fwd(n("ssd_r"), xbc, proj, bias_row, alog_row, 1, T)
    dsk = jnp.repeat(W["d_skip"], HEAD_DIM).reshape(1, D_INNER)
    (yn,) = rowcall(n("gnorm"), gnorm_fn, [(y_f, 256, 0), (y_b, 256, 0), (xbc, 256, 0), (proj, 256, 16)],
                    [dsk, _rows(W["ssm_norm_g"])], [(256, BF16, None)], T, groups=N_GROUPS)
    y_bo = matmul(n("mm_bout"), yn, W["w_b_out"], "nn")
    (merged,) = rowcall(n("merge"), merge_fn, [(proj, 1024, 2), (proj, 1024, 3), (y_a, 1024, 0), (y_bo, 1024, 0)], [],
                        [(1024, BF16, None)], T)
    mix = matmul(n("mm_o"), merged, W["w_o"], "nn")
    (h,) = rowcall(n("ln1"), resln_fn, [(x, 1024, 0), (mix, 1024, 0)], [_rows(W["ln1_g"]), _rows(W["ln1_b"])],
                   [(1024, F32, None)], T)
    gu = matmul(n("mm_gu"), h, W["w_gate_up"], "nn")
    (act,) = rowcall(n("swiglu"), swiglu_fn, [(gu, FFN_DIM, 0), (gu, FFN_DIM, 1)], [], [(FFN_DIM, BF16, None)], T)
    dn = matmul(n("mm_down"), act, W["w_down"], "nn")
    (h2,) = rowcall(n("ln2"), resln_fn, [(h, 1024, 0), (dn, 1024, 0)], [_rows(W["ln2_g"]), _rows(W["ln2_b"])],
                    [(1024, F32, None)], T)
    pe = matmul(n("mm_ple"), p_l, W["w_ple"], "nn")
    gl = matmul(n("mm_pg"), h2, W["w_ple_gate"], "nn")
    (xn,) = rowcall(n("pleout"), ple_fn, [(h2, 1024, 0), (pe, 1024, 0), (gl, 1024, 0)], [_rows(W["ple_norm_g"])],
                    [(1024, F32, None)], T)
    S.update(proj=proj, u0=u0, u1=u1, u3=u3, y_a=y_a, xbc=xbc, y_f=y_f, y_b=y_b, hs_f=hs_f, hs_b=hs_b, yn=yn, y_bo=y_bo,
             merged=merged, mix=mix, h=h, gu=gu, act=act, dn=dn, h2=h2, pe=pe, gl=gl, dsk=dsk, bias_row=bias_row,
             alog_row=alog_row)
    return xn, S


def layer_bwd(li, dxn, p_l, W, S, T):
    n = lambda s: f"l{li}_{s}"
    G = {}
    x, proj = S["x"], S["proj"]
    (dh2a, dpe, dgl), (dpg,) = rowvjp(
        n("pleout_b"), ple_fn, [(S["h2"], 1024, 0), (S["pe"], 1024, 0), (S["gl"], 1024, 0)], [_rows(W["ple_norm_g"])],
        [(dxn, 1024, 0)], [([0], F32, None), ([1], BF16, None), ([2], BF16, None)], T)
    G["ple_norm_g"] = dpg
    G["w_ple_gate"] = matmul(n("mm_pg_w"), S["h2"], dgl, "tn")
    G["w_ple"] = matmul(n("mm_ple_w"), p_l, dpe, "tn")
    dh2 = matmul(n("mm_pg_x"), dgl, W["w_ple_gate"], "nt", add=dh2a)
    (dha, ddn), (G["ln2_g"], G["ln2_b"]) = rowvjp(
        n("ln2_b"), resln_fn, [(S["h"], 1024, 0), (S["dn"], 1024, 0)], [_rows(W["ln2_g"]), _rows(W["ln2_b"])],
        [(dh2, 1024, 0)], [([0], F32, None), ([1], BF16, None)], T)
    G["w_down"] = matmul(n("mm_down_w"), S["act"], ddn, "tn")
    dact = matmul(n("mm_down_x"), ddn, W["w_down"], "nt")
    (dgu,), _ = rowvjp(n("swiglu_b"), swiglu_fn, [(S["gu"], FFN_DIM, 0), (S["gu"], FFN_DIM, 1)], [],
                       [(dact, FFN_DIM, 0)], [([0, 1], BF16, None)], T)
    G["w_gate_up"] = matmul(n("mm_gu_w"), S["h"], dgu, "tn")
    dh = matmul(n("mm_gu_x"), dgu, W["w_gate_up"], "nt", add=dha)
    (dxa, dmix), (G["ln1_g"], G["ln1_b"]) = rowvjp(
        n("ln1_b"), resln_fn, [(x, 1024, 0), (S["mix"], 1024, 0)], [_rows(W["ln1_g"]), _rows(W["ln1_b"])],
        [(dh, 1024, 0)], [([0], F32, None), ([1], BF16, None)], T)
    G["w_o"] = matmul(n("mm_o_w"), S["merged"], dmix, "tn")
    dmerged = matmul(n("mm_o_x"), dmix, W["w_o"], "nt")
    dproj = jnp.zeros((T, N_IN_PAD), BF16)
    (dproj, dy_a, dy_bo), _ = rowvjp(
        n("merge_b"), merge_fn, [(proj, 1024, 2), (proj, 1024, 3), (S["y_a"], 1024, 0), (S["y_bo"], 1024, 0)], [],
        [(dmerged, 1024, 0)], [([0, 1], BF16, (dproj, 1)), ([2], BF16, None), ([3], BF16, None)], T)
    G["w_a_out"] = matmul(n("mm_aout_w"), S["u3"], dy_a, "tn")
    du3 = matmul(n("mm_aout_x"), dy_a, W["w_a_out"], "nt")
    (du1,), (G["ln_a_g"], G["ln_a_b"]) = rowvjp(
        n("lnsilu_b"), lnsilu_fn, [(S["u1"], 1024, 0)], [_rows(W["ln_a_g"]), _rows(W["ln_a_b"])], [(du3, 1024, 0)],
        [([0], F32, None)], T)
    du0, G["conv_a_w"], G["conv_a_b"] = conv_bwd(n("conv_a_b"), du1, S["u0"], 0, W["conv_a_w"], T)
    (dproj,), _ = rowvjp(n("glu_b"), glu_fn, [(proj, 1024, 0), (proj, 1024, 1)], [], [(du0, 1024, 0)],
                         [([0, 1], BF16, (dproj, 0))], T)
    G["w_b_out"] = matmul(n("mm_bout_w"), S["yn"], dy_bo, "tn")
    dyn = matmul(n("mm_bout_x"), dy_bo, W["w_b_out"], "nt")
    (dys, dxs, dproj), (ddsk, G["ssm_norm_g"]) = rowvjp(
        n("gnorm_b"), gnorm_fn, [(S["y_f"], 256, 0), (S["y_b"], 256, 0), (S["xbc"], 256, 0), (proj, 256, 16)],
        [S["dsk"], _rows(W["ssm_norm_g"])], [(dyn, 256, 0)],
        [([0], F32, None), ([2], F32, None), ([3], BF16, (dproj, 16))], T, groups=N_GROUPS)
    G["d_skip"] = ddsk.reshape(N_HEADS, HEAD_DIM).sum(axis=1)
    zb = jnp.zeros((T, N_GROUPS * D_STATE), F32)
    dx1, db1, dc1, ddt1, dbias_f, dalog_f = ssd_bwd(
        n("ssd_f_b"), S["xbc"], proj, S["bias_row"], S["alog_row"], S["hs_f"], dys, (dxs, zb, zb, zb[:, :LANE]), 0, T)
    dxx, dbb, dcc, ddt, dbias_r, dalog_r = ssd_bwd(
        n("ssd_r_b"), S["xbc"], proj, S["bias_row"], S["alog_row"], S["hs_b"], dys, (dx1, db1, dc1, ddt1), 1, T)
    G["dt_bias"] = (dbias_f + dbias_r)[0, :2 * N_HEADS].reshape(2, N_HEADS)
    G["a_log"] = (dalog_f + dalog_r)[0, :2 * N_HEADS].reshape(2, N_HEADS)
    cw = W["ssm_conv_w"]
    b0 = 6144 // CONV_CB
    dproj, dwx, dbx = conv_bwd(n("conv_sx_b"), dxx, proj, b0, cw[:, :D_INNER], T, into=(dproj, b0))
    dproj, dwb, dbb_ = conv_bwd(n("conv_sb_b"), dbb, proj, b0 + 4, cw[:, D_INNER:D_INNER + 1024], T, into=(dproj, b0 + 4))
    dproj, dwc, dbc = conv_bwd(n("conv_sc_b"), dcc, proj, b0 + 6, cw[:, D_INNER + 1024:], T, into=(dproj, b0 + 6))
    G["ssm_conv_w"] = jnp.concatenate([dwx, dwb, dwc], axis=1)
    G["ssm_conv_b"] = jnp.concatenate([dbx, dbb_, dbc], axis=1)
    (dproj,) = rowcall(n("dt_cast"), ident_fn, [(ddt, LANE, 0)], [], [(LANE, BF16, (dproj, (N_IN_PAD - LANE) // LANE))], T)
    G["w_in"] = matmul(n("mm_in_w"), x, dproj, "tn")[:, :N_IN]
    dx = matmul(n("mm_in_x"), dproj, W["w_in"], "nt", add=dxa)
    return dx, G


def local_step(x, p, loss_target, FW, T):
    Ws, saves = [], []
    cur = x
    for li in range(DEPTH):
        W = {k: v[li] for k, v in FW.items()}
        W["w_in"] = jnp.pad(W["w_in"], ((0, 0), (0, N_IN_PAD - N_IN)))
        Ws.append(W)
        cur, S = layer_fwd(li, cur, p[li], W, T)
        saves.append(S)
    dcur, sq = loss_head(cur, loss_target, T)
    loss = 0.5 * jnp.sum(sq) / D_MODEL
    grads = [None] * DEPTH
    for li in reversed(range(DEPTH)):
        dcur, grads[li] = layer_bwd(li, dcur, p[li], Ws[li], saves[li], T)
    out = {}
    for k in WEIGHTS:
        out[k] = jnp.stack([grads[li][k].reshape(FW[k].shape[1:]) for li in range(DEPTH)])
    return loss, dcur, out


def kernel(x, p, w_in, conv_a_w, conv_a_b, ln_a_g, ln_a_b, w_a_out, ssm_conv_w, ssm_conv_b, a_log, dt_bias, d_skip, ssm_norm_g, w_b_out, w_o, ln1_g, ln1_b, w_gate_up, w_down, ln2_g, ln2_b, w_ple, ple_norm_g, w_ple_gate, loss_target, m_w_in, m_conv_a_w, m_conv_a_b, m_ln_a_g, m_ln_a_b, m_w_a_out, m_ssm_conv_w, m_ssm_conv_b, m_a_log, m_dt_bias, m_d_skip, m_ssm_norm_g, m_w_b_out, m_w_o, m_ln1_g, m_ln1_b, m_w_gate_up, m_w_down, m_ln2_g, m_ln2_b, m_w_ple, m_ple_norm_g, m_w_ple_gate, v_w_in, v_conv_a_w, v_conv_a_b, v_ln_a_g, v_ln_a_b, v_w_a_out, v_ssm_conv_w, v_ssm_conv_b, v_a_log, v_dt_bias, v_d_skip, v_ssm_norm_g, v_w_b_out, v_w_o, v_ln1_g, v_ln1_b, v_w_gate_up, v_w_down, v_ln2_g, v_ln2_b, v_w_ple, v_ple_norm_g, v_w_ple_gate):
    A = dict(locals())
    w = {k: A[k] for k in WEIGHTS}
    m = {k: A["m_" + k] for k in WEIGHTS}
    v = {k: A["v_" + k] for k in WEIGHTS}
    T = x.shape[1]
    big_shapes = [w[k].shape for k in BIG]
    small_shapes = [w[k].shape for k in SMALL]

    mm_names = [k for k in BIG if k not in CONV_W]
    gathered = all_gather("gather_weights", _pack([w[k].astype(BF16) for k in mm_names], 16))
    FW = {k: _full_from_gathered(k, g)
          for k, g in zip(mm_names, _unpack(gathered, [w[k].shape for k in mm_names], (N_DEV,)))}
    gathered = all_gather("gather_conv_weights", _pack([w[k] for k in CONV_W], SUBLANE))
    FW.update({k: _full_from_gathered(k, g)
               for k, g in zip(CONV_W, _unpack(gathered, [w[k].shape for k in CONV_W], (N_DEV,)))})
    FW.update({k: w[k] for k in SMALL})

    loss, grad_x, gfull = local_step(x[0], p[:, 0], loss_target[0], FW, T)
    loss = lax.psum(loss, ("x", "y", "c"))

    TR = 512
    gpack = jnp.stack([_pack([_pieces_from_full(k, gfull[k])[j] for k in BIG], TR) for j in range(N_DEV)])
    landed = grad_exchange("grad_exchange", gpack)
    res_big = adamw("adamw_big", landed, _pack([w[k] for k in BIG], TR), _pack([m[k] for k in BIG], TR),
                    _pack([v[k] for k in BIG], TR), TR)
    res_big = [dict(zip(BIG, _unpack(r, big_shapes))) for r in res_big]

    spack = _pack([gfull[k] for k in SMALL], SUBLANE)
    sall = all_gather("gather_small_grads", spack)
    rs = spack.shape[0]
    res_small = adamw("adamw_small", sall, _pack([w[k] for k in SMALL], SUBLANE), _pack([m[k] for k in SMALL], SUBLANE),
                      _pack([v[k] for k in SMALL], SUBLANE), rs)
    res_small = [dict(zip(SMALL, _unpack(r, small_shapes))) for r in res_small]

    outs = [loss, grad_x[None]]
    for q in range(4):
        for k in WEIGHTS:
            outs.append(res_big[q][k] if k in res_big[q] else res_small[q][k])
    return tuple(outs)
```

```python
import math

import jax
import jax.numpy as jnp
from jax import lax
from jax.experimental import pallas as pl
from jax.experimental.pallas import tpu as pltpu

F32 = jnp.float32
BF16 = jnp.bfloat16

D_MODEL = 1024
CONV_DIM = 1024
CONV_KERNEL = 31
D_INNER = 2048
HEAD_DIM = 64
N_HEADS = 32
N_GROUPS = 8
D_STATE = 128
SSM_CONV = 5
CHUNK = 128
XBC_DIM = D_INNER + 2 * N_GROUPS * D_STATE
FFN_DIM = 2816
PLE_DIM = 256
N_IN = 2 * CONV_DIM + 2 * D_MODEL + D_INNER + XBC_DIM + 2 * N_HEADS
N_IN_PAD = 10368
DEPTH = 2
N_DEV = 8
ALPHA = (2 * DEPTH) ** 0.25
LN_EPS = 1e-5
RMS_EPS = 1e-6
ADAM_LR, ADAM_B1, ADAM_B2, ADAM_EPS, ADAM_WD, ADAM_STEP = 0.001, 0.9, 0.999, 1e-08, 0.01, 10

LANE = 128
SUBLANE = 8
HALO = 16
VMEM_LIMIT = 52 * 1024 * 1024
NEG = -1e30

BIG = ["w_in", "conv_a_w", "w_a_out", "ssm_conv_w", "w_b_out", "w_o", "w_gate_up", "w_down", "w_ple", "w_ple_gate"]
BIG_AXIS = {"w_in": 2, "conv_a_w": 2, "w_a_out": 1, "ssm_conv_w": 2, "w_b_out": 1, "w_o": 1, "w_gate_up": 2,
            "w_down": 1, "w_ple": 2, "w_ple_gate": 1}
COL_T = ["w_in", "w_gate_up"]
CONV_W = ["conv_a_w", "ssm_conv_w"]
SMALL = ["conv_a_b", "ln_a_g", "ln_a_b", "ssm_conv_b", "a_log", "dt_bias", "d_skip", "ssm_norm_g", "ln1_g", "ln1_b",
         "ln2_g", "ln2_b", "ple_norm_g"]
WEIGHTS = ["w_in", "conv_a_w", "conv_a_b", "ln_a_g", "ln_a_b", "w_a_out", "ssm_conv_w", "ssm_conv_b", "a_log", "dt_bias",
           "d_skip", "ssm_norm_g", "w_b_out", "w_o", "ln1_g", "ln1_b", "w_gate_up", "w_down", "ln2_g", "ln2_b", "w_ple",
           "ple_norm_g", "w_ple_gate"]


def _cparams(sem):
    return pltpu.CompilerParams(dimension_semantics=sem, vmem_limit_bytes=VMEM_LIMIT)


def _pick(n, cap):
    if n <= cap:
        return n
    best = None
    for d in range(LANE, cap + 1, LANE):
        if n % d == 0:
            best = d
    assert best is not None, (n, cap)
    return best


def matmul(name, a, b, mode, out_dtype=F32, add=None):
    if mode == "nn":
        (M, K), (K2, N) = a.shape, b.shape
    elif mode == "nt":
        (M, K), (N, K2) = a.shape, b.shape
    else:
        (K, M), (K2, N) = a.shape, b.shape
    assert K == K2, (name, a.shape, b.shape)
    tm = _pick(M, 1024) if mode != "tn" else _pick(M, 1408)
    tn = _pick(N, 1408)
    tk = _pick(K, 512) if mode == "tn" else (K if (mode == "nn" and K <= 2816) else _pick(K, 1408))
    nk = K // tk
    grid = (M // tm, N // tn, nk)
    if mode == "tn":
        a_spec = pl.BlockSpec((tk, tm), lambda i, j, k: (k, i))
    else:
        a_spec = pl.BlockSpec((tm, tk), lambda i, j, k: (i, k))
    if mode == "nt":
        b_spec = pl.BlockSpec((tn, tk), lambda i, j, k: (j, k))
    else:
        b_spec = pl.BlockSpec((tk, tn), lambda i, j, k: (k, j))
    o_spec = pl.BlockSpec((tm, tn), lambda i, j, k: (i, j))
    dims = {"nn": ((1,), (0,)), "nt": ((1,), (1,)), "tn": ((0,), (0,))}[mode]
    has_add = add is not None

    def body(a_ref, b_ref, *rest):
        if has_add:
            add_ref, o_ref, *scr = rest
        else:
            o_ref, *scr = rest
        part = lax.dot_general(a_ref[...].astype(BF16), b_ref[...].astype(BF16), (dims, ((), ())),
                               preferred_element_type=F32)

        def finish(v):
            if has_add:
                v = v + add_ref[...].astype(F32)
            o_ref[...] = v.astype(o_ref.dtype)

        if nk == 1:
            finish(part)
        else:
            acc = scr[0]
            k = pl.program_id(2)

            @pl.when(k == 0)
            def _():
                acc[...] = part

            @pl.when(k > 0)
            def _():
                acc[...] += part

            @pl.when(k == nk - 1)
            def _():
                finish(acc[...])

    in_specs = [a_spec, b_spec] + ([o_spec] if has_add else [])
    args = (a, b) + ((add,) if has_add else ())
    return pl.pallas_call(
        body, out_shape=jax.ShapeDtypeStruct((M, N), out_dtype), grid=grid, in_specs=in_specs, out_specs=o_spec,
        scratch_shapes=[pltpu.VMEM((tm, tn), F32)] if nk > 1 else [], name=name,
        compiler_params=_cparams(("parallel", "parallel", "arbitrary")))(*args)


def _row_specs(items, tT, groups):
    specs = []
    for (_, w, blk) in items:
        assert blk % groups == 0
        specs.append(pl.BlockSpec((tT, w * groups), (lambda i, b=blk // groups: (i, b))))
    return specs


def _slices(v, groups):
    if groups == 1:
        return [v]
    w = v.shape[1] // groups
    return [v[:, w * s:w * (s + 1)] for s in range(groups)]


def _cat(vs):
    return vs[0] if len(vs) == 1 else jnp.concatenate(vs, axis=1)


def rowcall(name, fn, ins, pars, outs, T, tT=256, groups=1):
    n_in, n_par = len(ins), len(pars)
    intos = [o[2] for o in outs if o[2] is not None]
    in_specs = (_row_specs(ins, tT, groups) + [pl.BlockSpec(p.shape, lambda i: (0, 0)) for p in pars]
                + [pl.BlockSpec(memory_space=pl.ANY)] * len(intos))
    out_specs, out_shapes, aliases = [], [], {}
    n_alias = 0
    for oi, (w, dt, into) in enumerate(outs):
        if into is None:
            out_specs.append(pl.BlockSpec((tT, w * groups), lambda i: (i, 0)))
            out_shapes.append(jax.ShapeDtypeStruct((T, w * groups), dt))
        else:
            arr, blk = into
            out_specs.append(pl.BlockSpec((tT, w * groups), lambda i, b=blk // groups: (i, b)))
            out_shapes.append(jax.ShapeDtypeStruct(arr.shape, arr.dtype))
            aliases[n_in + n_par + n_alias] = oi
            n_alias += 1

    def body(*refs):
        xs = [_slices(r[...].astype(F32), groups) for r in refs[:n_in]]
        ps = [_slices(r[...], groups) for r in refs[n_in:n_in + n_par]]
        o_refs = refs[n_in + n_par + n_alias:]
        res = [fn(*[x[s] for x in xs], *[p[s] for p in ps]) for s in range(groups)]
        for k, r in enumerate(o_refs):
            r[...] = _cat([res[s][k] for s in range(groups)]).astype(r.dtype)

    res = pl.pallas_call(
        body, out_shape=out_shapes, grid=(T // tT,), in_specs=in_specs, out_specs=out_specs,
        input_output_aliases=aliases, name=name, compiler_params=_cparams(("parallel",)))(
            *[a for (a, _, _) in ins], *pars, *[a for (a, _) in intos])
    return list(res)


def rowvjp(name, fn, ins, pars, cts, douts, T, tT=256, groups=1):
    n_in, n_par, n_ct = len(ins), len(pars), len(cts)
    intos = [o[2] for o in douts if o[2] is not None]
    in_specs = (_row_specs(ins, tT, groups) + [pl.BlockSpec(p.shape, lambda i: (0, 0)) for p in pars]
                + _row_specs(cts, tT, groups) + [pl.BlockSpec(memory_space=pl.ANY)] * len(intos))
    out_specs, out_shapes, aliases = [], [], {}
    n_alias = 0
    for oi, (idxs, dt, into) in enumerate(douts):
        w = sum(ins[k][1] for k in idxs) * groups
        if into is None:
            out_specs.append(pl.BlockSpec((tT, w), lambda i: (i, 0)))
            out_shapes.append(jax.ShapeDtypeStruct((T, w), dt))
        else:
            assert len(idxs) == 1 or groups == 1
            arr, blk = into
            out_specs.append(pl.BlockSpec((tT, w), lambda i, b=blk // groups: (i, b)))
            out_shapes.append(jax.ShapeDtypeStruct(arr.shape, arr.dtype))
            aliases[n_in + n_par + n_ct + n_alias] = oi
            n_alias += 1
    n_dout = len(douts)
    for p in pars:
        out_specs.append(pl.BlockSpec(p.shape, lambda i: (0, 0)))
        out_shapes.append(jax.ShapeDtypeStruct(p.shape, F32))

    def body(*refs):
        xs = [_slices(r[...].astype(F32), groups) for r in refs[:n_in]]
        ps = [_slices(r[...], groups) for r in refs[n_in:n_in + n_par]]
        cs = [_slices(r[...].astype(F32), groups) for r in refs[n_in + n_par:n_in + n_par + n_ct]]
        o_refs = refs[n_in + n_par + n_ct + n_alias:]
        grads = []
        for s in range(groups):
            _, vjp_fn = jax.vjp(fn, *[x[s] for x in xs], *[p[s] for p in ps])
            grads.append(vjp_fn(tuple(c[s] for c in cs)))
        for r, (idxs, _, _) in zip(o_refs[:n_dout], douts):
            r[...] = _cat([grads[s][k] for k in idxs for s in range(groups)]).astype(r.dtype)
        for k, r in enumerate(o_refs[n_dout:]):
            @pl.when(pl.program_id(0) == 0)
            def _(r=r):
                r[...] = jnp.zeros(r.shape, F32)
            r[...] += _cat([grads[s][n_in + k] for s in range(groups)])

    res = pl.pallas_call(
        body, out_shape=out_shapes, grid=(T // tT,), in_specs=in_specs, out_specs=out_specs,
        input_output_aliases=aliases, name=name, compiler_params=_cparams(("arbitrary",)))(
            *[a for (a, _, _) in ins], *pars, *[a for (a, _, _) in cts], *[a for (a, _) in intos])
    res = list(res)
    return res[:n_dout], res[n_dout:]


def _sigmoid(x):
    return 1.0 / (1.0 + jnp.exp(-x))


def _silu(x):
    return x * _sigmoid(x)


def _softplus(x):
    return jnp.maximum(x, 0.0) + jnp.log(1.0 + jnp.exp(-jnp.abs(x)))


def _ln(x, g, b):
    mu = jnp.mean(x, axis=-1, keepdims=True)
    xc = x - mu
    var = jnp.mean(xc * xc, axis=-1, keepdims=True)
    return xc * lax.rsqrt(var + LN_EPS) * g + b


def glu_fn(a, gt):
    return (a * _sigmoid(gt),)


def lnsilu_fn(u, g, b):
    return (_silu(_ln(u, g, b)),)


def gnorm_fn(yf, yb, xp, z, dsk, ng):
    y = (yf + yb + _silu(xp) * dsk) * _silu(z)
    return (y * lax.rsqrt(jnp.mean(y * y, axis=-1, keepdims=True) + RMS_EPS) * ng,)


def merge_fn(ga, gb, ya, yb):
    return (_sigmoid(ga) * ya + _sigmoid(gb) * yb,)


def resln_fn(x, r, g, b):
    return (_ln(ALPHA * x + r, g, b),)


def swiglu_fn(g, u):
    return (_silu(g) * u,)


def ple_fn(h2, pe, gl, g):
    e = pe * lax.rsqrt(jnp.mean(pe * pe, axis=-1, keepdims=True) + RMS_EPS) * g
    return (h2 + e * _sigmoid(gl),)


def ident_fn(v):
    return (v,)


CONV_CB = 512
CONV_TT = 512
CONV_RB = 16


def _conv_specs(blk0, T, tT, cb):
    nh = tT // HALO
    cur = pl.BlockSpec((tT, cb), lambda j, i: (i, blk0 + j))
    prev = pl.BlockSpec((HALO, cb), lambda j, i: (jnp.maximum(i * nh - 1, 0), blk0 + j))
    nxt = pl.BlockSpec((HALO, cb), lambda j, i: (jnp.minimum((i + 1) * nh, T // HALO - 1), blk0 + j))
    return [prev, cur, nxt]


def _fill_shifted(sh_ref, pad_ref, prev_ref, cur_ref, next_ref, i, n_t, tT):
    pad_ref[pl.ds(0, HALO), :] = prev_ref[...].astype(F32) * (i > 0).astype(F32)
    pad_ref[pl.ds(HALO, tT), :] = cur_ref[...].astype(F32)
    pad_ref[pl.ds(HALO + tT, HALO), :] = next_ref[...].astype(F32) * (i < n_t - 1).astype(F32)
    for ph in range(SUBLANE):
        sh_ref[ph] = pad_ref[pl.ds(ph, tT + 3 * SUBLANE), :]


def _taps(sh_ref, base, ls, offsets):
    out = []
    for off in offsets:
        q, ph = divmod(off, SUBLANE)
        out.append((sh_ref[ph, pl.ds(base + SUBLANE * q, SUBLANE), ls],
                    sh_ref[ph, pl.ds(base + SUBLANE * (q + 1), SUBLANE), ls]))
    return out


def _conv_rows(sh_ref, w_ref, bias, o_ref, offsets, tT, cb):
    K = len(offsets)
    for lt in range(cb // LANE):
        ls = slice(LANE * lt, LANE * (lt + 1))
        wv = [jnp.broadcast_to(w_ref[k:k + 1, ls], (SUBLANE, LANE)) for k in range(K)]
        b0 = jnp.zeros((SUBLANE, LANE), F32) if bias is None else jnp.broadcast_to(bias[:, ls], (SUBLANE, LANE))

        def rows(r, carry, ls=ls, wv=wv, b0=b0):
            base = pl.multiple_of(r * CONV_RB, CONV_RB)
            lo, hi = b0, b0
            for k, (d0, d1) in enumerate(_taps(sh_ref, base, ls, offsets)):
                lo = lo + d0 * wv[k]
                hi = hi + d1 * wv[k]
            o_ref[pl.ds(base, CONV_RB), ls] = jnp.concatenate([lo, hi], axis=0).astype(o_ref.dtype)
            return carry

        lax.fori_loop(0, tT // CONV_RB, rows, 0)


def conv_fwd(name, u, blk0, w, b, T):
    K, C = w.shape
    P = (K - 1) // 2
    tT, cb = min(CONV_TT, T), CONV_CB
    n_t = T // tT

    def body(prev_ref, cur_ref, next_ref, w_ref, b_ref, o_ref, pad_ref, sh_ref):
        _fill_shifted(sh_ref, pad_ref, prev_ref, cur_ref, next_ref, pl.program_id(1), n_t, tT)
        _conv_rows(sh_ref, w_ref, b_ref[...], o_ref, [HALO - P + k for k in range(K)], tT, cb)

    return pl.pallas_call(
        body, out_shape=jax.ShapeDtypeStruct((T, C), F32), grid=(C // cb, n_t),
        in_specs=_conv_specs(blk0, T, tT, cb) + [pl.BlockSpec((K, cb), lambda j, i: (0, j)),
                                                  pl.BlockSpec((1, cb), lambda j, i: (0, j))],
        out_specs=pl.BlockSpec((tT, cb), lambda j, i: (i, j)),
        scratch_shapes=[pltpu.VMEM((tT + 2 * HALO, cb), F32), pltpu.VMEM((SUBLANE, tT + 3 * SUBLANE, cb), F32)],
        name=name, compiler_params=_cparams(("parallel", "arbitrary")))(u, u, u, w, b)


def conv_bwd(name, dy, u, blk0, w, T, into=None):
    K, C = w.shape
    P = (K - 1) // 2
    tT, cb = min(CONV_TT, T), CONV_CB
    n_t = T // tT

    def body(dprev, dcur, dnext, uprev, ucur, unext, w_ref, *rest):
        if into is not None:
            rest = rest[1:]
        du_ref, dw_ref, db_ref, pad_ref, shd_ref, shu_ref = rest
        i = pl.program_id(1)
        _fill_shifted(shd_ref, pad_ref, dprev, dcur, dnext, i, n_t, tT)
        _fill_shifted(shu_ref, pad_ref, uprev, ucur, unext, i, n_t, tT)
        _conv_rows(shd_ref, w_ref, None, du_ref, [HALO + P - k for k in range(K)], tT, cb)

        @pl.when(i == 0)
        def _():
            dw_ref[...] = jnp.zeros(dw_ref.shape, F32)
            db_ref[...] = jnp.zeros(db_ref.shape, F32)

        offsets = [HALO - P + k for k in range(K)]
        for lt in range(cb // LANE):
            ls = slice(LANE * lt, LANE * (lt + 1))

            def red(r, accs, ls=ls):
                base = pl.multiple_of(r * CONV_RB, CONV_RB)
                (d0, d1), = _taps(shd_ref, base, ls, [HALO])
                new = [acc + d0 * u0 + d1 * u1 for acc, (u0, u1) in zip(accs[:K], _taps(shu_ref, base, ls, offsets))]
                return tuple(new) + (accs[K] + d0 + d1,)

            zero = jnp.zeros((SUBLANE, LANE), F32)
            accs = lax.fori_loop(0, tT // CONV_RB, red, (zero,) * (K + 1))
            for k in range(K):
                dw_ref[k:k + 1, ls] += jnp.sum(accs[k], axis=0, keepdims=True)
            db_ref[:, ls] += jnp.sum(accs[K], axis=0, keepdims=True)

    dspecs = _conv_specs(0, T, tT, cb)
    uspecs = _conv_specs(blk0, T, tT, cb)
    in_specs = dspecs + uspecs + [pl.BlockSpec((K, cb), lambda j, i: (0, j))]
    args = [dy, dy, dy, u, u, u, w]
    aliases = {}
    if into is None:
        du_spec = pl.BlockSpec((tT, cb), lambda j, i: (i, j))
        du_shape = jax.ShapeDtypeStruct((T, C), F32)
    else:
        arr, oblk = into
        in_specs.append(pl.BlockSpec(memory_space=pl.ANY))
        args.append(arr)
        aliases = {7: 0}
        du_spec = pl.BlockSpec((tT, cb), lambda j, i: (i, oblk + j))
        du_shape = jax.ShapeDtypeStruct(arr.shape, arr.dtype)
    return pl.pallas_call(
        body, out_shape=[du_shape, jax.ShapeDtypeStruct((K, C), F32), jax.ShapeDtypeStruct((1, C), F32)],
        grid=(C // cb, n_t), in_specs=in_specs,
        out_specs=[du_spec, pl.BlockSpec((K, cb), lambda j, i: (0, j)), pl.BlockSpec((1, cb), lambda j, i: (0, j))],
        scratch_shapes=[pltpu.VMEM((tT + 2 * HALO, cb), F32), pltpu.VMEM((SUBLANE, tT + 3 * SUBLANE, cb), F32),
                        pltpu.VMEM((SUBLANE, tT + 3 * SUBLANE, cb), F32)],
        input_output_aliases=aliases, name=name, compiler_params=_cparams(("arbitrary", "arbitrary")))(*args)


def _dot(a, b, dims):
    return lax.dot_general(a.astype(BF16), b.astype(BF16), (dims, ((), ())), preferred_element_type=F32)


def _dnn(a, b):
    return _dot(a, b, ((1,), (0,)))


def _dnt(a, b):
    return _dot(a, b, ((1,), (1,)))


def _dtn(a, b):
    return _dot(a.T, b, ((1,), (0,)))


@jax.custom_vjp
def _nn(a, b):
    return _dnn(a, b)


_nn.defvjp(lambda a, b: (_dnn(a, b), (a, b)), lambda r, g: (_dnt(g, r[1]), _dtn(r[0], g)))


@jax.custom_vjp
def _nt(a, b):
    return _dnt(a, b)


_nt.defvjp(lambda a, b: (_dnt(a, b), (a, b)), lambda r, g: (_dnn(g, r[1]), _dtn(g, r[0])))


@jax.custom_vjp
def _tn(a, b):
    return _dtn(a, b)


_tn.defvjp(lambda a, b: (_dtn(a, b), (a, b)), lambda r, g: (_dnt(r[1], g), _dnn(r[0], g)))


def _split_dot(m, v):
    hi = v.astype(BF16)
    r1 = v - hi.astype(F32)
    mid = r1.astype(BF16)
    lo = (r1 - mid.astype(F32)).astype(BF16)
    mb = m.astype(BF16)
    d = lambda x: lax.dot_general(mb, x, (((1,), (0,)), ((), ())), preferred_element_type=F32)
    return d(hi) + d(mid) + d(lo)


@jax.custom_vjp
def _tri_dot(tri, tri_t, v):
    return _split_dot(tri, v)


_tri_dot.defvjp(lambda tri, tri_t, v: (_split_dot(tri, v), (tri, tri_t)),
                lambda r, g: (jnp.zeros_like(r[0]), jnp.zeros_like(r[1]), _split_dot(r[1], g)))


def _ssd_consts(dirn):
    ri = lax.broadcasted_iota(jnp.int32, (CHUNK, CHUNK), 0)
    ci = lax.broadcasted_iota(jnp.int32, (CHUNK, CHUNK), 1)
    keep = (ci <= ri) if dirn == 0 else (ci >= ri)
    tri = keep.astype(F32)
    tri_t = (~keep | (ci == ri)).astype(F32)
    lane = lax.broadcasted_iota(jnp.int32, (1, LANE), 1)
    sub = lax.broadcasted_iota(jnp.int32, (CHUNK, 1), 0)
    end = (sub == (CHUNK - 1 if dirn == 0 else 0)).astype(F32)
    lo_half = lane < HEAD_DIM
    oh_lane = [(lane == N_HEADS * dirn + h).astype(F32) for h in range(N_HEADS)]
    oh_sub = [(sub == N_HEADS * dirn + h).astype(F32) for h in range(N_HEADS)]
    return keep, tri, tri_t, end, lo_half, oh_lane, oh_sub


def _ssd_chunk(consts, x_t, b_t, c_t, dtr, bias, alog, h_t):
    keep, tri, tri_t, end, lo_half, oh_lane, oh_sub = consts
    dt = _softplus(dtr + bias)
    a = dt * (-jnp.exp(alog))
    cs = _tri_dot(tri, tri_t, a)
    cs_t = cs.T
    tot = jnp.sum(cs * end, axis=0, keepdims=True)
    ys, hn = [], []
    for g in range(N_GROUPS):
        bm, cm = _silu(b_t[g]), _silu(c_t[g])
        gm = _nt(cm, bm)
        for jj in range(2):
            j = 2 * g + jj
            h0, h1 = 2 * j, 2 * j + 1
            col = [jnp.sum(cs * oh_lane[h], axis=1, keepdims=True) for h in (h0, h1)]
            row = [jnp.sum(cs_t * oh_sub[h], axis=0, keepdims=True) for h in (h0, h1)]
            dth = [jnp.sum(dt * oh_lane[h], axis=1, keepdims=True) for h in (h0, h1)]
            toth = [jnp.sum(tot * oh_lane[h], axis=1, keepdims=True) for h in (h0, h1)]
            xd = _silu(x_t[j]) * jnp.where(lo_half, dth[0], dth[1])
            yd = [_nn(gm * jnp.exp(jnp.where(keep, col[k] - row[k], NEG)), xd) for k in range(2)]
            cp = jnp.where(lo_half, col[0], col[1])
            tp = jnp.where(lo_half, toth[0], toth[1])
            ys.append(jnp.where(lo_half, yd[0], yd[1]) + _nn(cm, h_t[j]) * jnp.exp(cp))
            hn.append(h_t[j] * jnp.exp(tp) + _tn(bm, xd * jnp.exp(tp - cp)))
    return ys, hn


N_PAIR = D_INNER // LANE


def _tiles(ref, n):
    return [ref[:, LANE * j:LANE * (j + 1)].astype(F32) for j in range(n)]


def _ssd_in_specs(cmap):
    return [pl.BlockSpec((CHUNK, D_INNER), lambda i: (cmap(i), 0)),
            pl.BlockSpec((CHUNK, N_GROUPS * D_STATE), lambda i: (cmap(i), 2)),
            pl.BlockSpec((CHUNK, N_GROUPS * D_STATE), lambda i: (cmap(i), 3)),
            pl.BlockSpec((CHUNK, LANE), lambda i: (cmap(i), (N_IN_PAD - LANE) // LANE)),
            pl.BlockSpec((1, LANE), lambda i: (0, 0)), pl.BlockSpec((1, LANE), lambda i: (0, 0))]


def ssd_fwd(name, xbc, proj, bias_row, alog_row, dirn, T):
    nc = T // CHUNK
    cmap = (lambda i: i) if dirn == 0 else (lambda i: nc - 1 - i)

    def body(x_ref, b_ref, c_ref, dt_ref, bias_ref, alog_ref, y_ref, hs_ref, h_scr):
        @pl.when(pl.program_id(0) == 0)
        def _():
            h_scr[...] = jnp.zeros(h_scr.shape, F32)

        hs_ref[0] = h_scr[...]
        ys, hn = _ssd_chunk(_ssd_consts(dirn), _tiles(x_ref, N_PAIR), _tiles(b_ref, N_GROUPS), _tiles(c_ref, N_GROUPS),
                            dt_ref[...], bias_ref[...], alog_ref[...], _tiles(h_scr, N_PAIR))
        for j in range(N_PAIR):
            y_ref[:, LANE * j:LANE * (j + 1)] = ys[j]
            h_scr[:, LANE * j:LANE * (j + 1)] = hn[j]

    return pl.pallas_call(
        body, out_shape=[jax.ShapeDtypeStruct((T, D_INNER), F32), jax.ShapeDtypeStruct((nc, D_STATE, D_INNER), F32)],
        grid=(nc,), in_specs=_ssd_in_specs(cmap),
        out_specs=[pl.BlockSpec((CHUNK, D_INNER), lambda i: (cmap(i), 0)),
                   pl.BlockSpec((1, D_STATE, D_INNER), lambda i: (cmap(i), 0, 0))],
        scratch_shapes=[pltpu.VMEM((D_STATE, D_INNER), F32)], name=name,
        compiler_params=_cparams(("arbitrary",)))(xbc, xbc, xbc, proj, bias_row, alog_row)


def ssd_bwd(name, xbc, proj, bias_row, alog_row, hs, dy, adds, dirn, T):
    nc = T // CHUNK
    cmap = (lambda i: nc - 1 - i) if dirn == 0 else (lambda i: i)
    GS = N_GROUPS * D_STATE

    def body(x_ref, b_ref, c_ref, dt_ref, bias_ref, alog_ref, hs_ref, dy_ref, ax_ref, ab_ref, ac_ref, adt_ref,
             dx_ref, db_ref, dc_ref, ddt_ref, dbias_ref, dalog_ref, dh_scr):
        first = pl.program_id(0) == 0

        @pl.when(first)
        def _():
            dh_scr[...] = jnp.zeros(dh_scr.shape, F32)
            dbias_ref[...] = jnp.zeros(dbias_ref.shape, F32)
            dalog_ref[...] = jnp.zeros(dalog_ref.shape, F32)

        consts = _ssd_consts(dirn)
        fn = lambda *a: _ssd_chunk(consts, *a)
        _, vjp_fn = jax.vjp(fn, _tiles(x_ref, N_PAIR), _tiles(b_ref, N_GROUPS), _tiles(c_ref, N_GROUPS), dt_ref[...],
                            bias_ref[...], alog_ref[...], [hs_ref[0, :, LANE * j:LANE * (j + 1)] for j in range(N_PAIR)])
        dx, db, dc, ddt, dbias, dalog, dh = vjp_fn((_tiles(dy_ref, N_PAIR), _tiles(dh_scr, N_PAIR)))
        for j in range(N_PAIR):
            s = slice(LANE * j, LANE * (j + 1))
            dx_ref[:, s] = dx[j] + ax_ref[:, s]
            dh_scr[:, s] = dh[j]
        for g in range(N_GROUPS):
            s = slice(LANE * g, LANE * (g + 1))
            db_ref[:, s] = db[g] + ab_ref[:, s]
            dc_ref[:, s] = dc[g] + ac_ref[:, s]
        ddt_ref[...] = ddt + adt_ref[...]
        dbias_ref[...] += dbias
        dalog_ref[...] += dalog

    blk = lambda w: pl.BlockSpec((CHUNK, w), lambda i: (cmap(i), 0))
    row = pl.BlockSpec((1, LANE), lambda i: (0, 0))
    return pl.pallas_call(
        body,
        out_shape=[jax.ShapeDtypeStruct((T, D_INNER), F32), jax.ShapeDtypeStruct((T, GS), F32),
                   jax.ShapeDtypeStruct((T, GS), F32), jax.ShapeDtypeStruct((T, LANE), F32),
                   jax.ShapeDtypeStruct((1, LANE), F32), jax.ShapeDtypeStruct((1, LANE), F32)],
        grid=(nc,),
        in_specs=_ssd_in_specs(cmap) + [pl.BlockSpec((1, D_STATE, D_INNER), lambda i: (cmap(i), 0, 0)), blk(D_INNER),
                                        blk(D_INNER), blk(GS), blk(GS), blk(LANE)],
        out_specs=[blk(D_INNER), blk(GS), blk(GS), blk(LANE), row, row],
        scratch_shapes=[pltpu.VMEM((D_STATE, D_INNER), F32)], name=name,
        compiler_params=_cparams(("arbitrary",)))(xbc, xbc, xbc, proj, bias_row, alog_row, hs, dy, *adds)


def loss_head(y, target, T, tT=256):
    def body(y_ref, t_ref, dy_ref, sq_ref):
        @pl.when(pl.program_id(0) == 0)
        def _():
            sq_ref[...] = jnp.zeros(sq_ref.shape, F32)
        e = y_ref[...] - t_ref[...]
        dy_ref[...] = e * (1.0 / D_MODEL)
        sq_ref[...] += jnp.sum(e * e, axis=0, keepdims=True)

    spec = pl.BlockSpec((tT, D_MODEL), lambda i: (i, 0))
    return pl.pallas_call(
        body, out_shape=[jax.ShapeDtypeStruct((T, D_MODEL), F32), jax.ShapeDtypeStruct((1, D_MODEL), F32)],
        grid=(T // tT,), in_specs=[spec, spec], out_specs=[spec, pl.BlockSpec((1, D_MODEL), lambda i: (0, 0))],
        name="loss_head", compiler_params=_cparams(("arbitrary",)))(y, target)


MESH_ID = pl.DeviceIdType.MESH


def all_gather(name, v):
    R, W = v.shape

    def body(v_ref, out_ref, send_sems, recv_sems, local_sem):
        x, y, c = lax.axis_index("x"), lax.axis_index("y"), lax.axis_index("c")
        me, sibling = (x, y, c), (x, y, 1 - c)
        chips = [(1 - x, y), (x, 1 - y), (1 - x, 1 - y)]

        def slot(px, py, pc):
            return out_ref.at[4 * px + 2 * py + pc]

        def copy(k, block, to, src=None):
            return pltpu.make_async_remote_copy(
                src_ref=slot(*block) if src is None else src, dst_ref=slot(*block), send_sem=send_sems.at[k],
                recv_sem=recv_sems.at[k], device_id=to, device_id_type=MESH_ID)

        mine = pltpu.make_async_copy(v_ref, slot(*me), local_sem)
        mine.start()
        first = [copy(0, me, sibling, src=v_ref)]
        first += [copy(1 + j, me, (*chip, c), src=v_ref) for j, chip in enumerate(chips)]
        for cp in first:
            cp.start()
        passed = [copy(4 + j, (*chip, c), sibling) for j, chip in enumerate(chips)]
        for j, chip in enumerate(chips):
            copy(1 + j, (*chip, c), me).wait_recv()
            passed[j].start()
        copy(0, sibling, me).wait_recv()
        for j, chip in enumerate(chips):
            copy(4 + j, (*chip, 1 - c), me).wait_recv()
        for cp in first + passed:
            cp.wait_send()
        mine.wait()

    return pl.pallas_call(
        body, out_shape=jax.ShapeDtypeStruct((N_DEV, R, W), v.dtype),
        in_specs=[pl.BlockSpec(memory_space=pl.ANY)], out_specs=pl.BlockSpec(memory_space=pl.ANY),
        scratch_shapes=[pltpu.SemaphoreType.DMA((7,)), pltpu.SemaphoreType.DMA((7,)), pltpu.SemaphoreType.DMA],
        name=name, compiler_params=pltpu.CompilerParams(has_side_effects=True))(v)


def grad_exchange(name, g):
    _, R, W = g.shape

    def body(g_ref, out_ref, send_sems, recv_sems, local_sem):
        x, y, c = lax.axis_index("x"), lax.axis_index("y"), lax.axis_index("c")
        me = 4 * x + 2 * y + c
        mine = pltpu.make_async_copy(g_ref.at[me], out_ref.at[me], local_sem)
        mine.start()
        copies = []
        for k in range(1, N_DEV):
            kx, ky, kc = (k >> 2) & 1, (k >> 1) & 1, k & 1
            px = 1 - x if kx else x
            py = 1 - y if ky else y
            pc = 1 - c if kc else c
            copies.append(pltpu.make_async_remote_copy(
                src_ref=g_ref.at[4 * px + 2 * py + pc], dst_ref=out_ref.at[me], send_sem=send_sems.at[k - 1],
                recv_sem=recv_sems.at[k - 1], device_id=(px, py, pc), device_id_type=MESH_ID))
        for cp in copies:
            cp.start()
        for cp in copies:
            cp.wait_recv()
        for cp in copies:
            cp.wait_send()
        mine.wait()

    return pl.pallas_call(
        body, out_shape=jax.ShapeDtypeStruct(g.shape, g.dtype),
        in_specs=[pl.BlockSpec(memory_space=pl.ANY)], out_specs=pl.BlockSpec(memory_space=pl.ANY),
        scratch_shapes=[pltpu.SemaphoreType.DMA((7,)), pltpu.SemaphoreType.DMA((7,)), pltpu.SemaphoreType.DMA],
        name=name, compiler_params=pltpu.CompilerParams(has_side_effects=True))(g)


def sum_slots(name, parts, tr):
    n_slot, R, W = parts.shape

    def body(p_ref, o_ref):
        g = p_ref[0].astype(F32)
        for s in range(1, n_slot):
            g = g + p_ref[s].astype(F32)
        o_ref[...] = g

    return pl.pallas_call(
        body, out_shape=jax.ShapeDtypeStruct((R, W), F32), grid=(R // tr,),
        in_specs=[pl.BlockSpec((n_slot, tr, W), lambda i: (0, i, 0))], out_specs=pl.BlockSpec((tr, W), lambda i: (i, 0)),
        name=name, compiler_params=_cparams(("parallel",)))(parts)


def adamw(name, parts, w, m, v, tr):
    R, W = w.shape
    n_slot = parts.shape[0]
    c1 = 1.0 / (1.0 - ADAM_B1 ** ADAM_STEP)
    c2 = 1.0 / (1.0 - ADAM_B2 ** ADAM_STEP)

    def body(p_ref, w_ref, m_ref, v_ref, g_ref, d_ref, nm_ref, nv_ref):
        g = p_ref[0]
        for s in range(1, n_slot):
            g = g + p_ref[s]
        nm = ADAM_B1 * m_ref[...] + (1.0 - ADAM_B1) * g
        nv = ADAM_B2 * v_ref[...] + (1.0 - ADAM_B2) * (g * g)
        g_ref[...] = g
        nm_ref[...] = nm
        nv_ref[...] = nv
        d_ref[...] = -ADAM_LR * ((nm * c1) / (jnp.sqrt(nv * c2) + ADAM_EPS) + ADAM_WD * w_ref[...])

    spec = pl.BlockSpec((tr, W), lambda i: (i, 0))
    return pl.pallas_call(
        body, out_shape=[jax.ShapeDtypeStruct((R, W), F32)] * 4, grid=(R // tr,),
        in_specs=[pl.BlockSpec((n_slot, tr, W), lambda i: (0, i, 0)), spec, spec, spec], out_specs=[spec] * 4,
        name=name, compiler_params=_cparams(("parallel",)))(parts, w, m, v)


def _pack(arrs, row_mult):
    flat = jnp.concatenate([a.reshape(-1) for a in arrs])
    n = flat.shape[0]
    rows = -(-n // LANE)
    rows = -(-rows // row_mult) * row_mult
    return jnp.pad(flat, (0, rows * LANE - n)).reshape(rows, LANE)


def _unpack(buf, shapes, lead=()):
    flat = buf.reshape(lead + (-1,))
    out, off = [], 0
    for s in shapes:
        n = math.prod(s)
        out.append(flat[..., off:off + n].reshape(lead + tuple(s)))
        off += n
    return out


def _rows_sharded(name):
    return name in COL_T or BIG_AXIS[name] == 1


def _shard_for_gather(name, w):
    return jnp.swapaxes(w, 1, 2) if name in COL_T else w


def _full_from_gathered(name, g):
    if _rows_sharded(name):
        return jnp.transpose(g, (1, 0, 2, 3)).reshape(g.shape[1], N_DEV * g.shape[2], g.shape[3])
    return jnp.transpose(g, (1, 2, 0, 3)).reshape(g.shape[1], g.shape[2], N_DEV * g.shape[3])


def _pieces_from_full(name, f):
    L, A, B = f.shape
    if _rows_sharded(name):
        return jnp.transpose(f.reshape(L, N_DEV, -1), (1, 0, 2)).reshape(N_DEV, -1)
    return jnp.transpose(f.reshape(L, A, N_DEV, B // N_DEV), (2, 0, 1, 3)).reshape(N_DEV, -1)


def _piece_shape(name, shard_shape):
    L, a, b = shard_shape
    return (L, b, a) if name in COL_T else (L, a, b)


def _rows(v):
    return v.reshape(1, -1).astype(F32)


def _head_rows(W):
    bias = jnp.pad(W["dt_bias"].reshape(1, -1), ((0, 0), (0, LANE - 2 * N_HEADS)))
    alog = jnp.pad(W["a_log"].reshape(1, -1), ((0, 0), (0, LANE - 2 * N_HEADS)))
    return bias, alog


def layer_fwd(li, x, p_l, W, T):
    n = lambda s: f"l{li}_{s}"
    S = {"x": x}
    proj = matmul(n("mm_in"), x, W["w_in"], "nt")
    (u0,) = rowcall(n("glu"), glu_fn, [(proj, 1024, 0), (proj, 1024, 1)], [], [(1024, F32, None)], T)
    u1 = conv_fwd(n("conv_a"), u0, 0, W["conv_a_w"], _rows(W["conv_a_b"]), T)
    (u3,) = rowcall(n("lnsilu"), lnsilu_fn, [(u1, 1024, 0)], [_rows(W["ln_a_g"]), _rows(W["ln_a_b"])],
                    [(1024, BF16, None)], T)
    y_a = matmul(n("mm_aout"), u3, W["w_a_out"], "nn")
    xbc = conv_fwd(n("conv_s"), proj, 6144 // CONV_CB, W["ssm_conv_w"], _rows(W["ssm_conv_b"]), T)
    bias_row, alog_row = _head_rows(W)
    y_f, hs_f = ssd_fwd(n("ssd_f"), xbc, proj, bias_row, alog_row, 0, T)
    y_b, hs_b = ssd_fwd(n("ssd_r"), xbc, proj, bias_row, alog_row, 1, T)
    dsk = jnp.repeat(W["d_skip"], HEAD_DIM).reshape(1, D_INNER)
    (yn,) = rowcall(n("gnorm"), gnorm_fn, [(y_f, 256, 0), (y_b, 256, 0), (xbc, 256, 0), (proj, 256, 16)],
                    [dsk, _rows(W["ssm_norm_g"])], [(256, BF16, None)], T, groups=N_GROUPS)
    y_bo = matmul(n("mm_bout"), yn, W["w_b_out"], "nn")
    (merged,) = rowcall(n("merge"), merge_fn, [(proj, 1024, 2), (proj, 1024, 3), (y_a, 1024, 0), (y_bo, 1024, 0)], [],
                        [(1024, BF16, None)], T)
    mix = matmul(n("mm_o"), merged, W["w_o"], "nn")
    (h,) = rowcall(n("ln1"), resln_fn, [(x, 1024, 0), (mix, 1024, 0)], [_rows(W["ln1_g"]), _rows(W["ln1_b"])],
                   [(1024, F32, None)], T)
    gu = matmul(n("mm_gu"), h, W["w_gate_up"], "nt")
    (act,) = rowcall(n("swiglu"), swiglu_fn, [(gu, FFN_DIM, 0), (gu, FFN_DIM, 1)], [], [(FFN_DIM, BF16, None)], T)
    dn = matmul(n("mm_down"), act, W["w_down"], "nn")
    (h2,) = rowcall(n("ln2"), resln_fn, [(h, 1024, 0), (dn, 1024, 0)], [_rows(W["ln2_g"]), _rows(W["ln2_b"])],
                    [(1024, F32, None)], T)
    pe = matmul(n("mm_ple"), p_l, W["w_ple"], "nn")
    gl = matmul(n("mm_pg"), h2, W["w_ple_gate"], "nn")
    (xn,) = rowcall(n("pleout"), ple_fn, [(h2, 1024, 0), (pe, 1024, 0), (gl, 1024, 0)], [_rows(W["ple_norm_g"])],
                    [(1024, F32, None)], T)
    S.update(proj=proj, u0=u0, u1=u1, u3=u3, y_a=y_a, xbc=xbc, y_f=y_f, y_b=y_b, hs_f=hs_f, hs_b=hs_b, yn=yn, y_bo=y_bo,
             merged=merged, mix=mix, h=h, gu=gu, act=act, dn=dn, h2=h2, pe=pe, gl=gl, dsk=dsk, bias_row=bias_row,
             alog_row=alog_row)
    return xn, S


def layer_bwd(li, dxn, p_l, W, S, T):
    n = lambda s: f"l{li}_{s}"
    G = {}
    x, proj = S["x"], S["proj"]
    (dh2a, dpe, dgl), (dpg,) = rowvjp(
        n("pleout_b"), ple_fn, [(S["h2"], 1024, 0), (S["pe"], 1024, 0), (S["gl"], 1024, 0)], [_rows(W["ple_norm_g"])],
        [(dxn, 1024, 0)], [([0], F32, None), ([1], BF16, None), ([2], BF16, None)], T)
    G["ple_norm_g"] = dpg
    G["w_ple_gate"] = matmul(n("mm_pg_w"), S["h2"], dgl, "tn")
    G["w_ple"] = matmul(n("mm_ple_w"), p_l, dpe, "tn")
    dh2 = matmul(n("mm_pg_x"), dgl, W["w_ple_gate"], "nt", add=dh2a)
    (dha, ddn), (G["ln2_g"], G["ln2_b"]) = rowvjp(
        n("ln2_b"), resln_fn, [(S["h"], 1024, 0), (S["dn"], 1024, 0)], [_rows(W["ln2_g"]), _rows(W["ln2_b"])],
        [(dh2, 1024, 0)], [([0], F32, None), ([1], BF16, None)], T)
    G["w_down"] = matmul(n("mm_down_w"), S["act"], ddn, "tn")
    dact = matmul(n("mm_down_x"), ddn, W["w_down"], "nt")
    (dgu,), _ = rowvjp(n("swiglu_b"), swiglu_fn, [(S["gu"], FFN_DIM, 0), (S["gu"], FFN_DIM, 1)], [],
                       [(dact, FFN_DIM, 0)], [([0, 1], BF16, None)], T)
    G["w_gate_up"] = matmul(n("mm_gu_w"), dgu, S["h"], "tn")
    dh = matmul(n("mm_gu_x"), dgu, W["w_gate_up"], "nn", add=dha)
    (dxa, dmix), (G["ln1_g"], G["ln1_b"]) = rowvjp(
        n("ln1_b"), resln_fn, [(x, 1024, 0), (S["mix"], 1024, 0)], [_rows(W["ln1_g"]), _rows(W["ln1_b"])],
        [(dh, 1024, 0)], [([0], F32, None), ([1], BF16, None)], T)
    G["w_o"] = matmul(n("mm_o_w"), S["merged"], dmix, "tn")
    dmerged = matmul(n("mm_o_x"), dmix, W["w_o"], "nt")
    dproj = jnp.zeros((T, N_IN_PAD), BF16)
    (dproj, dy_a, dy_bo), _ = rowvjp(
        n("merge_b"), merge_fn, [(proj, 1024, 2), (proj, 1024, 3), (S["y_a"], 1024, 0), (S["y_bo"], 1024, 0)], [],
        [(dmerged, 1024, 0)], [([0, 1], BF16, (dproj, 1)), ([2], BF16, None), ([3], BF16, None)], T)
    G["w_a_out"] = matmul(n("mm_aout_w"), S["u3"], dy_a, "tn")
    du3 = matmul(n("mm_aout_x"), dy_a, W["w_a_out"], "nt")
    (du1,), (G["ln_a_g"], G["ln_a_b"]) = rowvjp(
        n("lnsilu_b"), lnsilu_fn, [(S["u1"], 1024, 0)], [_rows(W["ln_a_g"]), _rows(W["ln_a_b"])], [(du3, 1024, 0)],
        [([0], F32, None)], T)
    du0, G["conv_a_w"], G["conv_a_b"] = conv_bwd(n("conv_a_b"), du1, S["u0"], 0, W["conv_a_w"], T)
    (dproj,), _ = rowvjp(n("glu_b"), glu_fn, [(proj, 1024, 0), (proj, 1024, 1)], [], [(du0, 1024, 0)],
                         [([0, 1], BF16, (dproj, 0))], T)
    G["w_b_out"] = matmul(n("mm_bout_w"), S["yn"], dy_bo, "tn")
    dyn = matmul(n("mm_bout_x"), dy_bo, W["w_b_out"], "nt")
    (dys, dxs, dproj), (ddsk, G["ssm_norm_g"]) = rowvjp(
        n("gnorm_b"), gnorm_fn, [(S["y_f"], 256, 0), (S["y_b"], 256, 0), (S["xbc"], 256, 0), (proj, 256, 16)],
        [S["dsk"], _rows(W["ssm_norm_g"])], [(dyn, 256, 0)],
        [([0], F32, None), ([2], F32, None), ([3], BF16, (dproj, 16))], T, groups=N_GROUPS)
    G["d_skip"] = ddsk.reshape(N_HEADS, HEAD_DIM).sum(axis=1)
    zb = jnp.zeros((T, N_GROUPS * D_STATE), F32)
    dx1, db1, dc1, ddt1, dbias_f, dalog_f = ssd_bwd(
        n("ssd_f_b"), S["xbc"], proj, S["bias_row"], S["alog_row"], S["hs_f"], dys, (dxs, zb, zb, zb[:, :LANE]), 0, T)
    dxx, dbb, dcc, ddt, dbias_r, dalog_r = ssd_bwd(
        n("ssd_r_b"), S["xbc"], proj, S["bias_row"], S["alog_row"], S["hs_b"], dys, (dx1, db1, dc1, ddt1), 1, T)
    G["dt_bias"] = (dbias_f + dbias_r)[0, :2 * N_HEADS].reshape(2, N_HEADS)
    G["a_log"] = (dalog_f + dalog_r)[0, :2 * N_HEADS].reshape(2, N_HEADS)
    cw = W["ssm_conv_w"]
    b0 = 6144 // CONV_CB
    dproj, dwx, dbx = conv_bwd(n("conv_sx_b"), dxx, proj, b0, cw[:, :D_INNER], T, into=(dproj, b0))
    dproj, dwb, dbb_ = conv_bwd(n("conv_sb_b"), dbb, proj, b0 + 4, cw[:, D_INNER:D_INNER + 1024], T, into=(dproj, b0 + 4))
    dproj, dwc, dbc = conv_bwd(n("conv_sc_b"), dcc, proj, b0 + 6, cw[:, D_INNER + 1024:], T, into=(dproj, b0 + 6))
    G["ssm_conv_w"] = jnp.concatenate([dwx, dwb, dwc], axis=1)
    G["ssm_conv_b"] = jnp.concatenate([dbx, dbb_, dbc], axis=1)
    (dproj,) = rowcall(n("dt_cast"), ident_fn, [(ddt, LANE, 0)], [], [(LANE, BF16, (dproj, (N_IN_PAD - LANE) // LANE))], T)
    G["w_in"] = matmul(n("mm_in_w"), dproj, x, "tn")[:N_IN]
    dx = matmul(n("mm_in_x"), dproj, W["w_in"], "nn", add=dxa)
    return dx, G


def local_step(x, p, loss_target, FW, T):
    Ws, saves = [], []
    cur = x
    for li in range(DEPTH):
        W = {k: v[li] for k, v in FW.items()}
        Ws.append(W)
        cur, S = layer_fwd(li, cur, p[li], W, T)
        saves.append(S)
    dcur, sq = loss_head(cur, loss_target, T)
    loss = 0.5 * jnp.sum(sq) / D_MODEL
    grads = [None] * DEPTH
    for li in reversed(range(DEPTH)):
        dcur, grads[li] = layer_bwd(li, dcur, p[li], Ws[li], saves[li], T)
    return loss, dcur, {k: jnp.stack([grads[li][k] for li in range(DEPTH)]) for k in WEIGHTS}


def kernel(x, p, w_in, conv_a_w, conv_a_b, ln_a_g, ln_a_b, w_a_out, ssm_conv_w, ssm_conv_b, a_log, dt_bias, d_skip, ssm_norm_g, w_b_out, w_o, ln1_g, ln1_b, w_gate_up, w_down, ln2_g, ln2_b, w_ple, ple_norm_g, w_ple_gate, loss_target, m_w_in, m_conv_a_w, m_conv_a_b, m_ln_a_g, m_ln_a_b, m_w_a_out, m_ssm_conv_w, m_ssm_conv_b, m_a_log, m_dt_bias, m_d_skip, m_ssm_norm_g, m_w_b_out, m_w_o, m_ln1_g, m_ln1_b, m_w_gate_up, m_w_down, m_ln2_g, m_ln2_b, m_w_ple, m_ple_norm_g, m_w_ple_gate, v_w_in, v_conv_a_w, v_conv_a_b, v_ln_a_g, v_ln_a_b, v_w_a_out, v_ssm_conv_w, v_ssm_conv_b, v_a_log, v_dt_bias, v_d_skip, v_ssm_norm_g, v_w_b_out, v_w_o, v_ln1_g, v_ln1_b, v_w_gate_up, v_w_down, v_ln2_g, v_ln2_b, v_w_ple, v_ple_norm_g, v_w_ple_gate):
    A = dict(locals())
    w = {k: A[k] for k in WEIGHTS}
    m = {k: A["m_" + k] for k in WEIGHTS}
    v = {k: A["v_" + k] for k in WEIGHTS}
    T = x.shape[1]
    big_shapes = [w[k].shape for k in BIG]
    small_shapes = [w[k].shape for k in SMALL]

    mm_names = [k for k in BIG if k not in CONV_W]
    shards = [_shard_for_gather(k, w[k].astype(BF16)) for k in mm_names]
    gathered = all_gather("gather_weights", _pack(shards, 16))
    FW = {k: _full_from_gathered(k, g) for k, g in zip(mm_names, _unpack(gathered, [t.shape for t in shards], (N_DEV,)))}
    FW["w_in"] = jnp.pad(FW["w_in"], ((0, 0), (0, N_IN_PAD - N_IN), (0, 0)))
    gathered = all_gather("gather_conv_weights", _pack([w[k] for k in CONV_W], SUBLANE))
    FW.update({k: _full_from_gathered(k, g)
               for k, g in zip(CONV_W, _unpack(gathered, [w[k].shape for k in CONV_W], (N_DEV,)))})
    FW.update({k: w[k] for k in SMALL})

    loss, grad_x, gfull = local_step(x[0], p[:, 0], loss_target[0], FW, T)
    loss = lax.psum(loss, ("x", "y", "c"))

    TR = 512
    flat = jnp.concatenate([_pieces_from_full(k, gfull[k]).astype(BF16) for k in BIG], axis=1)
    rows = -(-flat.shape[1] // (LANE * TR)) * TR
    gpack = jnp.pad(flat, ((0, 0), (0, rows * LANE - flat.shape[1]))).reshape(N_DEV, rows, LANE)
    landed = grad_exchange("grad_exchange", gpack)
    gsum = sum_slots("sum_grad_pieces", landed, TR)
    gshard = [jnp.swapaxes(g, 1, 2) if k in COL_T else g
              for k, g in zip(BIG, _unpack(gsum, [_piece_shape(k, w[k].shape) for k in BIG]))]
    res_big = adamw("adamw_big", _pack(gshard, TR)[None], _pack([w[k] for k in BIG], TR), _pack([m[k] for k in BIG], TR),
                    _pack([v[k] for k in BIG], TR), TR)
    res_big = [dict(zip(BIG, _unpack(r, big_shapes))) for r in res_big]

    spack = _pack([gfull[k].reshape(w[k].shape) for k in SMALL], SUBLANE)
    sall = all_gather("gather_small_grads", spack)
    rs = spack.shape[0]
    res_small = adamw("adamw_small", sall, _pack([w[k] for k in SMALL], SUBLANE), _pack([m[k] for k in SMALL], SUBLANE),
                      _pack([v[k] for k in SMALL], SUBLANE), rs)
    res_small = [dict(zip(SMALL, _unpack(r, small_shapes))) for r in res_small]

    outs = [loss, grad_x[None]]
    for q in range(4):
        for k in WEIGHTS:
            outs.append(res_big[q][k] if k in res_big[q] else res_small[q][k])
    return tuple(outs)
```

```python
import math

import jax
import jax.numpy as jnp
from jax import lax
from jax.experimental import pallas as pl
from jax.experimental.pallas import tpu as pltpu

F32 = jnp.float32
BF16 = jnp.bfloat16

D_MODEL = 1024
CONV_DIM = 1024
CONV_KERNEL = 31
D_INNER = 2048
HEAD_DIM = 64
N_HEADS = 32
N_GROUPS = 8
D_STATE = 128
SSM_CONV = 5
CHUNK = 128
XBC_DIM = D_INNER + 2 * N_GROUPS * D_STATE
FFN_DIM = 2816
PLE_DIM = 256
N_IN = 2 * CONV_DIM + 2 * D_MODEL + D_INNER + XBC_DIM + 2 * N_HEADS
N_IN_PAD = 10368
DEPTH = 2
N_DEV = 8
ALPHA = (2 * DEPTH) ** 0.25
LN_EPS = 1e-5
RMS_EPS = 1e-6
ADAM_LR, ADAM_B1, ADAM_B2, ADAM_EPS, ADAM_WD, ADAM_STEP = 0.001, 0.9, 0.999, 1e-08, 0.01, 10

LANE = 128
SUBLANE = 8
HALO = 16
VMEM_LIMIT = 52 * 1024 * 1024
NEG = -1e30

BIG = ["w_in", "conv_a_w", "w_a_out", "ssm_conv_w", "w_b_out", "w_o", "w_gate_up", "w_down", "w_ple", "w_ple_gate"]
BIG_AXIS = {"w_in": 2, "conv_a_w": 2, "w_a_out": 1, "ssm_conv_w": 2, "w_b_out": 1, "w_o": 1, "w_gate_up": 2,
            "w_down": 1, "w_ple": 2, "w_ple_gate": 1}
COL_T = ["w_in", "w_gate_up"]
CONV_W = ["conv_a_w", "ssm_conv_w"]
SMALL = ["conv_a_b", "ln_a_g", "ln_a_b", "ssm_conv_b", "a_log", "dt_bias", "d_skip", "ssm_norm_g", "ln1_g", "ln1_b",
         "ln2_g", "ln2_b", "ple_norm_g"]
WEIGHTS = ["w_in", "conv_a_w", "conv_a_b", "ln_a_g", "ln_a_b", "w_a_out", "ssm_conv_w", "ssm_conv_b", "a_log", "dt_bias",
           "d_skip", "ssm_norm_g", "w_b_out", "w_o", "ln1_g", "ln1_b", "w_gate_up", "w_down", "ln2_g", "ln2_b", "w_ple",
           "ple_norm_g", "w_ple_gate"]


def _cparams(sem):
    return pltpu.CompilerParams(dimension_semantics=sem, vmem_limit_bytes=VMEM_LIMIT)


def _pick(n, cap):
    if n <= cap:
        return n
    best = None
    for d in range(LANE, cap + 1, LANE):
        if n % d == 0:
            best = d
    assert best is not None, (n, cap)
    return best


def matmul(name, a, b, mode, out_dtype=F32, add=None):
    if mode == "nn":
        (M, K), (K2, N) = a.shape, b.shape
    elif mode == "nt":
        (M, K), (N, K2) = a.shape, b.shape
    else:
        (K, M), (K2, N) = a.shape, b.shape
    assert K == K2, (name, a.shape, b.shape)
    tm = _pick(M, 1024) if mode != "tn" else _pick(M, 1408)
    tn = _pick(N, 1408)
    tk = _pick(K, 512) if mode == "tn" else (K if (mode == "nn" and K <= 2816) else _pick(K, 1408))
    nk = K // tk
    grid = (M // tm, N // tn, nk)
    if mode == "tn":
        a_spec = pl.BlockSpec((tk, tm), lambda i, j, k: (k, i))
    else:
        a_spec = pl.BlockSpec((tm, tk), lambda i, j, k: (i, k))
    if mode == "nt":
        b_spec = pl.BlockSpec((tn, tk), lambda i, j, k: (j, k))
    else:
        b_spec = pl.BlockSpec((tk, tn), lambda i, j, k: (k, j))
    o_spec = pl.BlockSpec((tm, tn), lambda i, j, k: (i, j))
    dims = {"nn": ((1,), (0,)), "nt": ((1,), (1,)), "tn": ((0,), (0,))}[mode]
    has_add = add is not None

    def body(a_ref, b_ref, *rest):
        if has_add:
            add_ref, o_ref, *scr = rest
        else:
            o_ref, *scr = rest
        part = lax.dot_general(a_ref[...].astype(BF16), b_ref[...].astype(BF16), (dims, ((), ())),
                               preferred_element_type=F32)

        def finish(v):
            if has_add:
                v = v + add_ref[...].astype(F32)
            o_ref[...] = v.astype(o_ref.dtype)

        if nk == 1:
            finish(part)
        else:
            acc = scr[0]
            k = pl.program_id(2)

            @pl.when(k == 0)
            def _():
                acc[...] = part

            @pl.when(k > 0)
            def _():
                acc[...] += part

            @pl.when(k == nk - 1)
            def _():
                finish(acc[...])

    in_specs = [a_spec, b_spec] + ([o_spec] if has_add else [])
    args = (a, b) + ((add,) if has_add else ())
    return pl.pallas_call(
        body, out_shape=jax.ShapeDtypeStruct((M, N), out_dtype), grid=grid, in_specs=in_specs, out_specs=o_spec,
        scratch_shapes=[pltpu.VMEM((tm, tn), F32)] if nk > 1 else [], name=name,
        compiler_params=_cparams(("parallel", "parallel", "arbitrary")))(*args)


def _row_specs(items, tT, groups):
    specs = []
    for (_, w, blk) in items:
        assert blk % groups == 0
        specs.append(pl.BlockSpec((tT, w * groups), (lambda i, b=blk // groups: (i, b))))
    return specs


def _slices(v, groups):
    if groups == 1:
        return [v]
    w = v.shape[1] // groups
    return [v[:, w * s:w * (s + 1)] for s in range(groups)]


def _cat(vs):
    return vs[0] if len(vs) == 1 else jnp.concatenate(vs, axis=1)


def rowcall(name, fn, ins, pars, outs, T, tT=256, groups=1):
    n_in, n_par = len(ins), len(pars)
    intos = [o[2] for o in outs if o[2] is not None]
    in_specs = (_row_specs(ins, tT, groups) + [pl.BlockSpec(p.shape, lambda i: (0, 0)) for p in pars]
                + [pl.BlockSpec(memory_space=pl.ANY)] * len(intos))
    out_specs, out_shapes, aliases = [], [], {}
    n_alias = 0
    for oi, (w, dt, into) in enumerate(outs):
        if into is None:
            out_specs.append(pl.BlockSpec((tT, w * groups), lambda i: (i, 0)))
            out_shapes.append(jax.ShapeDtypeStruct((T, w * groups), dt))
        else:
            arr, blk = into
            out_specs.append(pl.BlockSpec((tT, w * groups), lambda i, b=blk // groups: (i, b)))
            out_shapes.append(jax.ShapeDtypeStruct(arr.shape, arr.dtype))
            aliases[n_in + n_par + n_alias] = oi
            n_alias += 1

    def body(*refs):
        xs = [_slices(r[...].astype(F32), groups) for r in refs[:n_in]]
        ps = [_slices(r[...], groups) for r in refs[n_in:n_in + n_par]]
        o_refs = refs[n_in + n_par + n_alias:]
        res = [fn(*[x[s] for x in xs], *[p[s] for p in ps]) for s in range(groups)]
        for k, r in enumerate(o_refs):
            r[...] = _cat([res[s][k] for s in range(groups)]).astype(r.dtype)

    res = pl.pallas_call(
        body, out_shape=out_shapes, grid=(T // tT,), in_specs=in_specs, out_specs=out_specs,
        input_output_aliases=aliases, name=name, compiler_params=_cparams(("parallel",)))(
            *[a for (a, _, _) in ins], *pars, *[a for (a, _) in intos])
    return list(res)


def rowvjp(name, fn, ins, pars, cts, douts, T, tT=256, groups=1):
    n_in, n_par, n_ct = len(ins), len(pars), len(cts)
    intos = [o[2] for o in douts if o[2] is not None and not isinstance(o[2][0], jax.ShapeDtypeStruct)]
    in_specs = (_row_specs(ins, tT, groups) + [pl.BlockSpec(p.shape, lambda i: (0, 0)) for p in pars]
                + _row_specs(cts, tT, groups) + [pl.BlockSpec(memory_space=pl.ANY)] * len(intos))
    out_specs, out_shapes, aliases = [], [], {}
    n_alias = 0
    for oi, (idxs, dt, into) in enumerate(douts):
        w = sum(ins[k][1] for k in idxs) * groups
        if into is None:
            out_specs.append(pl.BlockSpec((tT, w), lambda i: (i, 0)))
            out_shapes.append(jax.ShapeDtypeStruct((T, w), dt))
        else:
            assert len(idxs) == 1 or groups == 1
            arr, blk = into
            out_specs.append(pl.BlockSpec((tT, w), lambda i, b=blk // groups: (i, b)))
            out_shapes.append(jax.ShapeDtypeStruct(arr.shape, arr.dtype))
            if not isinstance(arr, jax.ShapeDtypeStruct):
                aliases[n_in + n_par + n_ct + n_alias] = oi
                n_alias += 1
    n_dout = len(douts)
    for p in pars:
        out_specs.append(pl.BlockSpec(p.shape, lambda i: (0, 0)))
        out_shapes.append(jax.ShapeDtypeStruct(p.shape, F32))

    def body(*refs):
        xs = [_slices(r[...].astype(F32), groups) for r in refs[:n_in]]
        ps = [_slices(r[...], groups) for r in refs[n_in:n_in + n_par]]
        cs = [_slices(r[...].astype(F32), groups) for r in refs[n_in + n_par:n_in + n_par + n_ct]]
        o_refs = refs[n_in + n_par + n_ct + n_alias:]
        grads = []
        for s in range(groups):
            _, vjp_fn = jax.vjp(fn, *[x[s] for x in xs], *[p[s] for p in ps])
            grads.append(vjp_fn(tuple(c[s] for c in cs)))
        for r, (idxs, _, _) in zip(o_refs[:n_dout], douts):
            r[...] = _cat([grads[s][k] for k in idxs for s in range(groups)]).astype(r.dtype)
        for k, r in enumerate(o_refs[n_dout:]):
            @pl.when(pl.program_id(0) == 0)
            def _(r=r):
                r[...] = jnp.zeros(r.shape, F32)
            r[...] += _cat([grads[s][n_in + k] for s in range(groups)])

    res = pl.pallas_call(
        body, out_shape=out_shapes, grid=(T // tT,), in_specs=in_specs, out_specs=out_specs,
        input_output_aliases=aliases, name=name, compiler_params=_cparams(("arbitrary",)))(
            *[a for (a, _, _) in ins], *pars, *[a for (a, _, _) in cts], *[a for (a, _) in intos])
    res = list(res)
    return res[:n_dout], res[n_dout:]


def _sigmoid(x):
    return 1.0 / (1.0 + jnp.exp(-x))


def _silu(x):
    return x * _sigmoid(x)


def _softplus(x):
    return jnp.maximum(x, 0.0) + jnp.log(1.0 + jnp.exp(-jnp.abs(x)))


def _ln(x, g, b):
    mu = jnp.mean(x, axis=-1, keepdims=True)
    xc = x - mu
    var = jnp.mean(xc * xc, axis=-1, keepdims=True)
    return xc * lax.rsqrt(var + LN_EPS) * g + b


def glu_fn(a, gt):
    return (a * _sigmoid(gt),)


def lnsilu_fn(u, g, b):
    return (_silu(_ln(u, g, b)),)


def gnorm_fn(yf, yb, xp, z, dsk, ng):
    y = (yf + yb + _silu(xp) * dsk) * _silu(z)
    return (y * lax.rsqrt(jnp.mean(y * y, axis=-1, keepdims=True) + RMS_EPS) * ng,)


def merge_fn(ga, gb, ya, yb):
    return (_sigmoid(ga) * ya + _sigmoid(gb) * yb,)


def resln_fn(x, r, g, b):
    return (_ln(ALPHA * x + r, g, b),)


def swiglu_fn(g, u):
    return (_silu(g) * u,)


def ple_fn(h2, pe, gl, g):
    e = pe * lax.rsqrt(jnp.mean(pe * pe, axis=-1, keepdims=True) + RMS_EPS) * g
    return (h2 + e * _sigmoid(gl),)


def ident_fn(v):
    return (v,)


CONV_CB = 512
CONV_TT = 512
CONV_TILES = 4
CONV_RB = CONV_TILES * SUBLANE
CONV_RED_TILES = 2


def _conv_specs(blk0, T, tT, cb):
    nh = tT // HALO
    cur = pl.BlockSpec((tT, cb), lambda j, i: (i, blk0 + j))
    prev = pl.BlockSpec((HALO, cb), lambda j, i: (jnp.maximum(i * nh - 1, 0), blk0 + j))
    nxt = pl.BlockSpec((HALO, cb), lambda j, i: (jnp.minimum((i + 1) * nh, T // HALO - 1), blk0 + j))
    return [prev, cur, nxt]


def _phases(offsets):
    return sorted({off % SUBLANE for off in offsets})


def _fill_shifted(sh_ref, pad_ref, prev_ref, cur_ref, next_ref, i, n_t, tT, phases):
    pad_ref[pl.ds(0, HALO), :] = prev_ref[...].astype(F32) * (i > 0).astype(F32)
    pad_ref[pl.ds(HALO, tT), :] = cur_ref[...].astype(F32)
    pad_ref[pl.ds(HALO + tT, HALO), :] = next_ref[...].astype(F32) * (i < n_t - 1).astype(F32)
    for ph in phases:
        sh_ref[ph] = pad_ref[pl.ds(ph, tT + 3 * SUBLANE), :]


def _taps(sh_ref, base, ls, offsets, n_tiles):
    out = []
    for off in offsets:
        q, ph = divmod(off, SUBLANE)
        out.append(tuple(sh_ref[ph, pl.ds(base + SUBLANE * (q + t), SUBLANE), ls] for t in range(n_tiles)))
    return out


def _conv_rows(sh_ref, w_ref, bias, o_ref, offsets, tT, cb):
    K = len(offsets)
    for lt in range(cb // LANE):
        ls = slice(LANE * lt, LANE * (lt + 1))
        wv = [jnp.broadcast_to(w_ref[k:k + 1, ls], (SUBLANE, LANE)) for k in range(K)]
        b0 = jnp.zeros((SUBLANE, LANE), F32) if bias is None else jnp.broadcast_to(bias[:, ls], (SUBLANE, LANE))

        def rows(r, carry, ls=ls, wv=wv, b0=b0):
            base = pl.multiple_of(r * CONV_RB, CONV_RB)
            accs = [b0] * CONV_TILES
            for k, ds in enumerate(_taps(sh_ref, base, ls, offsets, CONV_TILES)):
                accs = [a + d * wv[k] for a, d in zip(accs, ds)]
            o_ref[pl.ds(base, CONV_RB), ls] = jnp.concatenate(accs, axis=0).astype(o_ref.dtype)
            return carry

        lax.fori_loop(0, tT // CONV_RB, rows, 0)


def conv_fwd(name, u, blk0, w, b, T):
    K, C = w.shape
    P = (K - 1) // 2
    tT, cb = min(CONV_TT, T), CONV_CB
    n_t = T // tT

    def body(prev_ref, cur_ref, next_ref, w_ref, b_ref, o_ref, pad_ref, sh_ref):
        offsets = [HALO - P + k for k in range(K)]
        _fill_shifted(sh_ref, pad_ref, prev_ref, cur_ref, next_ref, pl.program_id(1), n_t, tT, _phases(offsets))
        _conv_rows(sh_ref, w_ref, b_ref[...], o_ref, offsets, tT, cb)

    return pl.pallas_call(
        body, out_shape=jax.ShapeDtypeStruct((T, C), F32), grid=(C // cb, n_t),
        in_specs=_conv_specs(blk0, T, tT, cb) + [pl.BlockSpec((K, cb), lambda j, i: (0, j)),
                                                  pl.BlockSpec((1, cb), lambda j, i: (0, j))],
        out_specs=pl.BlockSpec((tT, cb), lambda j, i: (i, j)),
        scratch_shapes=[pltpu.VMEM((tT + 2 * HALO, cb), F32), pltpu.VMEM((SUBLANE, tT + 3 * SUBLANE, cb), F32)],
        name=name, compiler_params=_cparams(("parallel", "arbitrary")))(u, u, u, w, b)


def conv_bwd(name, dy, u, blk0, w, T, into=None):
    K, C = w.shape
    P = (K - 1) // 2
    tT, cb = min(CONV_TT, T), CONV_CB
    n_t = T // tT

    def body(dprev, dcur, dnext, uprev, ucur, unext, w_ref, *rest):
        if into is not None:
            rest = rest[1:]
        du_ref, dw_ref, db_ref, pad_ref, shd_ref, shu_ref = rest
        i = pl.program_id(1)
        offsets = [HALO - P + k for k in range(K)]
        back = [HALO + P - k for k in range(K)]
        _fill_shifted(shd_ref, pad_ref, dprev, dcur, dnext, i, n_t, tT, _phases(back + [HALO]))
        _fill_shifted(shu_ref, pad_ref, uprev, ucur, unext, i, n_t, tT, _phases(offsets))
        _conv_rows(shd_ref, w_ref, None, du_ref, back, tT, cb)

        @pl.when(i == 0)
        def _():
            dw_ref[...] = jnp.zeros(dw_ref.shape, F32)
            db_ref[...] = jnp.zeros(db_ref.shape, F32)

        rb = CONV_RED_TILES * SUBLANE
        for lt in range(cb // LANE):
            ls = slice(LANE * lt, LANE * (lt + 1))

            def red(r, accs, ls=ls):
                base = pl.multiple_of(r * rb, rb)
                (d0, d1), = _taps(shd_ref, base, ls, [HALO], CONV_RED_TILES)
                new = [acc + d0 * u0 + d1 * u1
                       for acc, (u0, u1) in zip(accs[:K], _taps(shu_ref, base, ls, offsets, CONV_RED_TILES))]
                return tuple(new) + (accs[K] + d0 + d1,)

            zero = jnp.zeros((SUBLANE, LANE), F32)
            accs = lax.fori_loop(0, tT // rb, red, (zero,) * (K + 1))
            for k in range(K):
                dw_ref[k:k + 1, ls] += jnp.sum(accs[k], axis=0, keepdims=True)
            db_ref[:, ls] += jnp.sum(accs[K], axis=0, keepdims=True)

    dspecs = _conv_specs(0, T, tT, cb)
    uspecs = _conv_specs(blk0, T, tT, cb)
    in_specs = dspecs + uspecs + [pl.BlockSpec((K, cb), lambda j, i: (0, j))]
    args = [dy, dy, dy, u, u, u, w]
    aliases = {}
    if into is None:
        du_spec = pl.BlockSpec((tT, cb), lambda j, i: (i, j))
        du_shape = jax.ShapeDtypeStruct((T, C), F32)
    else:
        arr, oblk = into
        in_specs.append(pl.BlockSpec(memory_space=pl.ANY))
        args.append(arr)
        aliases = {7: 0}
        du_spec = pl.BlockSpec((tT, cb), lambda j, i: (i, oblk + j))
        du_shape = jax.ShapeDtypeStruct(arr.shape, arr.dtype)
    return pl.pallas_call(
        body, out_shape=[du_shape, jax.ShapeDtypeStruct((K, C), F32), jax.ShapeDtypeStruct((1, C), F32)],
        grid=(C // cb, n_t), in_specs=in_specs,
        out_specs=[du_spec, pl.BlockSpec((K, cb), lambda j, i: (0, j)), pl.BlockSpec((1, cb), lambda j, i: (0, j))],
        scratch_shapes=[pltpu.VMEM((tT + 2 * HALO, cb), F32), pltpu.VMEM((SUBLANE, tT + 3 * SUBLANE, cb), F32),
                        pltpu.VMEM((SUBLANE, tT + 3 * SUBLANE, cb), F32)],
        input_output_aliases=aliases, name=name, compiler_params=_cparams(("arbitrary", "arbitrary")))(*args)


def _dot(a, b, dims):
    return lax.dot_general(a.astype(BF16), b.astype(BF16), (dims, ((), ())), preferred_element_type=F32)


def _dnn(a, b):
    return _dot(a, b, ((1,), (0,)))


def _dnt(a, b):
    return _dot(a, b, ((1,), (1,)))


def _dtn(a, b):
    return _dot(a.T, b, ((1,), (0,)))


@jax.custom_vjp
def _nn(a, b):
    return _dnn(a, b)


_nn.defvjp(lambda a, b: (_dnn(a, b), (a, b)), lambda r, g: (_dnt(g, r[1]), _dtn(r[0], g)))


@jax.custom_vjp
def _nt(a, b):
    return _dnt(a, b)


_nt.defvjp(lambda a, b: (_dnt(a, b), (a, b)), lambda r, g: (_dnn(g, r[1]), _dtn(g, r[0])))


@jax.custom_vjp
def _tn(a, b):
    return _dtn(a, b)


_tn.defvjp(lambda a, b: (_dtn(a, b), (a, b)), lambda r, g: (_dnt(r[1], g), _dnn(r[0], g)))


def _split_dot(m, v):
    hi = v.astype(BF16)
    r1 = v - hi.astype(F32)
    mid = r1.astype(BF16)
    lo = (r1 - mid.astype(F32)).astype(BF16)
    mb = m.astype(BF16)
    d = lambda x: lax.dot_general(mb, x, (((1,), (0,)), ((), ())), preferred_element_type=F32)
    return d(hi) + d(mid) + d(lo)


@jax.custom_vjp
def _tri_dot(tri, tri_t, v):
    return _split_dot(tri, v)


_tri_dot.defvjp(lambda tri, tri_t, v: (_split_dot(tri, v), (tri, tri_t)),
                lambda r, g: (jnp.zeros_like(r[0]), jnp.zeros_like(r[1]), _split_dot(r[1], g)))


def _ssd_consts(dirn):
    ri = lax.broadcasted_iota(jnp.int32, (CHUNK, CHUNK), 0)
    ci = lax.broadcasted_iota(jnp.int32, (CHUNK, CHUNK), 1)
    keep = (ci <= ri) if dirn == 0 else (ci >= ri)
    tri = keep.astype(F32)
    tri_t = (~keep | (ci == ri)).astype(F32)
    lane = lax.broadcasted_iota(jnp.int32, (1, LANE), 1)
    sub = lax.broadcasted_iota(jnp.int32, (CHUNK, 1), 0)
    end = (sub == (CHUNK - 1 if dirn == 0 else 0)).astype(F32)
    lo_half = lane < HEAD_DIM
    oh_lane = [(lane == N_HEADS * dirn + h).astype(F32) for h in range(N_HEADS)]
    oh_sub = [(sub == N_HEADS * dirn + h).astype(F32) for h in range(N_HEADS)]
    return keep, tri, tri_t, end, lo_half, oh_lane, oh_sub


def _ssd_chunk(consts, x_t, b_t, c_t, dtr, bias, alog, h_t):
    keep, tri, tri_t, end, lo_half, oh_lane, oh_sub = consts
    dt = _softplus(dtr + bias)
    a = dt * (-jnp.exp(alog))
    cs = _tri_dot(tri, tri_t, a)
    cs_t = cs.T
    tot = jnp.sum(cs * end, axis=0, keepdims=True)
    ys, hn = [], []
    for g in range(N_GROUPS):
        bm, cm = _silu(b_t[g]), _silu(c_t[g])
        gm = _nt(cm, bm)
        for jj in range(2):
            j = 2 * g + jj
            h0, h1 = 2 * j, 2 * j + 1
            col = [jnp.sum(cs * oh_lane[h], axis=1, keepdims=True) for h in (h0, h1)]
            row = [jnp.sum(cs_t * oh_sub[h], axis=0, keepdims=True) for h in (h0, h1)]
            dth = [jnp.sum(dt * oh_lane[h], axis=1, keepdims=True) for h in (h0, h1)]
            toth = [jnp.sum(tot * oh_lane[h], axis=1, keepdims=True) for h in (h0, h1)]
            xd = _silu(x_t[j]) * jnp.where(lo_half, dth[0], dth[1])
            yd = [_nn(gm * jnp.exp(jnp.where(keep, col[k] - row[k], NEG)), xd) for k in range(2)]
            cp = jnp.where(lo_half, col[0], col[1])
            tp = jnp.where(lo_half, toth[0], toth[1])
            ys.append(jnp.where(lo_half, yd[0], yd[1]) + _nn(cm, h_t[j]) * jnp.exp(cp))
            hn.append(h_t[j] * jnp.exp(tp) + _tn(bm, xd * jnp.exp(tp - cp)))
    return ys, hn


N_PAIR = D_INNER // LANE


def _tiles(ref, n):
    return [ref[:, LANE * j:LANE * (j + 1)].astype(F32) for j in range(n)]


def _ssd_in_specs(cmap):
    return [pl.BlockSpec((CHUNK, D_INNER), lambda i: (cmap(i), 0)),
            pl.BlockSpec((CHUNK, N_GROUPS * D_STATE), lambda i: (cmap(i), 2)),
            pl.BlockSpec((CHUNK, N_GROUPS * D_STATE), lambda i: (cmap(i), 3)),
            pl.BlockSpec((CHUNK, LANE), lambda i: (cmap(i), (N_IN_PAD - LANE) // LANE)),
            pl.BlockSpec((1, LANE), lambda i: (0, 0)), pl.BlockSpec((1, LANE), lambda i: (0, 0))]


def ssd_fwd(name, xbc, proj, bias_row, alog_row, dirn, T):
    nc = T // CHUNK
    cmap = (lambda i: i) if dirn == 0 else (lambda i: nc - 1 - i)

    def body(x_ref, b_ref, c_ref, dt_ref, bias_ref, alog_ref, y_ref, hs_ref, h_scr):
        @pl.when(pl.program_id(0) == 0)
        def _():
            h_scr[...] = jnp.zeros(h_scr.shape, F32)

        hs_ref[0] = h_scr[...]
        ys, hn = _ssd_chunk(_ssd_consts(dirn), _tiles(x_ref, N_PAIR), _tiles(b_ref, N_GROUPS), _tiles(c_ref, N_GROUPS),
                            dt_ref[...], bias_ref[...], alog_ref[...], _tiles(h_scr, N_PAIR))
        for j in range(N_PAIR):
            y_ref[:, LANE * j:LANE * (j + 1)] = ys[j]
            h_scr[:, LANE * j:LANE * (j + 1)] = hn[j]

    return pl.pallas_call(
        body, out_shape=[jax.ShapeDtypeStruct((T, D_INNER), F32), jax.ShapeDtypeStruct((nc, D_STATE, D_INNER), F32)],
        grid=(nc,), in_specs=_ssd_in_specs(cmap),
        out_specs=[pl.BlockSpec((CHUNK, D_INNER), lambda i: (cmap(i), 0)),
                   pl.BlockSpec((1, D_STATE, D_INNER), lambda i: (cmap(i), 0, 0))],
        scratch_shapes=[pltpu.VMEM((D_STATE, D_INNER), F32)], name=name,
        compiler_params=_cparams(("arbitrary",)))(xbc, xbc, xbc, proj, bias_row, alog_row)


def ssd_bwd(name, xbc, proj, bias_row, alog_row, hs, dy, adds, dirn, T):
    nc = T // CHUNK
    cmap = (lambda i: nc - 1 - i) if dirn == 0 else (lambda i: i)
    GS = N_GROUPS * D_STATE

    n_add = len(adds)

    def body(x_ref, b_ref, c_ref, dt_ref, bias_ref, alog_ref, hs_ref, dy_ref, *rest):
        add_refs, (dx_ref, db_ref, dc_ref, ddt_ref, dbias_ref, dalog_ref, dh_scr) = rest[:n_add], rest[n_add:]
        ax_ref = add_refs[0]
        ab_ref, ac_ref, adt_ref = add_refs[1:] if n_add == 4 else (None, None, None)
        first = pl.program_id(0) == 0

        @pl.when(first)
        def _():
            dh_scr[...] = jnp.zeros(dh_scr.shape, F32)
            dbias_ref[...] = jnp.zeros(dbias_ref.shape, F32)
            dalog_ref[...] = jnp.zeros(dalog_ref.shape, F32)

        consts = _ssd_consts(dirn)
        fn = lambda *a: _ssd_chunk(consts, *a)
        _, vjp_fn = jax.vjp(fn, _tiles(x_ref, N_PAIR), _tiles(b_ref, N_GROUPS), _tiles(c_ref, N_GROUPS), dt_ref[...],
                            bias_ref[...], alog_ref[...], [hs_ref[0, :, LANE * j:LANE * (j + 1)] for j in range(N_PAIR)])
        dx, db, dc, ddt, dbias, dalog, dh = vjp_fn((_tiles(dy_ref, N_PAIR), _tiles(dh_scr, N_PAIR)))
        for j in range(N_PAIR):
            s = slice(LANE * j, LANE * (j + 1))
            dx_ref[:, s] = dx[j] + ax_ref[:, s]
            dh_scr[:, s] = dh[j]
        for g in range(N_GROUPS):
            s = slice(LANE * g, LANE * (g + 1))
            db_ref[:, s] = db[g] + (ab_ref[:, s] if n_add == 4 else 0.0)
            dc_ref[:, s] = dc[g] + (ac_ref[:, s] if n_add == 4 else 0.0)
        ddt_ref[...] = ddt + (adt_ref[...] if n_add == 4 else 0.0)
        dbias_ref[...] += dbias
        dalog_ref[...] += dalog

    blk = lambda w: pl.BlockSpec((CHUNK, w), lambda i: (cmap(i), 0))
    row = pl.BlockSpec((1, LANE), lambda i: (0, 0))
    return pl.pallas_call(
        body,
        out_shape=[jax.ShapeDtypeStruct((T, D_INNER), F32), jax.ShapeDtypeStruct((T, GS), F32),
                   jax.ShapeDtypeStruct((T, GS), F32), jax.ShapeDtypeStruct((T, LANE), F32),
                   jax.ShapeDtypeStruct((1, LANE), F32), jax.ShapeDtypeStruct((1, LANE), F32)],
        grid=(nc,),
        in_specs=_ssd_in_specs(cmap) + [pl.BlockSpec((1, D_STATE, D_INNER), lambda i: (cmap(i), 0, 0)), blk(D_INNER)]
        + [blk(D_INNER), blk(GS), blk(GS), blk(LANE)][:n_add],
        out_specs=[blk(D_INNER), blk(GS), blk(GS), blk(LANE), row, row],
        scratch_shapes=[pltpu.VMEM((D_STATE, D_INNER), F32)], name=name,
        compiler_params=_cparams(("arbitrary",)))(xbc, xbc, xbc, proj, bias_row, alog_row, hs, dy, *adds)


def loss_head(y, target, T, tT=256):
    def body(y_ref, t_ref, dy_ref, sq_ref):
        @pl.when(pl.program_id(0) == 0)
        def _():
            sq_ref[...] = jnp.zeros(sq_ref.shape, F32)
        e = y_ref[...] - t_ref[...]
        dy_ref[...] = e * (1.0 / D_MODEL)
        sq_ref[...] += jnp.sum(e * e, axis=0, keepdims=True)

    spec = pl.BlockSpec((tT, D_MODEL), lambda i: (i, 0))
    return pl.pallas_call(
        body, out_shape=[jax.ShapeDtypeStruct((T, D_MODEL), F32), jax.ShapeDtypeStruct((1, D_MODEL), F32)],
        grid=(T // tT,), in_specs=[spec, spec], out_specs=[spec, pl.BlockSpec((1, D_MODEL), lambda i: (0, 0))],
        name="loss_head", compiler_params=_cparams(("arbitrary",)))(y, target)


MESH_ID = pl.DeviceIdType.MESH


def all_gather(name, v):
    R, W = v.shape

    def body(v_ref, out_ref, send_sems, recv_sems, local_sem):
        x, y, c = lax.axis_index("x"), lax.axis_index("y"), lax.axis_index("c")
        me, sibling = (x, y, c), (x, y, 1 - c)
        chips = [(1 - x, y), (x, 1 - y), (1 - x, 1 - y)]

        def slot(px, py, pc):
            return out_ref.at[4 * px + 2 * py + pc]

        def copy(k, block, to, src=None):
            return pltpu.make_async_remote_copy(
                src_ref=slot(*block) if src is None else src, dst_ref=slot(*block), send_sem=send_sems.at[k],
                recv_sem=recv_sems.at[k], device_id=to, device_id_type=MESH_ID)

        mine = pltpu.make_async_copy(v_ref, slot(*me), local_sem)
        mine.start()
        first = [copy(0, me, sibling, src=v_ref)]
        first += [copy(1 + j, me, (*chip, c), src=v_ref) for j, chip in enumerate(chips)]
        for cp in first:
            cp.start()
        passed = [copy(4 + j, (*chip, c), sibling) for j, chip in enumerate(chips)]
        for j, chip in enumerate(chips):
            copy(1 + j, (*chip, c), me).wait_recv()
            passed[j].start()
        copy(0, sibling, me).wait_recv()
        for j, chip in enumerate(chips):
            copy(4 + j, (*chip, 1 - c), me).wait_recv()
        for cp in first + passed:
            cp.wait_send()
        mine.wait()

    return pl.pallas_call(
        body, out_shape=jax.ShapeDtypeStruct((N_DEV, R, W), v.dtype),
        in_specs=[pl.BlockSpec(memory_space=pl.ANY)], out_specs=pl.BlockSpec(memory_space=pl.ANY),
        scratch_shapes=[pltpu.SemaphoreType.DMA((7,)), pltpu.SemaphoreType.DMA((7,)), pltpu.SemaphoreType.DMA],
        name=name, compiler_params=pltpu.CompilerParams(has_side_effects=True))(v)


def grad_exchange(name, g):
    _, R, W = g.shape

    def body(g_ref, out_ref, send_sems, recv_sems, local_sem):
        x, y, c = lax.axis_index("x"), lax.axis_index("y"), lax.axis_index("c")
        me = 4 * x + 2 * y + c
        mine = pltpu.make_async_copy(g_ref.at[me], out_ref.at[me], local_sem)
        mine.start()
        copies = []
        for k in range(1, N_DEV):
            kx, ky, kc = (k >> 2) & 1, (k >> 1) & 1, k & 1
            px = 1 - x if kx else x
            py = 1 - y if ky else y
            pc = 1 - c if kc else c
            copies.append(pltpu.make_async_remote_copy(
                src_ref=g_ref.at[4 * px + 2 * py + pc], dst_ref=out_ref.at[me], send_sem=send_sems.at[k - 1],
                recv_sem=recv_sems.at[k - 1], device_id=(px, py, pc), device_id_type=MESH_ID))
        for cp in copies:
            cp.start()
        for cp in copies:
            cp.wait_recv()
        for cp in copies:
            cp.wait_send()
        mine.wait()

    return pl.pallas_call(
        body, out_shape=jax.ShapeDtypeStruct(g.shape, g.dtype),
        in_specs=[pl.BlockSpec(memory_space=pl.ANY)], out_specs=pl.BlockSpec(memory_space=pl.ANY),
        scratch_shapes=[pltpu.SemaphoreType.DMA((7,)), pltpu.SemaphoreType.DMA((7,)), pltpu.SemaphoreType.DMA],
        name=name, compiler_params=pltpu.CompilerParams(has_side_effects=True))(g)


def sum_slots(name, parts, tr):
    n_slot, R, W = parts.shape

    def body(p_ref, o_ref):
        g = p_ref[0].astype(F32)
        for s in range(1, n_slot):
            g = g + p_ref[s].astype(F32)
        o_ref[...] = g

    return pl.pallas_call(
        body, out_shape=jax.ShapeDtypeStruct((R, W), F32), grid=(R // tr,),
        in_specs=[pl.BlockSpec((n_slot, tr, W), lambda i: (0, i, 0))], out_specs=pl.BlockSpec((tr, W), lambda i: (i, 0)),
        name=name, compiler_params=_cparams(("parallel",)))(parts)


def adamw(name, parts, w, m, v, tr):
    R, W = w.shape
    n_slot = parts.shape[0]
    c1 = 1.0 / (1.0 - ADAM_B1 ** ADAM_STEP)
    c2 = 1.0 / (1.0 - ADAM_B2 ** ADAM_STEP)

    def body(p_ref, w_ref, m_ref, v_ref, g_ref, d_ref, nm_ref, nv_ref):
        g = p_ref[0]
        for s in range(1, n_slot):
            g = g + p_ref[s]
        nm = ADAM_B1 * m_ref[...] + (1.0 - ADAM_B1) * g
        nv = ADAM_B2 * v_ref[...] + (1.0 - ADAM_B2) * (g * g)
        g_ref[...] = g
        nm_ref[...] = nm
        nv_ref[...] = nv
        d_ref[...] = -ADAM_LR * ((nm * c1) / (jnp.sqrt(nv * c2) + ADAM_EPS) + ADAM_WD * w_ref[...])

    spec = pl.BlockSpec((tr, W), lambda i: (i, 0))
    return pl.pallas_call(
        body, out_shape=[jax.ShapeDtypeStruct((R, W), F32)] * 4, grid=(R // tr,),
        in_specs=[pl.BlockSpec((n_slot, tr, W), lambda i: (0, i, 0)), spec, spec, spec], out_specs=[spec] * 4,
        name=name, compiler_params=_cparams(("parallel",)))(parts, w, m, v)


def _pack(arrs, row_mult):
    flat = jnp.concatenate([a.reshape(-1) for a in arrs])
    n = flat.shape[0]
    rows = -(-n // LANE)
    rows = -(-rows // row_mult) * row_mult
    return jnp.pad(flat, (0, rows * LANE - n)).reshape(rows, LANE)


def _unpack(buf, shapes, lead=()):
    flat = buf.reshape(lead + (-1,))
    out, off = [], 0
    for s in shapes:
        n = math.prod(s)
        out.append(flat[..., off:off + n].reshape(lead + tuple(s)))
        off += n
    return out


def _rows_sharded(name):
    return name in COL_T or BIG_AXIS[name] == 1


def _shard_for_gather(name, w):
    return jnp.swapaxes(w, 1, 2) if name in COL_T else w


def _full_from_gathered(name, g):
    if _rows_sharded(name):
        return jnp.transpose(g, (1, 0, 2, 3)).reshape(g.shape[1], N_DEV * g.shape[2], g.shape[3])
    return jnp.transpose(g, (1, 2, 0, 3)).reshape(g.shape[1], g.shape[2], N_DEV * g.shape[3])


def _pieces_from_full(name, f):
    A, B = f.shape
    if _rows_sharded(name):
        return f.reshape(N_DEV, -1)
    return jnp.transpose(f.reshape(A, N_DEV, B // N_DEV), (1, 0, 2)).reshape(N_DEV, -1)


def _piece_shape(name, shard_shape):
    L, a, b = shard_shape
    return (L, b, a) if name in COL_T else (L, a, b)


def _rows(v):
    return v.reshape(1, -1).astype(F32)


def _head_rows(W):
    bias = jnp.pad(W["dt_bias"].reshape(1, -1), ((0, 0), (0, LANE - 2 * N_HEADS)))
    alog = jnp.pad(W["a_log"].reshape(1, -1), ((0, 0), (0, LANE - 2 * N_HEADS)))
    return bias, alog


def layer_fwd(li, x, p_l, W, T):
    n = lambda s: f"l{li}_{s}"
    S = {"x": x}
    proj = matmul(n("mm_in"), x, W["w_in"], "nt")
    (u0,) = rowcall(n("glu"), glu_fn, [(proj, 1024, 0), (proj, 1024, 1)], [], [(1024, F32, None)], T)
    u1 = conv_fwd(n("conv_a"), u0, 0, W["conv_a_w"], _rows(W["conv_a_b"]), T)
    (u3,) = rowcall(n("lnsilu"), lnsilu_fn, [(u1, 1024, 0)], [_rows(W["ln_a_g"]), _rows(W["ln_a_b"])],
                    [(1024, BF16, None)], T)
    y_a = matmul(n("mm_aout"), u3, W["w_a_out"], "nn")
    xbc = conv_fwd(n("conv_s"), proj, 6144 // CONV_CB, W["ssm_conv_w"], _rows(W["ssm_conv_b"]), T)
    bias_row, alog_row = _head_rows(W)
    y_f, hs_f = ssd_fwd(n("ssd_f"), xbc, proj, bias_row, alog_row, 0, T)
    y_b, hs_b = ssd_fwd(n("ssd_r"), xbc, proj, bias_row, alog_row, 1, T)
    dsk = jnp.repeat(W["d_skip"], HEAD_DIM).reshape(1, D_INNER)
    (yn,) = rowcall(n("gnorm"), gnorm_fn, [(y_f, 256, 0), (y_b, 256, 0), (xbc, 256, 0), (proj, 256, 16)],
                    [dsk, _rows(W["ssm_norm_g"])], [(256, BF16, None)], T, groups=N_GROUPS)
    y_bo = matmul(n("mm_bout"), yn, W["w_b_out"], "nn")
    (merged,) = rowcall(n("merge"), merge_fn, [(proj, 1024, 2), (proj, 1024, 3), (y_a, 1024, 0), (y_bo, 1024, 0)], [],
                        [(1024, BF16, None)], T)
    mix = matmul(n("mm_o"), merged, W["w_o"], "nn")
    (h,) = rowcall(n("ln1"), resln_fn, [(x, 1024, 0), (mix, 1024, 0)], [_rows(W["ln1_g"]), _rows(W["ln1_b"])],
                   [(1024, F32, None)], T)
    gu = matmul(n("mm_gu"), h, W["w_gate_up"], "nt")
    (act,) = rowcall(n("swiglu"), swiglu_fn, [(gu, FFN_DIM, 0), (gu, FFN_DIM, 1)], [], [(FFN_DIM, BF16, None)], T)
    dn = matmul(n("mm_down"), act, W["w_down"], "nn")
    (h2,) = rowcall(n("ln2"), resln_fn, [(h, 1024, 0), (dn, 1024, 0)], [_rows(W["ln2_g"]), _rows(W["ln2_b"])],
                    [(1024, F32, None)], T)
    pe = matmul(n("mm_ple"), p_l, W["w_ple"], "nn")
    gl = matmul(n("mm_pg"), h2, W["w_ple_gate"], "nn")
    (xn,) = rowcall(n("pleout"), ple_fn, [(h2, 1024, 0), (pe, 1024, 0), (gl, 1024, 0)], [_rows(W["ple_norm_g"])],
                    [(1024, F32, None)], T)
    S.update(proj=proj, u0=u0, u1=u1, u3=u3, y_a=y_a, xbc=xbc, y_f=y_f, y_b=y_b, hs_f=hs_f, hs_b=hs_b, yn=yn, y_bo=y_bo,
             merged=merged, mix=mix, h=h, gu=gu, act=act, dn=dn, h2=h2, pe=pe, gl=gl, dsk=dsk, bias_row=bias_row,
             alog_row=alog_row)
    return xn, S


def layer_bwd(li, dxn, p_l, W, S, T):
    n = lambda s: f"l{li}_{s}"
    G = {}
    x, proj = S["x"], S["proj"]
    (dh2a, dpe, dgl), (dpg,) = rowvjp(
        n("pleout_b"), ple_fn, [(S["h2"], 1024, 0), (S["pe"], 1024, 0), (S["gl"], 1024, 0)], [_rows(W["ple_norm_g"])],
        [(dxn, 1024, 0)], [([0], F32, None), ([1], BF16, None), ([2], BF16, None)], T)
    G["ple_norm_g"] = dpg
    G["w_ple_gate"] = matmul(n("mm_pg_w"), S["h2"], dgl, "tn")
    G["w_ple"] = matmul(n("mm_ple_w"), p_l, dpe, "tn")
    dh2 = matmul(n("mm_pg_x"), dgl, W["w_ple_gate"], "nt", add=dh2a)
    (dha, ddn), (G["ln2_g"], G["ln2_b"]) = rowvjp(
        n("ln2_b"), resln_fn, [(S["h"], 1024, 0), (S["dn"], 1024, 0)], [_rows(W["ln2_g"]), _rows(W["ln2_b"])],
        [(dh2, 1024, 0)], [([0], F32, None), ([1], BF16, None)], T)
    G["w_down"] = matmul(n("mm_down_w"), S["act"], ddn, "tn")
    dact = matmul(n("mm_down_x"), ddn, W["w_down"], "nt")
    (dgu,), _ = rowvjp(n("swiglu_b"), swiglu_fn, [(S["gu"], FFN_DIM, 0), (S["gu"], FFN_DIM, 1)], [],
                       [(dact, FFN_DIM, 0)], [([0, 1], BF16, None)], T)
    G["w_gate_up"] = matmul(n("mm_gu_w"), dgu, S["h"], "tn")
    dh = matmul(n("mm_gu_x"), dgu, W["w_gate_up"], "nn", add=dha)
    (dxa, dmix), (G["ln1_g"], G["ln1_b"]) = rowvjp(
        n("ln1_b"), resln_fn, [(x, 1024, 0), (S["mix"], 1024, 0)], [_rows(W["ln1_g"]), _rows(W["ln1_b"])],
        [(dh, 1024, 0)], [([0], F32, None), ([1], BF16, None)], T)
    G["w_o"] = matmul(n("mm_o_w"), S["merged"], dmix, "tn")
    dmerged = matmul(n("mm_o_x"), dmix, W["w_o"], "nt")
    dproj = jax.ShapeDtypeStruct((T, N_IN_PAD), BF16)
    (dproj, dy_a, dy_bo), _ = rowvjp(
        n("merge_b"), merge_fn, [(proj, 1024, 2), (proj, 1024, 3), (S["y_a"], 1024, 0), (S["y_bo"], 1024, 0)], [],
        [(dmerged, 1024, 0)], [([0, 1], BF16, (dproj, 1)), ([2], BF16, None), ([3], BF16, None)], T)
    G["w_a_out"] = matmul(n("mm_aout_w"), S["u3"], dy_a, "tn")
    du3 = matmul(n("mm_aout_x"), dy_a, W["w_a_out"], "nt")
    (du1,), (G["ln_a_g"], G["ln_a_b"]) = rowvjp(
        n("lnsilu_b"), lnsilu_fn, [(S["u1"], 1024, 0)], [_rows(W["ln_a_g"]), _rows(W["ln_a_b"])], [(du3, 1024, 0)],
        [([0], F32, None)], T)
    du0, G["conv_a_w"], G["conv_a_b"] = conv_bwd(n("conv_a_b"), du1, S["u0"], 0, W["conv_a_w"], T)
    (dproj,), _ = rowvjp(n("glu_b"), glu_fn, [(proj, 1024, 0), (proj, 1024, 1)], [], [(du0, 1024, 0)],
                         [([0, 1], BF16, (dproj, 0))], T)
    G["w_b_out"] = matmul(n("mm_bout_w"), S["yn"], dy_bo, "tn")
    dyn = matmul(n("mm_bout_x"), dy_bo, W["w_b_out"], "nt")
    (dys, dxs, dproj), (ddsk, G["ssm_norm_g"]) = rowvjp(
        n("gnorm_b"), gnorm_fn, [(S["y_f"], 256, 0), (S["y_b"], 256, 0), (S["xbc"], 256, 0), (proj, 256, 16)],
        [S["dsk"], _rows(W["ssm_norm_g"])], [(dyn, 256, 0)],
        [([0], F32, None), ([2], F32, None), ([3], BF16, (dproj, 16))], T, groups=N_GROUPS)
    G["d_skip"] = ddsk.reshape(N_HEADS, HEAD_DIM).sum(axis=1)
    dx1, db1, dc1, ddt1, dbias_f, dalog_f = ssd_bwd(
        n("ssd_f_b"), S["xbc"], proj, S["bias_row"], S["alog_row"], S["hs_f"], dys, (dxs,), 0, T)
    dxx, dbb, dcc, ddt, dbias_r, dalog_r = ssd_bwd(
        n("ssd_r_b"), S["xbc"], proj, S["bias_row"], S["alog_row"], S["hs_b"], dys, (dx1, db1, dc1, ddt1), 1, T)
    G["dt_bias"] = (dbias_f + dbias_r)[0, :2 * N_HEADS].reshape(2, N_HEADS)
    G["a_log"] = (dalog_f + dalog_r)[0, :2 * N_HEADS].reshape(2, N_HEADS)
    cw = W["ssm_conv_w"]
    b0 = 6144 // CONV_CB
    dproj, dwx, dbx = conv_bwd(n("conv_sx_b"), dxx, proj, b0, cw[:, :D_INNER], T, into=(dproj, b0))
    dproj, dwb, dbb_ = conv_bwd(n("conv_sb_b"), dbb, proj, b0 + 4, cw[:, D_INNER:D_INNER + 1024], T, into=(dproj, b0 + 4))
    dproj, dwc, dbc = conv_bwd(n("conv_sc_b"), dcc, proj, b0 + 6, cw[:, D_INNER + 1024:], T, into=(dproj, b0 + 6))
    G["ssm_conv_w"] = jnp.concatenate([dwx, dwb, dwc], axis=1)
    G["ssm_conv_b"] = jnp.concatenate([dbx, dbb_, dbc], axis=1)
    (dproj,) = rowcall(n("dt_cast"), ident_fn, [(ddt, LANE, 0)], [], [(LANE, BF16, (dproj, (N_IN_PAD - LANE) // LANE))], T)
    G["w_in"] = matmul(n("mm_in_w"), dproj, x, "tn")[:N_IN]
    dx = matmul(n("mm_in_x"), dproj, W["w_in"], "nn", add=dxa)
    return dx, G


def local_step(x, p, loss_target, FW, T):
    Ws, saves = [], []
    cur = x
    for li in range(DEPTH):
        W = {k: v[li] for k, v in FW.items()}
        Ws.append(W)
        cur, S = layer_fwd(li, cur, p[li], W, T)
        saves.append(S)
    dcur, sq = loss_head(cur, loss_target, T)
    loss = 0.5 * jnp.sum(sq) / D_MODEL
    grads = [None] * DEPTH
    for li in reversed(range(DEPTH)):
        dcur, grads[li] = layer_bwd(li, dcur, p[li], Ws[li], saves[li], T)
    return loss, dcur, grads


def kernel(x, p, w_in, conv_a_w, conv_a_b, ln_a_g, ln_a_b, w_a_out, ssm_conv_w, ssm_conv_b, a_log, dt_bias, d_skip, ssm_norm_g, w_b_out, w_o, ln1_g, ln1_b, w_gate_up, w_down, ln2_g, ln2_b, w_ple, ple_norm_g, w_ple_gate, loss_target, m_w_in, m_conv_a_w, m_conv_a_b, m_ln_a_g, m_ln_a_b, m_w_a_out, m_ssm_conv_w, m_ssm_conv_b, m_a_log, m_dt_bias, m_d_skip, m_ssm_norm_g, m_w_b_out, m_w_o, m_ln1_g, m_ln1_b, m_w_gate_up, m_w_down, m_ln2_g, m_ln2_b, m_w_ple, m_ple_norm_g, m_w_ple_gate, v_w_in, v_conv_a_w, v_conv_a_b, v_ln_a_g, v_ln_a_b, v_w_a_out, v_ssm_conv_w, v_ssm_conv_b, v_a_log, v_dt_bias, v_d_skip, v_ssm_norm_g, v_w_b_out, v_w_o, v_ln1_g, v_ln1_b, v_w_gate_up, v_w_down, v_ln2_g, v_ln2_b, v_w_ple, v_ple_norm_g, v_w_ple_gate):
    A = dict(locals())
    w = {k: A[k] for k in WEIGHTS}
    m = {k: A["m_" + k] for k in WEIGHTS}
    v = {k: A["v_" + k] for k in WEIGHTS}
    T = x.shape[1]
    big_shapes = [w[k].shape for k in BIG]
    small_shapes = [w[k].shape for k in SMALL]

    mm_names = [k for k in BIG if k not in CONV_W]
    shards = [_shard_for_gather(k, w[k].astype(BF16)) for k in mm_names]
    gathered = all_gather("gather_weights", _pack(shards, 16))
    FW = {k: _full_from_gathered(k, g) for k, g in zip(mm_names, _unpack(gathered, [t.shape for t in shards], (N_DEV,)))}
    FW["w_in"] = jnp.pad(FW["w_in"], ((0, 0), (0, N_IN_PAD - N_IN), (0, 0)))
    gathered = all_gather("gather_conv_weights", _pack([w[k] for k in CONV_W], SUBLANE))
    FW.update({k: _full_from_gathered(k, g)
               for k, g in zip(CONV_W, _unpack(gathered, [w[k].shape for k in CONV_W], (N_DEV,)))})
    FW.update({k: w[k] for k in SMALL})

    loss, grad_x, gfull = local_step(x[0], p[:, 0], loss_target[0], FW, T)
    loss = lax.psum(loss, ("x", "y", "c"))

    TR = 512
    flat = jnp.concatenate([_pieces_from_full(k, gfull[li][k]).astype(BF16) for k in BIG for li in range(DEPTH)], axis=1)
    rows = -(-flat.shape[1] // (LANE * TR)) * TR
    gpack = jnp.pad(flat, ((0, 0), (0, rows * LANE - flat.shape[1]))).reshape(N_DEV, rows, LANE)
    landed = grad_exchange("grad_exchange", gpack)
    gsum = sum_slots("sum_grad_pieces", landed, TR)
    gshard = [jnp.swapaxes(g, 1, 2) if k in COL_T else g
              for k, g in zip(BIG, _unpack(gsum, [_piece_shape(k, w[k].shape) for k in BIG]))]
    res_big = adamw("adamw_big", _pack(gshard, TR)[None], _pack([w[k] for k in BIG], TR), _pack([m[k] for k in BIG], TR),
                    _pack([v[k] for k in BIG], TR), TR)
    res_big = [dict(zip(BIG, _unpack(r, big_shapes))) for r in res_big]

    spack = _pack([jnp.stack([gfull[li][k] for li in range(DEPTH)]).reshape(w[k].shape) for k in SMALL], SUBLANE)
    sall = all_gather("gather_small_grads", spack)
    rs = spack.shape[0]
    res_small = adamw("adamw_small", sall, _pack([w[k] for k in SMALL], SUBLANE), _pack([m[k] for k in SMALL], SUBLANE),
                      _pack([v[k] for k in SMALL], SUBLANE), rs)
    res_small = [dict(zip(SMALL, _unpack(r, small_shapes))) for r in res_small]

    outs = [loss, grad_x[None]]
    for q in range(4):
        for k in WEIGHTS:
            outs.append(res_big[q][k] if k in res_big[q] else res_small[q][k])
    return tuple(outs)
```

```python
import math

import jax
import jax.numpy as jnp
from jax import lax
from jax.experimental import pallas as pl
from jax.experimental.pallas import tpu as pltpu

F32 = jnp.float32
BF16 = jnp.bfloat16

D_MODEL = 1024
CONV_DIM = 1024
CONV_KERNEL = 31
D_INNER = 2048
HEAD_DIM = 64
N_HEADS = 32
N_GROUPS = 8
D_STATE = 128
SSM_CONV = 5
CHUNK = 128
XBC_DIM = D_INNER + 2 * N_GROUPS * D_STATE
FFN_DIM = 2816
PLE_DIM = 256
N_IN = 2 * CONV_DIM + 2 * D_MODEL + D_INNER + XBC_DIM + 2 * N_HEADS
N_IN_PAD = 10368
DEPTH = 2
N_DEV = 8
ALPHA = (2 * DEPTH) ** 0.25
LN_EPS = 1e-5
RMS_EPS = 1e-6
ADAM_LR, ADAM_B1, ADAM_B2, ADAM_EPS, ADAM_WD, ADAM_STEP = 0.001, 0.9, 0.999, 1e-08, 0.01, 10

LANE = 128
SUBLANE = 8
HALO = 16
VMEM_LIMIT = 52 * 1024 * 1024
NEG = -1e30

BIG = ["w_in", "conv_a_w", "w_a_out", "ssm_conv_w", "w_b_out", "w_o", "w_gate_up", "w_down", "w_ple", "w_ple_gate"]
BIG_AXIS = {"w_in": 2, "conv_a_w": 2, "w_a_out": 1, "ssm_conv_w": 2, "w_b_out": 1, "w_o": 1, "w_gate_up": 2,
            "w_down": 1, "w_ple": 2, "w_ple_gate": 1}
COL_T = ["w_in", "w_gate_up"]
CONV_W = ["conv_a_w", "ssm_conv_w"]
SMALL = ["conv_a_b", "ln_a_g", "ln_a_b", "ssm_conv_b", "a_log", "dt_bias", "d_skip", "ssm_norm_g", "ln1_g", "ln1_b",
         "ln2_g", "ln2_b", "ple_norm_g"]
WEIGHTS = ["w_in", "conv_a_w", "conv_a_b", "ln_a_g", "ln_a_b", "w_a_out", "ssm_conv_w", "ssm_conv_b", "a_log", "dt_bias",
           "d_skip", "ssm_norm_g", "w_b_out", "w_o", "ln1_g", "ln1_b", "w_gate_up", "w_down", "ln2_g", "ln2_b", "w_ple",
           "ple_norm_g", "w_ple_gate"]


def _cparams(sem):
    return pltpu.CompilerParams(dimension_semantics=sem, vmem_limit_bytes=VMEM_LIMIT)


def _pick(n, cap):
    if n <= cap:
        return n
    best = None
    for d in range(LANE, cap + 1, LANE):
        if n % d == 0:
            best = d
    assert best is not None, (n, cap)
    return best


def matmul(name, a, b, mode, out_dtype=F32, add=None):
    if mode == "nn":
        (M, K), (K2, N) = a.shape, b.shape
    elif mode == "nt":
        (M, K), (N, K2) = a.shape, b.shape
    else:
        (K, M), (K2, N) = a.shape, b.shape
    assert K == K2, (name, a.shape, b.shape)
    tm = _pick(M, 1024) if mode != "tn" else _pick(M, 1408)
    tn = _pick(N, 1408)
    tk = _pick(K, 512) if mode == "tn" else (K if (mode == "nn" and K <= 2816) else _pick(K, 1408))
    nk = K // tk
    grid = (M // tm, N // tn, nk)
    if mode == "tn":
        a_spec = pl.BlockSpec((tk, tm), lambda i, j, k: (k, i))
    else:
        a_spec = pl.BlockSpec((tm, tk), lambda i, j, k: (i, k))
    if mode == "nt":
        b_spec = pl.BlockSpec((tn, tk), lambda i, j, k: (j, k))
    else:
        b_spec = pl.BlockSpec((tk, tn), lambda i, j, k: (k, j))
    o_spec = pl.BlockSpec((tm, tn), lambda i, j, k: (i, j))
    dims = {"nn": ((1,), (0,)), "nt": ((1,), (1,)), "tn": ((0,), (0,))}[mode]
    has_add = add is not None

    def body(a_ref, b_ref, *rest):
        if has_add:
            add_ref, o_ref, *scr = rest
        else:
            o_ref, *scr = rest
        part = lax.dot_general(a_ref[...].astype(BF16), b_ref[...].astype(BF16), (dims, ((), ())),
                               preferred_element_type=F32)

        def finish(v):
            if has_add:
                v = v + add_ref[...].astype(F32)
            o_ref[...] = v.astype(o_ref.dtype)

        if nk == 1:
            finish(part)
        else:
            acc = scr[0]
            k = pl.program_id(2)

            @pl.when(k == 0)
            def _():
                acc[...] = part

            @pl.when(k > 0)
            def _():
                acc[...] += part

            @pl.when(k == nk - 1)
            def _():
                finish(acc[...])

    in_specs = [a_spec, b_spec] + ([o_spec] if has_add else [])
    args = (a, b) + ((add,) if has_add else ())
    return pl.pallas_call(
        body, out_shape=jax.ShapeDtypeStruct((M, N), out_dtype), grid=grid, in_specs=in_specs, out_specs=o_spec,
        scratch_shapes=[pltpu.VMEM((tm, tn), F32)] if nk > 1 else [], name=name,
        compiler_params=_cparams(("parallel", "parallel", "arbitrary")))(*args)


def _row_specs(items, tT, groups):
    specs = []
    for (_, w, blk) in items:
        assert blk % groups == 0
        specs.append(pl.BlockSpec((tT, w * groups), (lambda i, b=blk // groups: (i, b))))
    return specs


def _slices(v, groups):
    if groups == 1:
        return [v]
    w = v.shape[1] // groups
    return [v[:, w * s:w * (s + 1)] for s in range(groups)]


def _cat(vs):
    return vs[0] if len(vs) == 1 else jnp.concatenate(vs, axis=1)


def rowcall(name, fn, ins, pars, outs, T, tT=256, groups=1):
    n_in, n_par = len(ins), len(pars)
    intos = [o[2] for o in outs if o[2] is not None]
    in_specs = (_row_specs(ins, tT, groups) + [pl.BlockSpec(p.shape, lambda i: (0, 0)) for p in pars]
                + [pl.BlockSpec(memory_space=pl.ANY)] * len(intos))
    out_specs, out_shapes, aliases = [], [], {}
    n_alias = 0
    for oi, (w, dt, into) in enumerate(outs):
        if into is None:
            out_specs.append(pl.BlockSpec((tT, w * groups), lambda i: (i, 0)))
            out_shapes.append(jax.ShapeDtypeStruct((T, w * groups), dt))
        else:
            arr, blk = into
            out_specs.append(pl.BlockSpec((tT, w * groups), lambda i, b=blk // groups: (i, b)))
            out_shapes.append(jax.ShapeDtypeStruct(arr.shape, arr.dtype))
            aliases[n_in + n_par + n_alias] = oi
            n_alias += 1

    def body(*refs):
        xs = [_slices(r[...].astype(F32), groups) for r in refs[:n_in]]
        ps = [_slices(r[...], groups) for r in refs[n_in:n_in + n_par]]
        o_refs = refs[n_in + n_par + n_alias:]
        res = [fn(*[x[s] for x in xs], *[p[s] for p in ps]) for s in range(groups)]
        for k, r in enumerate(o_refs):
            r[...] = _cat([res[s][k] for s in range(groups)]).astype(r.dtype)

    res = pl.pallas_call(
        body, out_shape=out_shapes, grid=(T // tT,), in_specs=in_specs, out_specs=out_specs,
        input_output_aliases=aliases, name=name, compiler_params=_cparams(("parallel",)))(
            *[a for (a, _, _) in ins], *pars, *[a for (a, _) in intos])
    return list(res)


def rowvjp(name, fn, ins, pars, cts, douts, T, tT=256, groups=1):
    n_in, n_par, n_ct = len(ins), len(pars), len(cts)
    intos = [o[2] for o in douts if o[2] is not None and not isinstance(o[2][0], jax.ShapeDtypeStruct)]
    in_specs = (_row_specs(ins, tT, groups) + [pl.BlockSpec(p.shape, lambda i: (0, 0)) for p in pars]
                + _row_specs(cts, tT, groups) + [pl.BlockSpec(memory_space=pl.ANY)] * len(intos))
    out_specs, out_shapes, aliases = [], [], {}
    n_alias = 0
    for oi, (idxs, dt, into) in enumerate(douts):
        w = sum(ins[k][1] for k in idxs) * groups
        if into is None:
            out_specs.append(pl.BlockSpec((tT, w), lambda i: (i, 0)))
            out_shapes.append(jax.ShapeDtypeStruct((T, w), dt))
        else:
            assert len(idxs) == 1 or groups == 1
            arr, blk = into
            out_specs.append(pl.BlockSpec((tT, w), lambda i, b=blk // groups: (i, b)))
            out_shapes.append(jax.ShapeDtypeStruct(arr.shape, arr.dtype))
            if not isinstance(arr, jax.ShapeDtypeStruct):
                aliases[n_in + n_par + n_ct + n_alias] = oi
                n_alias += 1
    n_dout = len(douts)
    for p in pars:
        out_specs.append(pl.BlockSpec(p.shape, lambda i: (0, 0)))
        out_shapes.append(jax.ShapeDtypeStruct(p.shape, F32))

    def body(*refs):
        xs = [_slices(r[...].astype(F32), groups) for r in refs[:n_in]]
        ps = [_slices(r[...], groups) for r in refs[n_in:n_in + n_par]]
        cs = [_slices(r[...].astype(F32), groups) for r in refs[n_in + n_par:n_in + n_par + n_ct]]
        o_refs = refs[n_in + n_par + n_ct + n_alias:]
        grads = []
        for s in range(groups):
            _, vjp_fn = jax.vjp(fn, *[x[s] for x in xs], *[p[s] for p in ps])
            grads.append(vjp_fn(tuple(c[s] for c in cs)))
        for r, (idxs, _, _) in zip(o_refs[:n_dout], douts):
            r[...] = _cat([grads[s][k] for k in idxs for s in range(groups)]).astype(r.dtype)
        for k, r in enumerate(o_refs[n_dout:]):
            @pl.when(pl.program_id(0) == 0)
            def _(r=r):
                r[...] = jnp.zeros(r.shape, F32)
            r[...] += _cat([grads[s][n_in + k] for s in range(groups)])

    res = pl.pallas_call(
        body, out_shape=out_shapes, grid=(T // tT,), in_specs=in_specs, out_specs=out_specs,
        input_output_aliases=aliases, name=name, compiler_params=_cparams(("arbitrary",)))(
            *[a for (a, _, _) in ins], *pars, *[a for (a, _, _) in cts], *[a for (a, _) in intos])
    res = list(res)
    return res[:n_dout], res[n_dout:]


def _sigmoid(x):
    return 1.0 / (1.0 + jnp.exp(-x))


def _silu(x):
    return x * _sigmoid(x)


def _softplus(x):
    return jnp.maximum(x, 0.0) + jnp.log(1.0 + jnp.exp(-jnp.abs(x)))


def _ln(x, g, b):
    mu = jnp.mean(x, axis=-1, keepdims=True)
    xc = x - mu
    var = jnp.mean(xc * xc, axis=-1, keepdims=True)
    return xc * lax.rsqrt(var + LN_EPS) * g + b


def glu_fn(a, gt):
    return (a * _sigmoid(gt),)


def lnsilu_fn(u, g, b):
    return (_silu(_ln(u, g, b)),)


def gnorm_fn(yf, yb, xp, z, dsk, ng):
    y = (yf + yb + _silu(xp) * dsk) * _silu(z)
    return (y * lax.rsqrt(jnp.mean(y * y, axis=-1, keepdims=True) + RMS_EPS) * ng,)


def merge_fn(ga, gb, ya, yb):
    return (_sigmoid(ga) * ya + _sigmoid(gb) * yb,)


def resln_fn(x, r, g, b):
    return (_ln(ALPHA * x + r, g, b),)


def swiglu_fn(g, u):
    return (_silu(g) * u,)


def ple_fn(h2, pe, gl, g):
    e = pe * lax.rsqrt(jnp.mean(pe * pe, axis=-1, keepdims=True) + RMS_EPS) * g
    return (h2 + e * _sigmoid(gl),)


def ident_fn(v):
    return (v,)


CONV_CB = 512
CONV_TT = 512
CONV_TILES = 4
CONV_RB = CONV_TILES * SUBLANE
CONV_RED_TILES = 2


def _conv_specs(blk0, T, tT, cb):
    nh = tT // HALO
    cur = pl.BlockSpec((tT, cb), lambda j, i: (i, blk0 + j))
    prev = pl.BlockSpec((HALO, cb), lambda j, i: (jnp.maximum(i * nh - 1, 0), blk0 + j))
    nxt = pl.BlockSpec((HALO, cb), lambda j, i: (jnp.minimum((i + 1) * nh, T // HALO - 1), blk0 + j))
    return [prev, cur, nxt]


def _phases(offsets):
    return sorted({off % SUBLANE for off in offsets})


def _fill_shifted(sh_ref, pad_ref, prev_ref, cur_ref, next_ref, i, n_t, tT, phases):
    pad_ref[pl.ds(0, HALO), :] = prev_ref[...].astype(F32) * (i > 0).astype(F32)
    pad_ref[pl.ds(HALO, tT), :] = cur_ref[...].astype(F32)
    pad_ref[pl.ds(HALO + tT, HALO), :] = next_ref[...].astype(F32) * (i < n_t - 1).astype(F32)
    for ph in phases:
        sh_ref[ph] = pad_ref[pl.ds(ph, tT + 3 * SUBLANE), :]


def _taps(sh_ref, base, ls, offsets, n_tiles):
    out = []
    for off in offsets:
        q, ph = divmod(off, SUBLANE)
        out.append(tuple(sh_ref[ph, pl.ds(base + SUBLANE * (q + t), SUBLANE), ls] for t in range(n_tiles)))
    return out


def _conv_rows(sh_ref, w_ref, bias, o_ref, offsets, tT, cb):
    K = len(offsets)
    for lt in range(cb // LANE):
        ls = slice(LANE * lt, LANE * (lt + 1))
        wv = [jnp.broadcast_to(w_ref[k:k + 1, ls], (SUBLANE, LANE)) for k in range(K)]
        b0 = jnp.zeros((SUBLANE, LANE), F32) if bias is None else jnp.broadcast_to(bias[:, ls], (SUBLANE, LANE))

        def rows(r, carry, ls=ls, wv=wv, b0=b0):
            base = pl.multiple_of(r * CONV_RB, CONV_RB)
            accs = [b0] * CONV_TILES
            for k, ds in enumerate(_taps(sh_ref, base, ls, offsets, CONV_TILES)):
                accs = [a + d * wv[k] for a, d in zip(accs, ds)]
            o_ref[pl.ds(base, CONV_RB), ls] = jnp.concatenate(accs, axis=0).astype(o_ref.dtype)
            return carry

        lax.fori_loop(0, tT // CONV_RB, rows, 0)


def conv_fwd(name, u, blk0, w, b, T):
    K, C = w.shape
    P = (K - 1) // 2
    tT, cb = min(CONV_TT, T), CONV_CB
    n_t = T // tT

    def body(prev_ref, cur_ref, next_ref, w_ref, b_ref, o_ref, pad_ref, sh_ref):
        offsets = [HALO - P + k for k in range(K)]
        _fill_shifted(sh_ref, pad_ref, prev_ref, cur_ref, next_ref, pl.program_id(1), n_t, tT, _phases(offsets))
        _conv_rows(sh_ref, w_ref, b_ref[...], o_ref, offsets, tT, cb)

    return pl.pallas_call(
        body, out_shape=jax.ShapeDtypeStruct((T, C), F32), grid=(C // cb, n_t),
        in_specs=_conv_specs(blk0, T, tT, cb) + [pl.BlockSpec((K, cb), lambda j, i: (0, j)),
                                                  pl.BlockSpec((1, cb), lambda j, i: (0, j))],
        out_specs=pl.BlockSpec((tT, cb), lambda j, i: (i, j)),
        scratch_shapes=[pltpu.VMEM((tT + 2 * HALO, cb), F32), pltpu.VMEM((SUBLANE, tT + 3 * SUBLANE, cb), F32)],
        name=name, compiler_params=_cparams(("parallel", "arbitrary")))(u, u, u, w, b)


def conv_bwd(name, dy, u, blk0, w, T, into=None):
    K, C = w.shape
    P = (K - 1) // 2
    tT, cb = min(CONV_TT, T), CONV_CB
    n_t = T // tT

    def body(dprev, dcur, dnext, uprev, ucur, unext, w_ref, *rest):
        if into is not None:
            rest = rest[1:]
        du_ref, dw_ref, db_ref, pad_ref, shd_ref, shu_ref = rest
        i = pl.program_id(1)
        offsets = [HALO - P + k for k in range(K)]
        back = [HALO + P - k for k in range(K)]
        _fill_shifted(shd_ref, pad_ref, dprev, dcur, dnext, i, n_t, tT, _phases(back + [HALO]))
        _fill_shifted(shu_ref, pad_ref, uprev, ucur, unext, i, n_t, tT, _phases(offsets))
        _conv_rows(shd_ref, w_ref, None, du_ref, back, tT, cb)

        @pl.when(i == 0)
        def _():
            dw_ref[...] = jnp.zeros(dw_ref.shape, F32)
            db_ref[...] = jnp.zeros(db_ref.shape, F32)

        rb = CONV_RED_TILES * SUBLANE
        for lt in range(cb // LANE):
            ls = slice(LANE * lt, LANE * (lt + 1))

            def red(r, accs, ls=ls):
                base = pl.multiple_of(r * rb, rb)
                (d0, d1), = _taps(shd_ref, base, ls, [HALO], CONV_RED_TILES)
                new = [acc + d0 * u0 + d1 * u1
                       for acc, (u0, u1) in zip(accs[:K], _taps(shu_ref, base, ls, offsets, CONV_RED_TILES))]
                return tuple(new) + (accs[K] + d0 + d1,)

            zero = jnp.zeros((SUBLANE, LANE), F32)
            accs = lax.fori_loop(0, tT // rb, red, (zero,) * (K + 1))
            for k in range(K):
                dw_ref[k:k + 1, ls] += jnp.sum(accs[k], axis=0, keepdims=True)
            db_ref[:, ls] += jnp.sum(accs[K], axis=0, keepdims=True)

    dspecs = _conv_specs(0, T, tT, cb)
    uspecs = _conv_specs(blk0, T, tT, cb)
    in_specs = dspecs + uspecs + [pl.BlockSpec((K, cb), lambda j, i: (0, j))]
    args = [dy, dy, dy, u, u, u, w]
    aliases = {}
    if into is None:
        du_spec = pl.BlockSpec((tT, cb), lambda j, i: (i, j))
        du_shape = jax.ShapeDtypeStruct((T, C), F32)
    else:
        arr, oblk = into
        in_specs.append(pl.BlockSpec(memory_space=pl.ANY))
        args.append(arr)
        aliases = {7: 0}
        du_spec = pl.BlockSpec((tT, cb), lambda j, i: (i, oblk + j))
        du_shape = jax.ShapeDtypeStruct(arr.shape, arr.dtype)
    return pl.pallas_call(
        body, out_shape=[du_shape, jax.ShapeDtypeStruct((K, C), F32), jax.ShapeDtypeStruct((1, C), F32)],
        grid=(C // cb, n_t), in_specs=in_specs,
        out_specs=[du_spec, pl.BlockSpec((K, cb), lambda j, i: (0, j)), pl.BlockSpec((1, cb), lambda j, i: (0, j))],
        scratch_shapes=[pltpu.VMEM((tT + 2 * HALO, cb), F32), pltpu.VMEM((SUBLANE, tT + 3 * SUBLANE, cb), F32),
                        pltpu.VMEM((SUBLANE, tT + 3 * SUBLANE, cb), F32)],
        input_output_aliases=aliases, name=name, compiler_params=_cparams(("arbitrary", "arbitrary")))(*args)


def _dot(a, b, dims):
    return lax.dot_general(a.astype(BF16), b.astype(BF16), (dims, ((), ())), preferred_element_type=F32)


def _dnn(a, b):
    return _dot(a, b, ((1,), (0,)))


def _dnt(a, b):
    return _dot(a, b, ((1,), (1,)))


def _dtn(a, b):
    return _dot(a.T, b, ((1,), (0,)))


@jax.custom_vjp
def _nn(a, b):
    return _dnn(a, b)


_nn.defvjp(lambda a, b: (_dnn(a, b), (a, b)), lambda r, g: (_dnt(g, r[1]), _dtn(r[0], g)))


@jax.custom_vjp
def _nt(a, b):
    return _dnt(a, b)


_nt.defvjp(lambda a, b: (_dnt(a, b), (a, b)), lambda r, g: (_dnn(g, r[1]), _dtn(g, r[0])))


@jax.custom_vjp
def _tn(a, b):
    return _dtn(a, b)


_tn.defvjp(lambda a, b: (_dtn(a, b), (a, b)), lambda r, g: (_dnt(r[1], g), _dnn(r[0], g)))


def _split_dot(m, v):
    hi = v.astype(BF16)
    r1 = v - hi.astype(F32)
    mid = r1.astype(BF16)
    lo = (r1 - mid.astype(F32)).astype(BF16)
    mb = m.astype(BF16)
    d = lambda x: lax.dot_general(mb, x, (((1,), (0,)), ((), ())), preferred_element_type=F32)
    return d(hi) + d(mid) + d(lo)


@jax.custom_vjp
def _tri_dot(tri, tri_t, v):
    return _split_dot(tri, v)


_tri_dot.defvjp(lambda tri, tri_t, v: (_split_dot(tri, v), (tri, tri_t)),
                lambda r, g: (jnp.zeros_like(r[0]), jnp.zeros_like(r[1]), _split_dot(r[1], g)))


def _pick_vjp(axis):
    def pick(v, h):
        return v[:, h:h + 1] if axis == 1 else v[h:h + 1, :]

    def fwd(v, h):
        return pick(v, h), v.shape

    def bwd(h, shape, g):
        idx = lax.broadcasted_iota(jnp.int32, shape, axis)
        return (jnp.where(idx == h, g, 0.0),)

    f = jax.custom_vjp(pick, nondiff_argnums=(1,))
    f.defvjp(fwd, bwd)
    return f


_lane_pick = _pick_vjp(1)
_sub_pick = _pick_vjp(0)


def _ssd_consts(dirn):
    ri = lax.broadcasted_iota(jnp.int32, (CHUNK, CHUNK), 0)
    ci = lax.broadcasted_iota(jnp.int32, (CHUNK, CHUNK), 1)
    keep = (ci <= ri) if dirn == 0 else (ci >= ri)
    tri = keep.astype(F32)
    tri_t = (~keep | (ci == ri)).astype(F32)
    lane = lax.broadcasted_iota(jnp.int32, (1, LANE), 1)
    sub = lax.broadcasted_iota(jnp.int32, (CHUNK, 1), 0)
    end = (sub == (CHUNK - 1 if dirn == 0 else 0)).astype(F32)
    lo_half = lane < HEAD_DIM
    return keep, tri, tri_t, end, lo_half, N_HEADS * dirn


def _ssd_chunk(consts, x_t, b_t, c_t, dtr, bias, alog, h_t):
    keep, tri, tri_t, end, lo_half, h_base = consts
    dt = _softplus(dtr + bias)
    a = dt * (-jnp.exp(alog))
    cs = _tri_dot(tri, tri_t, a)
    cs_t = cs.T
    tot = jnp.sum(cs * end, axis=0, keepdims=True)
    ys, hn = [], []
    for g in range(N_GROUPS):
        bm, cm = _silu(b_t[g]), _silu(c_t[g])
        gm = _nt(cm, bm)
        for jj in range(2):
            j = 2 * g + jj
            hh = (h_base + 2 * j, h_base + 2 * j + 1)
            col = [_lane_pick(cs, h) for h in hh]
            row = [_sub_pick(cs_t, h) for h in hh]
            dth = [_lane_pick(dt, h) for h in hh]
            toth = [_lane_pick(tot, h) for h in hh]
            xd = _silu(x_t[j]) * jnp.where(lo_half, dth[0], dth[1])
            yd = [_nn(gm * jnp.exp(jnp.where(keep, col[k] - row[k], NEG)), xd) for k in range(2)]
            cp = jnp.where(lo_half, col[0], col[1])
            tp = jnp.where(lo_half, toth[0], toth[1])
            ys.append(jnp.where(lo_half, yd[0], yd[1]) + _nn(cm, h_t[j]) * jnp.exp(cp))
            hn.append(h_t[j] * jnp.exp(tp) + _tn(bm, xd * jnp.exp(tp - cp)))
    return ys, hn


N_PAIR = D_INNER // LANE


def _tiles(ref, n):
    return [ref[:, LANE * j:LANE * (j + 1)].astype(F32) for j in range(n)]


def _ssd_in_specs(cmap):
    return [pl.BlockSpec((CHUNK, D_INNER), lambda i: (cmap(i), 0)),
            pl.BlockSpec((CHUNK, N_GROUPS * D_STATE), lambda i: (cmap(i), 2)),
            pl.BlockSpec((CHUNK, N_GROUPS * D_STATE), lambda i: (cmap(i), 3)),
            pl.BlockSpec((CHUNK, LANE), lambda i: (cmap(i), 0)),
            pl.BlockSpec((1, LANE), lambda i: (0, 0)), pl.BlockSpec((1, LANE), lambda i: (0, 0))]


def ssd_fwd(name, xbc, pdt, bias_row, alog_row, dirn, T):
    nc = T // CHUNK
    cmap = (lambda i: i) if dirn == 0 else (lambda i: nc - 1 - i)

    def body(x_ref, b_ref, c_ref, dt_ref, bias_ref, alog_ref, y_ref, hs_ref, h_scr):
        @pl.when(pl.program_id(0) == 0)
        def _():
            h_scr[...] = jnp.zeros(h_scr.shape, F32)

        hs_ref[0] = h_scr[...]
        ys, hn = _ssd_chunk(_ssd_consts(dirn), _tiles(x_ref, N_PAIR), _tiles(b_ref, N_GROUPS), _tiles(c_ref, N_GROUPS),
                            dt_ref[...], bias_ref[...], alog_ref[...], _tiles(h_scr, N_PAIR))
        for j in range(N_PAIR):
            y_ref[:, LANE * j:LANE * (j + 1)] = ys[j]
            h_scr[:, LANE * j:LANE * (j + 1)] = hn[j]

    return pl.pallas_call(
        body, out_shape=[jax.ShapeDtypeStruct((T, D_INNER), F32), jax.ShapeDtypeStruct((nc, D_STATE, D_INNER), F32)],
        grid=(nc,), in_specs=_ssd_in_specs(cmap),
        out_specs=[pl.BlockSpec((CHUNK, D_INNER), lambda i: (cmap(i), 0)),
                   pl.BlockSpec((1, D_STATE, D_INNER), lambda i: (cmap(i), 0, 0))],
        scratch_shapes=[pltpu.VMEM((D_STATE, D_INNER), F32)], name=name,
        compiler_params=_cparams(("arbitrary",)))(xbc, xbc, xbc, pdt, bias_row, alog_row)


def ssd_bwd(name, xbc, pdt, bias_row, alog_row, hs, dy, adds, dirn, T):
    nc = T // CHUNK
    cmap = (lambda i: nc - 1 - i) if dirn == 0 else (lambda i: i)
    GS = N_GROUPS * D_STATE

    n_add = len(adds)

    def body(x_ref, b_ref, c_ref, dt_ref, bias_ref, alog_ref, hs_ref, dy_ref, *rest):
        add_refs, (dx_ref, db_ref, dc_ref, ddt_ref, dbias_ref, dalog_ref, dh_scr) = rest[:n_add], rest[n_add:]
        ax_ref = add_refs[0]
        ab_ref, ac_ref, adt_ref = add_refs[1:] if n_add == 4 else (None, None, None)
        first = pl.program_id(0) == 0

        @pl.when(first)
        def _():
            dh_scr[...] = jnp.zeros(dh_scr.shape, F32)
            dbias_ref[...] = jnp.zeros(dbias_ref.shape, F32)
            dalog_ref[...] = jnp.zeros(dalog_ref.shape, F32)

        consts = _ssd_consts(dirn)
        fn = lambda *a: _ssd_chunk(consts, *a)
        _, vjp_fn = jax.vjp(fn, _tiles(x_ref, N_PAIR), _tiles(b_ref, N_GROUPS), _tiles(c_ref, N_GROUPS), dt_ref[...],
                            bias_ref[...], alog_ref[...], [hs_ref[0, :, LANE * j:LANE * (j + 1)] for j in range(N_PAIR)])
        dx, db, dc, ddt, dbias, dalog, dh = vjp_fn((_tiles(dy_ref, N_PAIR), _tiles(dh_scr, N_PAIR)))
        for j in range(N_PAIR):
            s = slice(LANE * j, LANE * (j + 1))
            dx_ref[:, s] = dx[j] + ax_ref[:, s]
            dh_scr[:, s] = dh[j]
        for g in range(N_GROUPS):
            s = slice(LANE * g, LANE * (g + 1))
            db_ref[:, s] = db[g] + (ab_ref[:, s] if n_add == 4 else 0.0)
            dc_ref[:, s] = dc[g] + (ac_ref[:, s] if n_add == 4 else 0.0)
        ddt_ref[...] = ddt + (adt_ref[...] if n_add == 4 else 0.0)
        dbias_ref[...] += dbias
        dalog_ref[...] += dalog

    blk = lambda w: pl.BlockSpec((CHUNK, w), lambda i: (cmap(i), 0))
    row = pl.BlockSpec((1, LANE), lambda i: (0, 0))
    return pl.pallas_call(
        body,
        out_shape=[jax.ShapeDtypeStruct((T, D_INNER), F32), jax.ShapeDtypeStruct((T, GS), F32),
                   jax.ShapeDtypeStruct((T, GS), F32), jax.ShapeDtypeStruct((T, LANE), F32),
                   jax.ShapeDtypeStruct((1, LANE), F32), jax.ShapeDtypeStruct((1, LANE), F32)],
        grid=(nc,),
        in_specs=_ssd_in_specs(cmap) + [pl.BlockSpec((1, D_STATE, D_INNER), lambda i: (cmap(i), 0, 0)), blk(D_INNER)]
        + [blk(D_INNER), blk(GS), blk(GS), blk(LANE)][:n_add],
        out_specs=[blk(D_INNER), blk(GS), blk(GS), blk(LANE), row, row],
        scratch_shapes=[pltpu.VMEM((D_STATE, D_INNER), F32)], name=name,
        compiler_params=_cparams(("arbitrary",)))(xbc, xbc, xbc, pdt, bias_row, alog_row, hs, dy, *adds)


def loss_head(y, target, T, tT=256):
    def body(y_ref, t_ref, dy_ref, sq_ref):
        @pl.when(pl.program_id(0) == 0)
        def _():
            sq_ref[...] = jnp.zeros(sq_ref.shape, F32)
        e = y_ref[...] - t_ref[...]
        dy_ref[...] = e * (1.0 / D_MODEL)
        sq_ref[...] += jnp.sum(e * e, axis=0, keepdims=True)

    spec = pl.BlockSpec((tT, D_MODEL), lambda i: (i, 0))
    return pl.pallas_call(
        body, out_shape=[jax.ShapeDtypeStruct((T, D_MODEL), F32), jax.ShapeDtypeStruct((1, D_MODEL), F32)],
        grid=(T // tT,), in_specs=[spec, spec], out_specs=[spec, pl.BlockSpec((1, D_MODEL), lambda i: (0, 0))],
        name="loss_head", compiler_params=_cparams(("arbitrary",)))(y, target)


MESH_ID = pl.DeviceIdType.MESH


def all_gather(name, v):
    R, W = v.shape

    def body(v_ref, out_ref, send_sems, recv_sems, local_sem):
        x, y, c = lax.axis_index("x"), lax.axis_index("y"), lax.axis_index("c")
        me, sibling = (x, y, c), (x, y, 1 - c)
        chips = [(1 - x, y), (x, 1 - y), (1 - x, 1 - y)]

        def slot(px, py, pc):
            return out_ref.at[4 * px + 2 * py + pc]

        def copy(k, block, to, src=None):
            return pltpu.make_async_remote_copy(
                src_ref=slot(*block) if src is None else src, dst_ref=slot(*block), send_sem=send_sems.at[k],
                recv_sem=recv_sems.at[k], device_id=to, device_id_type=MESH_ID)

        mine = pltpu.make_async_copy(v_ref, slot(*me), local_sem)
        mine.start()
        first = [copy(0, me, sibling, src=v_ref)]
        first += [copy(1 + j, me, (*chip, c), src=v_ref) for j, chip in enumerate(chips)]
        for cp in first:
            cp.start()
        passed = [copy(4 + j, (*chip, c), sibling) for j, chip in enumerate(chips)]
        for j, chip in enumerate(chips):
            copy(1 + j, (*chip, c), me).wait_recv()
            passed[j].start()
        copy(0, sibling, me).wait_recv()
        for j, chip in enumerate(chips):
            copy(4 + j, (*chip, 1 - c), me).wait_recv()
        for cp in first + passed:
            cp.wait_send()
        mine.wait()

    return pl.pallas_call(
        body, out_shape=jax.ShapeDtypeStruct((N_DEV, R, W), v.dtype),
        in_specs=[pl.BlockSpec(memory_space=pl.ANY)], out_specs=pl.BlockSpec(memory_space=pl.ANY),
        scratch_shapes=[pltpu.SemaphoreType.DMA((7,)), pltpu.SemaphoreType.DMA((7,)), pltpu.SemaphoreType.DMA],
        name=name, compiler_params=pltpu.CompilerParams(has_side_effects=True))(v)


def grad_exchange(name, g):
    _, R, W = g.shape

    def body(g_ref, out_ref, send_sems, recv_sems, local_sem):
        x, y, c = lax.axis_index("x"), lax.axis_index("y"), lax.axis_index("c")
        me = 4 * x + 2 * y + c
        mine = pltpu.make_async_copy(g_ref.at[me], out_ref.at[me], local_sem)
        mine.start()
        copies = []
        for k in range(1, N_DEV):
            kx, ky, kc = (k >> 2) & 1, (k >> 1) & 1, k & 1
            px = 1 - x if kx else x
            py = 1 - y if ky else y
            pc = 1 - c if kc else c
            copies.append(pltpu.make_async_remote_copy(
                src_ref=g_ref.at[4 * px + 2 * py + pc], dst_ref=out_ref.at[me], send_sem=send_sems.at[k - 1],
                recv_sem=recv_sems.at[k - 1], device_id=(px, py, pc), device_id_type=MESH_ID))
        for cp in copies:
            cp.start()
        for cp in copies:
            cp.wait_recv()
        for cp in copies:
            cp.wait_send()
        mine.wait()

    return pl.pallas_call(
        body, out_shape=jax.ShapeDtypeStruct(g.shape, g.dtype),
        in_specs=[pl.BlockSpec(memory_space=pl.ANY)], out_specs=pl.BlockSpec(memory_space=pl.ANY),
        scratch_shapes=[pltpu.SemaphoreType.DMA((7,)), pltpu.SemaphoreType.DMA((7,)), pltpu.SemaphoreType.DMA],
        name=name, compiler_params=pltpu.CompilerParams(has_side_effects=True))(g)


def sum_slots(name, parts, tr):
    n_slot, R, W = parts.shape

    def body(p_ref, o_ref):
        g = p_ref[0].astype(F32)
        for s in range(1, n_slot):
            g = g + p_ref[s].astype(F32)
        o_ref[...] = g

    return pl.pallas_call(
        body, out_shape=jax.ShapeDtypeStruct((R, W), F32), grid=(R // tr,),
        in_specs=[pl.BlockSpec((n_slot, tr, W), lambda i: (0, i, 0))], out_specs=pl.BlockSpec((tr, W), lambda i: (i, 0)),
        name=name, compiler_params=_cparams(("parallel",)))(parts)


def adamw(name, parts, w, m, v, tr):
    R, W = w.shape
    n_slot = parts.shape[0]
    c1 = 1.0 / (1.0 - ADAM_B1 ** ADAM_STEP)
    c2 = 1.0 / (1.0 - ADAM_B2 ** ADAM_STEP)

    def body(p_ref, w_ref, m_ref, v_ref, g_ref, d_ref, nm_ref, nv_ref):
        g = p_ref[0]
        for s in range(1, n_slot):
            g = g + p_ref[s]
        nm = ADAM_B1 * m_ref[...] + (1.0 - ADAM_B1) * g
        nv = ADAM_B2 * v_ref[...] + (1.0 - ADAM_B2) * (g * g)
        g_ref[...] = g
        nm_ref[...] = nm
        nv_ref[...] = nv
        d_ref[...] = -ADAM_LR * ((nm * c1) / (jnp.sqrt(nv * c2) + ADAM_EPS) + ADAM_WD * w_ref[...])

    spec = pl.BlockSpec((tr, W), lambda i: (i, 0))
    return pl.pallas_call(
        body, out_shape=[jax.ShapeDtypeStruct((R, W), F32)] * 4, grid=(R // tr,),
        in_specs=[pl.BlockSpec((n_slot, tr, W), lambda i: (0, i, 0)), spec, spec, spec], out_specs=[spec] * 4,
        name=name, compiler_params=_cparams(("parallel",)))(parts, w, m, v)


def _pack(arrs, row_mult):
    flat = jnp.concatenate([a.reshape(-1) for a in arrs])
    n = flat.shape[0]
    rows = -(-n // LANE)
    rows = -(-rows // row_mult) * row_mult
    return jnp.pad(flat, (0, rows * LANE - n)).reshape(rows, LANE)


def _unpack(buf, shapes, lead=()):
    flat = buf.reshape(lead + (-1,))
    out, off = [], 0
    for s in shapes:
        n = math.prod(s)
        out.append(flat[..., off:off + n].reshape(lead + tuple(s)))
        off += n
    return out


def _rows_sharded(name):
    return name in COL_T or BIG_AXIS[name] == 1


def _shard_for_gather(name, w):
    return jnp.swapaxes(w, 1, 2) if name in COL_T else w


def _full_from_gathered(name, g):
    if _rows_sharded(name):
        return jnp.transpose(g, (1, 0, 2, 3)).reshape(g.shape[1], N_DEV * g.shape[2], g.shape[3])
    return jnp.transpose(g, (1, 2, 0, 3)).reshape(g.shape[1], g.shape[2], N_DEV * g.shape[3])


def _pieces_from_full(name, f):
    A, B = f.shape
    if _rows_sharded(name):
        return f.reshape(N_DEV, -1)
    return jnp.transpose(f.reshape(A, N_DEV, B // N_DEV), (1, 0, 2)).reshape(N_DEV, -1)


def _piece_shape(name, shard_shape):
    L, a, b = shard_shape
    return (L, b, a) if name in COL_T else (L, a, b)


def _rows(v):
    return v.reshape(1, -1).astype(F32)


def _head_rows(W):
    bias = jnp.pad(W["dt_bias"].reshape(1, -1), ((0, 0), (0, LANE - 2 * N_HEADS)))
    alog = jnp.pad(W["a_log"].reshape(1, -1), ((0, 0), (0, LANE - 2 * N_HEADS)))
    return bias, alog


def layer_fwd(li, x, p_l, W, T):
    n = lambda s: f"l{li}_{s}"
    S = {"x": x}
    proj = matmul(n("mm_in"), x, W["w_in"], "nt", out_dtype=BF16)
    pdt = matmul(n("mm_dt"), x, W["w_in"][N_IN_PAD - LANE:], "nt")
    (u0,) = rowcall(n("glu"), glu_fn, [(proj, 1024, 0), (proj, 1024, 1)], [], [(1024, F32, None)], T)
    u1 = conv_fwd(n("conv_a"), u0, 0, W["conv_a_w"], _rows(W["conv_a_b"]), T)
    (u3,) = rowcall(n("lnsilu"), lnsilu_fn, [(u1, 1024, 0)], [_rows(W["ln_a_g"]), _rows(W["ln_a_b"])],
                    [(1024, BF16, None)], T)
    y_a = matmul(n("mm_aout"), u3, W["w_a_out"], "nn")
    xbc = conv_fwd(n("conv_s"), proj, 6144 // CONV_CB, W["ssm_conv_w"], _rows(W["ssm_conv_b"]), T)
    bias_row, alog_row = _head_rows(W)
    y_f, hs_f = ssd_fwd(n("ssd_f"), xbc, pdt, bias_row, alog_row, 0, T)
    y_b, hs_b = ssd_fwd(n("ssd_r"), xbc, pdt, bias_row, alog_row, 1, T)
    dsk = jnp.repeat(W["d_skip"], HEAD_DIM).reshape(1, D_INNER)
    (yn,) = rowcall(n("gnorm"), gnorm_fn, [(y_f, 256, 0), (y_b, 256, 0), (xbc, 256, 0), (proj, 256, 16)],
                    [dsk, _rows(W["ssm_norm_g"])], [(256, BF16, None)], T, groups=N_GROUPS)
    y_bo = matmul(n("mm_bout"), yn, W["w_b_out"], "nn")
    (merged,) = rowcall(n("merge"), merge_fn, [(proj, 1024, 2), (proj, 1024, 3), (y_a, 1024, 0), (y_bo, 1024, 0)], [],
                        [(1024, BF16, None)], T)
    mix = matmul(n("mm_o"), merged, W["w_o"], "nn")
    (h,) = rowcall(n("ln1"), resln_fn, [(x, 1024, 0), (mix, 1024, 0)], [_rows(W["ln1_g"]), _rows(W["ln1_b"])],
                   [(1024, F32, None)], T)
    gu = matmul(n("mm_gu"), h, W["w_gate_up"], "nt", out_dtype=BF16)
    (act,) = rowcall(n("swiglu"), swiglu_fn, [(gu, FFN_DIM, 0), (gu, FFN_DIM, 1)], [], [(FFN_DIM, BF16, None)], T)
    dn = matmul(n("mm_down"), act, W["w_down"], "nn")
    (h2,) = rowcall(n("ln2"), resln_fn, [(h, 1024, 0), (dn, 1024, 0)], [_rows(W["ln2_g"]), _rows(W["ln2_b"])],
                    [(1024, F32, None)], T)
    pe = matmul(n("mm_ple"), p_l, W["w_ple"], "nn")
    gl = matmul(n("mm_pg"), h2, W["w_ple_gate"], "nn")
    (xn,) = rowcall(n("pleout"), ple_fn, [(h2, 1024, 0), (pe, 1024, 0), (gl, 1024, 0)], [_rows(W["ple_norm_g"])],
                    [(1024, F32, None)], T)
    S.update(proj=proj, pdt=pdt, u0=u0, u1=u1, u3=u3, y_a=y_a, xbc=xbc, y_f=y_f, y_b=y_b, hs_f=hs_f, hs_b=hs_b, yn=yn, y_bo=y_bo,
             merged=merged, mix=mix, h=h, gu=gu, act=act, dn=dn, h2=h2, pe=pe, gl=gl, dsk=dsk, bias_row=bias_row,
             alog_row=alog_row)
    return xn, S


def layer_bwd(li, dxn, p_l, W, S, T):
    n = lambda s: f"l{li}_{s}"
    G = {}
    x, proj = S["x"], S["proj"]
    (dh2a, dpe, dgl), (dpg,) = rowvjp(
        n("pleout_b"), ple_fn, [(S["h2"], 1024, 0), (S["pe"], 1024, 0), (S["gl"], 1024, 0)], [_rows(W["ple_norm_g"])],
        [(dxn, 1024, 0)], [([0], F32, None), ([1], BF16, None), ([2], BF16, None)], T)
    G["ple_norm_g"] = dpg
    G["w_ple_gate"] = matmul(n("mm_pg_w"), S["h2"], dgl, "tn", out_dtype=BF16)
    G["w_ple"] = matmul(n("mm_ple_w"), p_l, dpe, "tn", out_dtype=BF16)
    dh2 = matmul(n("mm_pg_x"), dgl, W["w_ple_gate"], "nt", add=dh2a)
    (dha, ddn), (G["ln2_g"], G["ln2_b"]) = rowvjp(
        n("ln2_b"), resln_fn, [(S["h"], 1024, 0), (S["dn"], 1024, 0)], [_rows(W["ln2_g"]), _rows(W["ln2_b"])],
        [(dh2, 1024, 0)], [([0], F32, None), ([1], BF16, None)], T)
    G["w_down"] = matmul(n("mm_down_w"), S["act"], ddn, "tn", out_dtype=BF16)
    dact = matmul(n("mm_down_x"), ddn, W["w_down"], "nt", out_dtype=BF16)
    (dgu,), _ = rowvjp(n("swiglu_b"), swiglu_fn, [(S["gu"], FFN_DIM, 0), (S["gu"], FFN_DIM, 1)], [],
                       [(dact, FFN_DIM, 0)], [([0, 1], BF16, None)], T)
    G["w_gate_up"] = matmul(n("mm_gu_w"), dgu, S["h"], "tn", out_dtype=BF16)
    dh = matmul(n("mm_gu_x"), dgu, W["w_gate_up"], "nn", add=dha)
    (dxa, dmix), (G["ln1_g"], G["ln1_b"]) = rowvjp(
        n("ln1_b"), resln_fn, [(x, 1024, 0), (S["mix"], 1024, 0)], [_rows(W["ln1_g"]), _rows(W["ln1_b"])],
        [(dh, 1024, 0)], [([0], F32, None), ([1], BF16, None)], T)
    G["w_o"] = matmul(n("mm_o_w"), S["merged"], dmix, "tn", out_dtype=BF16)
    dmerged = matmul(n("mm_o_x"), dmix, W["w_o"], "nt")
    dproj = jax.ShapeDtypeStruct((T, N_IN_PAD), BF16)
    (dproj, dy_a, dy_bo), _ = rowvjp(
        n("merge_b"), merge_fn, [(proj, 1024, 2), (proj, 1024, 3), (S["y_a"], 1024, 0), (S["y_bo"], 1024, 0)], [],
        [(dmerged, 1024, 0)], [([0, 1], BF16, (dproj, 1)), ([2], BF16, None), ([3], BF16, None)], T)
    G["w_a_out"] = matmul(n("mm_aout_w"), S["u3"], dy_a, "tn", out_dtype=BF16)
    du3 = matmul(n("mm_aout_x"), dy_a, W["w_a_out"], "nt")
    (du1,), (G["ln_a_g"], G["ln_a_b"]) = rowvjp(
        n("lnsilu_b"), lnsilu_fn, [(S["u1"], 1024, 0)], [_rows(W["ln_a_g"]), _rows(W["ln_a_b"])], [(du3, 1024, 0)],
        [([0], F32, None)], T)
    du0, G["conv_a_w"], G["conv_a_b"] = conv_bwd(n("conv_a_b"), du1, S["u0"], 0, W["conv_a_w"], T)
    (dproj,), _ = rowvjp(n("glu_b"), glu_fn, [(proj, 1024, 0), (proj, 1024, 1)], [], [(du0, 1024, 0)],
                         [([0, 1], BF16, (dproj, 0))], T)
    G["w_b_out"] = matmul(n("mm_bout_w"), S["yn"], dy_bo, "tn", out_dtype=BF16)
    dyn = matmul(n("mm_bout_x"), dy_bo, W["w_b_out"], "nt")
    (dys, dxs, dproj), (ddsk, G["ssm_norm_g"]) = rowvjp(
        n("gnorm_b"), gnorm_fn, [(S["y_f"], 256, 0), (S["y_b"], 256, 0), (S["xbc"], 256, 0), (proj, 256, 16)],
        [S["dsk"], _rows(W["ssm_norm_g"])], [(dyn, 256, 0)],
        [([0], F32, None), ([2], F32, None), ([3], BF16, (dproj, 16))], T, groups=N_GROUPS)
    G["d_skip"] = ddsk.reshape(N_HEADS, HEAD_DIM).sum(axis=1)
    dx1, db1, dc1, ddt1, dbias_f, dalog_f = ssd_bwd(
        n("ssd_f_b"), S["xbc"], S["pdt"], S["bias_row"], S["alog_row"], S["hs_f"], dys, (dxs,), 0, T)
    dxx, dbb, dcc, ddt, dbias_r, dalog_r = ssd_bwd(
        n("ssd_r_b"), S["xbc"], S["pdt"], S["bias_row"], S["alog_row"], S["hs_b"], dys, (dx1, db1, dc1, ddt1), 1, T)
    G["dt_bias"] = (dbias_f + dbias_r)[0, :2 * N_HEADS].reshape(2, N_HEADS)
    G["a_log"] = (dalog_f + dalog_r)[0, :2 * N_HEADS].reshape(2, N_HEADS)
    cw = W["ssm_conv_w"]
    b0 = 6144 // CONV_CB
    dproj, dwx, dbx = conv_bwd(n("conv_sx_b"), dxx, proj, b0, cw[:, :D_INNER], T, into=(dproj, b0))
    dproj, dwb, dbb_ = conv_bwd(n("conv_sb_b"), dbb, proj, b0 + 4, cw[:, D_INNER:D_INNER + 1024], T, into=(dproj, b0 + 4))
    dproj, dwc, dbc = conv_bwd(n("conv_sc_b"), dcc, proj, b0 + 6, cw[:, D_INNER + 1024:], T, into=(dproj, b0 + 6))
    G["ssm_conv_w"] = jnp.concatenate([dwx, dwb, dwc], axis=1)
    G["ssm_conv_b"] = jnp.concatenate([dbx, dbb_, dbc], axis=1)
    (dproj,) = rowcall(n("dt_cast"), ident_fn, [(ddt, LANE, 0)], [], [(LANE, BF16, (dproj, (N_IN_PAD - LANE) // LANE))], T)
    G["w_in"] = matmul(n("mm_in_w"), dproj, x, "tn", out_dtype=BF16)[:N_IN]
    dx = matmul(n("mm_in_x"), dproj, W["w_in"], "nn", add=dxa)
    return dx, G


def local_step(x, p, loss_target, FW, T):
    Ws, saves = [], []
    cur = x
    for li in range(DEPTH):
        W = {k: v[li] for k, v in FW.items()}
        Ws.append(W)
        cur, S = layer_fwd(li, cur, p[li], W, T)
        saves.append(S)
    dcur, sq = loss_head(cur, loss_target, T)
    loss = 0.5 * jnp.sum(sq) / D_MODEL
    grads = [None] * DEPTH
    for li in reversed(range(DEPTH)):
        dcur, grads[li] = layer_bwd(li, dcur, p[li], Ws[li], saves[li], T)
    return loss, dcur, grads


def kernel(x, p, w_in, conv_a_w, conv_a_b, ln_a_g, ln_a_b, w_a_out, ssm_conv_w, ssm_conv_b, a_log, dt_bias, d_skip, ssm_norm_g, w_b_out, w_o, ln1_g, ln1_b, w_gate_up, w_down, ln2_g, ln2_b, w_ple, ple_norm_g, w_ple_gate, loss_target, m_w_in, m_conv_a_w, m_conv_a_b, m_ln_a_g, m_ln_a_b, m_w_a_out, m_ssm_conv_w, m_ssm_conv_b, m_a_log, m_dt_bias, m_d_skip, m_ssm_norm_g, m_w_b_out, m_w_o, m_ln1_g, m_ln1_b, m_w_gate_up, m_w_down, m_ln2_g, m_ln2_b, m_w_ple, m_ple_norm_g, m_w_ple_gate, v_w_in, v_conv_a_w, v_conv_a_b, v_ln_a_g, v_ln_a_b, v_w_a_out, v_ssm_conv_w, v_ssm_conv_b, v_a_log, v_dt_bias, v_d_skip, v_ssm_norm_g, v_w_b_out, v_w_o, v_ln1_g, v_ln1_b, v_w_gate_up, v_w_down, v_ln2_g, v_ln2_b, v_w_ple, v_ple_norm_g, v_w_ple_gate):
    A = dict(locals())
    w = {k: A[k] for k in WEIGHTS}
    m = {k: A["m_" + k] for k in WEIGHTS}
    v = {k: A["v_" + k] for k in WEIGHTS}
    T = x.shape[1]
    big_shapes = [w[k].shape for k in BIG]
    small_shapes = [w[k].shape for k in SMALL]

    mm_names = [k for k in BIG if k not in CONV_W]
    shards = [_shard_for_gather(k, w[k].astype(BF16)) for k in mm_names]
    gathered = all_gather("gather_weights", _pack(shards, 16))
    FW = {k: _full_from_gathered(k, g) for k, g in zip(mm_names, _unpack(gathered, [t.shape for t in shards], (N_DEV,)))}
    FW["w_in"] = jnp.pad(FW["w_in"], ((0, 0), (0, N_IN_PAD - N_IN), (0, 0)))
    gathered = all_gather("gather_conv_weights", _pack([w[k] for k in CONV_W], SUBLANE))
    FW.update({k: _full_from_gathered(k, g)
               for k, g in zip(CONV_W, _unpack(gathered, [w[k].shape for k in CONV_W], (N_DEV,)))})
    FW.update({k: w[k] for k in SMALL})

    loss, grad_x, gfull = local_step(x[0], p[:, 0], loss_target[0], FW, T)
    loss = lax.psum(loss, ("x", "y", "c"))

    TR = 512
    flat = jnp.concatenate([_pieces_from_full(k, gfull[li][k]).astype(BF16) for k in BIG for li in range(DEPTH)], axis=1)
    rows = -(-flat.shape[1] // (LANE * TR)) * TR
    gpack = jnp.pad(flat, ((0, 0), (0, rows * LANE - flat.shape[1]))).reshape(N_DEV, rows, LANE)
    landed = grad_exchange("grad_exchange", gpack)
    gsum = sum_slots("sum_grad_pieces", landed, TR)
    gshard = [jnp.swapaxes(g, 1, 2) if k in COL_T else g
              for k, g in zip(BIG, _unpack(gsum, [_piece_shape(k, w[k].shape) for k in BIG]))]
    res_big = adamw("adamw_big", _pack(gshard, TR)[None], _pack([w[k] for k in BIG], TR), _pack([m[k] for k in BIG], TR),
                    _pack([v[k] for k in BIG], TR), TR)
    res_big = [dict(zip(BIG, _unpack(r, big_shapes))) for r in res_big]

    spack = _pack([jnp.stack([gfull[li][k] for li in range(DEPTH)]).reshape(w[k].shape) for k in SMALL], SUBLANE)
    sall = all_gather("gather_small_grads", spack)
    rs = spack.shape[0]
    res_small = adamw("adamw_small", sall, _pack([w[k] for k in SMALL], SUBLANE), _pack([m[k] for k in SMALL], SUBLANE),
                      _pack([v[k] for k in SMALL], SUBLANE), rs)
    res_small = [dict(zip(SMALL, _unpack(r, small_shapes))) for r in res_small]

    outs = [loss, grad_x[None]]
    for q in range(4):
        for k in WEIGHTS:
            outs.append(res_big[q][k] if k in res_big[q] else res_small[q][k])
    return tuple(outs)
```

```python
import math

import jax
import jax.numpy as jnp
from jax import lax
from jax.experimental import pallas as pl
from jax.experimental.pallas import tpu as pltpu

F32 = jnp.float32
BF16 = jnp.bfloat16

D_MODEL = 1024
CONV_DIM = 1024
CONV_KERNEL = 31
D_INNER = 2048
HEAD_DIM = 64
N_HEADS = 32
N_GROUPS = 8
D_STATE = 128
SSM_CONV = 5
CHUNK = 128
XBC_DIM = D_INNER + 2 * N_GROUPS * D_STATE
FFN_DIM = 2816
PLE_DIM = 256
N_IN = 2 * CONV_DIM + 2 * D_MODEL + D_INNER + XBC_DIM + 2 * N_HEADS
N_IN_PAD = 10368
DEPTH = 2
N_DEV = 8
ALPHA = (2 * DEPTH) ** 0.25
LN_EPS = 1e-5
RMS_EPS = 1e-6
ADAM_LR, ADAM_B1, ADAM_B2, ADAM_EPS, ADAM_WD, ADAM_STEP = 0.001, 0.9, 0.999, 1e-08, 0.01, 10

LANE = 128
SUBLANE = 8
HALO = 16
VMEM_LIMIT = 52 * 1024 * 1024
NEG = -1e30

BIG = ["w_in", "conv_a_w", "w_a_out", "ssm_conv_w", "w_b_out", "w_o", "w_gate_up", "w_down", "w_ple", "w_ple_gate"]
BIG_AXIS = {"w_in": 2, "conv_a_w": 2, "w_a_out": 1, "ssm_conv_w": 2, "w_b_out": 1, "w_o": 1, "w_gate_up": 2,
            "w_down": 1, "w_ple": 2, "w_ple_gate": 1}
COL_T = ["w_in", "w_gate_up"]
CONV_W = ["conv_a_w", "ssm_conv_w"]
SMALL = ["conv_a_b", "ln_a_g", "ln_a_b", "ssm_conv_b", "a_log", "dt_bias", "d_skip", "ssm_norm_g", "ln1_g", "ln1_b",
         "ln2_g", "ln2_b", "ple_norm_g"]
WEIGHTS = ["w_in", "conv_a_w", "conv_a_b", "ln_a_g", "ln_a_b", "w_a_out", "ssm_conv_w", "ssm_conv_b", "a_log", "dt_bias",
           "d_skip", "ssm_norm_g", "w_b_out", "w_o", "ln1_g", "ln1_b", "w_gate_up", "w_down", "ln2_g", "ln2_b", "w_ple",
           "ple_norm_g", "w_ple_gate"]


def _cparams(sem):
    return pltpu.CompilerParams(dimension_semantics=sem, vmem_limit_bytes=VMEM_LIMIT)


def _pick(n, cap):
    if n <= cap:
        return n
    best = None
    for d in range(LANE, cap + 1, LANE):
        if n % d == 0:
            best = d
    assert best is not None, (n, cap)
    return best


def matmul(name, a, b, mode, out_dtype=F32, add=None):
    if mode == "nn":
        (M, K), (K2, N) = a.shape, b.shape
    elif mode == "nt":
        (M, K), (N, K2) = a.shape, b.shape
    else:
        (K, M), (K2, N) = a.shape, b.shape
    assert K == K2, (name, a.shape, b.shape)
    tm = _pick(M, 1024) if mode != "tn" else _pick(M, 1408)
    tn = _pick(N, 1408)
    tk = _pick(K, 512) if mode == "tn" else (K if (mode == "nn" and K <= 2816) else _pick(K, 1408))
    nk = K // tk
    grid = (M // tm, N // tn, nk)
    if mode == "tn":
        a_spec = pl.BlockSpec((tk, tm), lambda i, j, k: (k, i))
    else:
        a_spec = pl.BlockSpec((tm, tk), lambda i, j, k: (i, k))
    if mode == "nt":
        b_spec = pl.BlockSpec((tn, tk), lambda i, j, k: (j, k))
    else:
        b_spec = pl.BlockSpec((tk, tn), lambda i, j, k: (k, j))
    o_spec = pl.BlockSpec((tm, tn), lambda i, j, k: (i, j))
    dims = {"nn": ((1,), (0,)), "nt": ((1,), (1,)), "tn": ((0,), (0,))}[mode]
    has_add = add is not None

    def body(a_ref, b_ref, *rest):
        if has_add:
            add_ref, o_ref, *scr = rest
        else:
            o_ref, *scr = rest
        part = lax.dot_general(a_ref[...].astype(BF16), b_ref[...].astype(BF16), (dims, ((), ())),
                               preferred_element_type=F32)

        def finish(v):
            if has_add:
                v = v + add_ref[...].astype(F32)
            o_ref[...] = v.astype(o_ref.dtype)

        if nk == 1:
            finish(part)
        else:
            acc = scr[0]
            k = pl.program_id(2)

            @pl.when(k == 0)
            def _():
                acc[...] = part

            @pl.when(k > 0)
            def _():
                acc[...] += part

            @pl.when(k == nk - 1)
            def _():
                finish(acc[...])

    in_specs = [a_spec, b_spec] + ([o_spec] if has_add else [])
    args = (a, b) + ((add,) if has_add else ())
    return pl.pallas_call(
        body, out_shape=jax.ShapeDtypeStruct((M, N), out_dtype), grid=grid, in_specs=in_specs, out_specs=o_spec,
        scratch_shapes=[pltpu.VMEM((tm, tn), F32)] if nk > 1 else [], name=name,
        compiler_params=_cparams(("parallel", "parallel", "arbitrary")))(*args)


def _row_specs(items, tT, groups):
    specs = []
    for (_, w, blk) in items:
        assert blk % groups == 0
        specs.append(pl.BlockSpec((tT, w * groups), (lambda i, b=blk // groups: (i, b))))
    return specs


def _slices(v, groups):
    if groups == 1:
        return [v]
    w = v.shape[1] // groups
    return [v[:, w * s:w * (s + 1)] for s in range(groups)]


def _cat(vs):
    return vs[0] if len(vs) == 1 else jnp.concatenate(vs, axis=1)


def rowcall(name, fn, ins, pars, outs, T, tT=256, groups=1):
    n_in, n_par = len(ins), len(pars)
    intos = [o[2] for o in outs if o[2] is not None]
    in_specs = (_row_specs(ins, tT, groups) + [pl.BlockSpec(p.shape, lambda i: (0, 0)) for p in pars]
                + [pl.BlockSpec(memory_space=pl.ANY)] * len(intos))
    out_specs, out_shapes, aliases = [], [], {}
    n_alias = 0
    for oi, (w, dt, into) in enumerate(outs):
        if into is None:
            out_specs.append(pl.BlockSpec((tT, w * groups), lambda i: (i, 0)))
            out_shapes.append(jax.ShapeDtypeStruct((T, w * groups), dt))
        else:
            arr, blk = into
            out_specs.append(pl.BlockSpec((tT, w * groups), lambda i, b=blk // groups: (i, b)))
            out_shapes.append(jax.ShapeDtypeStruct(arr.shape, arr.dtype))
            aliases[n_in + n_par + n_alias] = oi
            n_alias += 1

    def body(*refs):
        xs = [_slices(r[...].astype(F32), groups) for r in refs[:n_in]]
        ps = [_slices(r[...], groups) for r in refs[n_in:n_in + n_par]]
        o_refs = refs[n_in + n_par + n_alias:]
        res = [fn(*[x[s] for x in xs], *[p[s] for p in ps]) for s in range(groups)]
        for k, r in enumerate(o_refs):
            r[...] = _cat([res[s][k] for s in range(groups)]).astype(r.dtype)

    res = pl.pallas_call(
        body, out_shape=out_shapes, grid=(T // tT,), in_specs=in_specs, out_specs=out_specs,
        input_output_aliases=aliases, name=name, compiler_params=_cparams(("parallel",)))(
            *[a for (a, _, _) in ins], *pars, *[a for (a, _) in intos])
    return list(res)


def rowvjp(name, fn, ins, pars, cts, douts, T, tT=256, groups=1):
    n_in, n_par, n_ct = len(ins), len(pars), len(cts)
    intos = [o[2] for o in douts if o[2] is not None and not isinstance(o[2][0], jax.ShapeDtypeStruct)]
    in_specs = (_row_specs(ins, tT, groups) + [pl.BlockSpec(p.shape, lambda i: (0, 0)) for p in pars]
                + _row_specs(cts, tT, groups) + [pl.BlockSpec(memory_space=pl.ANY)] * len(intos))
    out_specs, out_shapes, aliases = [], [], {}
    n_alias = 0
    for oi, (idxs, dt, into) in enumerate(douts):
        w = sum(ins[k][1] for k in idxs) * groups
        if into is None:
            out_specs.append(pl.BlockSpec((tT, w), lambda i: (i, 0)))
            out_shapes.append(jax.ShapeDtypeStruct((T, w), dt))
        else:
            assert len(idxs) == 1 or groups == 1
            arr, blk = into
            out_specs.append(pl.BlockSpec((tT, w), lambda i, b=blk // groups: (i, b)))
            out_shapes.append(jax.ShapeDtypeStruct(arr.shape, arr.dtype))
            if not isinstance(arr, jax.ShapeDtypeStruct):
                aliases[n_in + n_par + n_ct + n_alias] = oi
                n_alias += 1
    n_dout = len(douts)
    for p in pars:
        out_specs.append(pl.BlockSpec(p.shape, lambda i: (0, 0)))
        out_shapes.append(jax.ShapeDtypeStruct(p.shape, F32))

    def body(*refs):
        xs = [_slices(r[...].astype(F32), groups) for r in refs[:n_in]]
        ps = [_slices(r[...], groups) for r in refs[n_in:n_in + n_par]]
        cs = [_slices(r[...].astype(F32), groups) for r in refs[n_in + n_par:n_in + n_par + n_ct]]
        o_refs = refs[n_in + n_par + n_ct + n_alias:]
        grads = []
        for s in range(groups):
            _, vjp_fn = jax.vjp(fn, *[x[s] for x in xs], *[p[s] for p in ps])
            grads.append(vjp_fn(tuple(c[s] for c in cs)))
        for r, (idxs, _, _) in zip(o_refs[:n_dout], douts):
            r[...] = _cat([grads[s][k] for k in idxs for s in range(groups)]).astype(r.dtype)
        for k, r in enumerate(o_refs[n_dout:]):
            @pl.when(pl.program_id(0) == 0)
            def _(r=r):
                r[...] = jnp.zeros(r.shape, F32)
            r[...] += _cat([grads[s][n_in + k] for s in range(groups)])

    res = pl.pallas_call(
        body, out_shape=out_shapes, grid=(T // tT,), in_specs=in_specs, out_specs=out_specs,
        input_output_aliases=aliases, name=name, compiler_params=_cparams(("arbitrary",)))(
            *[a for (a, _, _) in ins], *pars, *[a for (a, _, _) in cts], *[a for (a, _) in intos])
    res = list(res)
    return res[:n_dout], res[n_dout:]


def _sigmoid(x):
    return 1.0 / (1.0 + jnp.exp(-x))


def _silu(x):
    return x * _sigmoid(x)


def _softplus(x):
    return jnp.maximum(x, 0.0) + jnp.log(1.0 + jnp.exp(-jnp.abs(x)))


def _ln(x, g, b):
    mu = jnp.mean(x, axis=-1, keepdims=True)
    xc = x - mu
    var = jnp.mean(xc * xc, axis=-1, keepdims=True)
    return xc * lax.rsqrt(var + LN_EPS) * g + b


def glu_fn(a, gt):
    return (a * _sigmoid(gt),)


def lnsilu_fn(u, g, b):
    return (_silu(_ln(u, g, b)),)


def gnorm_fn(yf, yb, xp, z, dsk, ng):
    y = (yf + yb + _silu(xp) * dsk) * _silu(z)
    return (y * lax.rsqrt(jnp.mean(y * y, axis=-1, keepdims=True) + RMS_EPS) * ng,)


def merge_fn(ga, gb, ya, yb):
    return (_sigmoid(ga) * ya + _sigmoid(gb) * yb,)


def resln_fn(x, r, g, b):
    return (_ln(ALPHA * x + r, g, b),)


def swiglu_fn(g, u):
    return (_silu(g) * u,)


def ple_fn(h2, pe, gl, g):
    e = pe * lax.rsqrt(jnp.mean(pe * pe, axis=-1, keepdims=True) + RMS_EPS) * g
    return (h2 + e * _sigmoid(gl),)


def ident_fn(v):
    return (v,)


CONV_CB = 512
CONV_TT = 512
CONV_TILES = 4
CONV_RB = CONV_TILES * SUBLANE
CONV_RED_TILES = 2
CONV_STATIC_MAX_K = 8


def _conv_specs(blk0, T, tT, cb):
    nh = tT // HALO
    cur = pl.BlockSpec((tT, cb), lambda j, i: (i, blk0 + j))
    prev = pl.BlockSpec((HALO, cb), lambda j, i: (jnp.maximum(i * nh - 1, 0), blk0 + j))
    nxt = pl.BlockSpec((HALO, cb), lambda j, i: (jnp.minimum((i + 1) * nh, T // HALO - 1), blk0 + j))
    return [prev, cur, nxt]


def _phases(offsets):
    return sorted({off % SUBLANE for off in offsets})


def _fill_padded(pad_ref, prev_ref, cur_ref, next_ref, i, n_t, tT):
    pad_ref[pl.ds(0, HALO), :] = prev_ref[...].astype(F32) * (i > 0).astype(F32)
    pad_ref[pl.ds(HALO, tT), :] = cur_ref[...].astype(F32)
    pad_ref[pl.ds(HALO + tT, HALO), :] = next_ref[...].astype(F32) * (i < n_t - 1).astype(F32)


def _fill_shifted(sh_ref, pad_ref, phases, tT):
    for ph in phases:
        sh_ref[ph] = pad_ref[pl.ds(ph, tT + 3 * SUBLANE), :]


class _Shifted:
    def __init__(self, pad_ref, sh_ref, offsets, tT, static):
        self.pad_ref, self.sh_ref, self.static = pad_ref, sh_ref, static
        if not static:
            _fill_shifted(sh_ref, pad_ref, _phases(offsets), tT)

    def tiles(self, base, ls, off, n_tiles):
        if self.static:
            return tuple(self.pad_ref[pl.ds(base + off + SUBLANE * t, SUBLANE), ls] for t in range(n_tiles))
        q, ph = divmod(off, SUBLANE)
        return tuple(self.sh_ref[ph, pl.ds(base + SUBLANE * (q + t), SUBLANE), ls] for t in range(n_tiles))


def _row_loop(static, n, body, init):
    if not static:
        return lax.fori_loop(0, n, body, init)
    carry = init
    for r in range(n):
        carry = body(r, carry)
    return carry


def _conv_rows(src, w_ref, bias, o_ref, offsets, tT, cb):
    K = len(offsets)
    for lt in range(cb // LANE):
        ls = slice(LANE * lt, LANE * (lt + 1))
        wv = [jnp.broadcast_to(w_ref[k:k + 1, ls], (SUBLANE, LANE)) for k in range(K)]
        b0 = jnp.zeros((SUBLANE, LANE), F32) if bias is None else jnp.broadcast_to(bias[:, ls], (SUBLANE, LANE))

        def rows(r, carry, ls=ls, wv=wv, b0=b0):
            base = r * CONV_RB if src.static else pl.multiple_of(r * CONV_RB, CONV_RB)
            accs = [b0] * CONV_TILES
            for k, off in enumerate(offsets):
                accs = [a + d * wv[k] for a, d in zip(accs, src.tiles(base, ls, off, CONV_TILES))]
            o_ref[pl.ds(base, CONV_RB), ls] = jnp.concatenate(accs, axis=0).astype(o_ref.dtype)
            return carry

        _row_loop(src.static, tT // CONV_RB, rows, 0)


def conv_fwd(name, u, blk0, w, b, T):
    K, C = w.shape
    P = (K - 1) // 2
    tT, cb = min(CONV_TT, T), CONV_CB
    n_t = T // tT

    static = K <= CONV_STATIC_MAX_K

    def body(prev_ref, cur_ref, next_ref, w_ref, b_ref, o_ref, pad_ref, sh_ref):
        offsets = [HALO - P + k for k in range(K)]
        _fill_padded(pad_ref, prev_ref, cur_ref, next_ref, pl.program_id(1), n_t, tT)
        _conv_rows(_Shifted(pad_ref, sh_ref, offsets, tT, static), w_ref, b_ref[...], o_ref, offsets, tT, cb)

    return pl.pallas_call(
        body, out_shape=jax.ShapeDtypeStruct((T, C), F32), grid=(C // cb, n_t),
        in_specs=_conv_specs(blk0, T, tT, cb) + [pl.BlockSpec((K, cb), lambda j, i: (0, j)),
                                                  pl.BlockSpec((1, cb), lambda j, i: (0, j))],
        out_specs=pl.BlockSpec((tT, cb), lambda j, i: (i, j)),
        scratch_shapes=[pltpu.VMEM((tT + 2 * HALO, cb), F32), pltpu.VMEM((SUBLANE, tT + 3 * SUBLANE, cb), F32)],
        name=name, compiler_params=_cparams(("parallel", "arbitrary")))(u, u, u, w, b)


def conv_bwd(name, dy, u, blk0, w, T, into=None):
    K, C = w.shape
    P = (K - 1) // 2
    tT, cb = min(CONV_TT, T), CONV_CB
    n_t = T // tT

    static = K <= CONV_STATIC_MAX_K

    def body(dprev, dcur, dnext, uprev, ucur, unext, w_ref, *rest):
        if into is not None:
            rest = rest[1:]
        du_ref, dw_ref, db_ref, padd_ref, padu_ref, shd_ref, shu_ref = rest
        i = pl.program_id(1)
        offsets = [HALO - P + k for k in range(K)]
        back = [HALO + P - k for k in range(K)]
        _fill_padded(padd_ref, dprev, dcur, dnext, i, n_t, tT)
        _fill_padded(padu_ref, uprev, ucur, unext, i, n_t, tT)
        src_d = _Shifted(padd_ref, shd_ref, back + [HALO], tT, static)
        src_u = _Shifted(padu_ref, shu_ref, offsets, tT, static)
        _conv_rows(src_d, w_ref, None, du_ref, back, tT, cb)

        @pl.when(i == 0)
        def _():
            dw_ref[...] = jnp.zeros(dw_ref.shape, F32)
            db_ref[...] = jnp.zeros(db_ref.shape, F32)

        rb = CONV_RED_TILES * SUBLANE
        for lt in range(cb // LANE):
            ls = slice(LANE * lt, LANE * (lt + 1))

            def red(r, accs, ls=ls):
                base = r * rb if static else pl.multiple_of(r * rb, rb)
                d0, d1 = src_d.tiles(base, ls, HALO, CONV_RED_TILES)
                new = []
                for acc, off in zip(accs[:K], offsets):
                    u0, u1 = src_u.tiles(base, ls, off, CONV_RED_TILES)
                    new.append(acc + d0 * u0 + d1 * u1)
                return tuple(new) + (accs[K] + d0 + d1,)

            zero = jnp.zeros((SUBLANE, LANE), F32)
            accs = _row_loop(static, tT // rb, red, (zero,) * (K + 1))
            for k in range(K):
                dw_ref[k:k + 1, ls] += jnp.sum(accs[k], axis=0, keepdims=True)
            db_ref[:, ls] += jnp.sum(accs[K], axis=0, keepdims=True)

    dspecs = _conv_specs(0, T, tT, cb)
    uspecs = _conv_specs(blk0, T, tT, cb)
    in_specs = dspecs + uspecs + [pl.BlockSpec((K, cb), lambda j, i: (0, j))]
    args = [dy, dy, dy, u, u, u, w]
    aliases = {}
    if into is None:
        du_spec = pl.BlockSpec((tT, cb), lambda j, i: (i, j))
        du_shape = jax.ShapeDtypeStruct((T, C), F32)
    else:
        arr, oblk = into
        in_specs.append(pl.BlockSpec(memory_space=pl.ANY))
        args.append(arr)
        aliases = {7: 0}
        du_spec = pl.BlockSpec((tT, cb), lambda j, i: (i, oblk + j))
        du_shape = jax.ShapeDtypeStruct(arr.shape, arr.dtype)
    return pl.pallas_call(
        body, out_shape=[du_shape, jax.ShapeDtypeStruct((K, C), F32), jax.ShapeDtypeStruct((1, C), F32)],
        grid=(C // cb, n_t), in_specs=in_specs,
        out_specs=[du_spec, pl.BlockSpec((K, cb), lambda j, i: (0, j)), pl.BlockSpec((1, cb), lambda j, i: (0, j))],
        scratch_shapes=[pltpu.VMEM((tT + 2 * HALO, cb), F32), pltpu.VMEM((tT + 2 * HALO, cb), F32),
                        pltpu.VMEM((SUBLANE, tT + 3 * SUBLANE, cb), F32), pltpu.VMEM((SUBLANE, tT + 3 * SUBLANE, cb), F32)],
        input_output_aliases=aliases, name=name, compiler_params=_cparams(("arbitrary", "arbitrary")))(*args)


def _dot(a, b, dims):
    return lax.dot_general(a.astype(BF16), b.astype(BF16), (dims, ((), ())), preferred_element_type=F32)


def _dnn(a, b):
    return _dot(a, b, ((1,), (0,)))


def _dnt(a, b):
    return _dot(a, b, ((1,), (1,)))


def _dtn(a, b):
    return _dot(a.T, b, ((1,), (0,)))


@jax.custom_vjp
def _nn(a, b):
    return _dnn(a, b)


_nn.defvjp(lambda a, b: (_dnn(a, b), (a, b)), lambda r, g: (_dnt(g, r[1]), _dtn(r[0], g)))


@jax.custom_vjp
def _nt(a, b):
    return _dnt(a, b)


_nt.defvjp(lambda a, b: (_dnt(a, b), (a, b)), lambda r, g: (_dnn(g, r[1]), _dtn(g, r[0])))


@jax.custom_vjp
def _tn(a, b):
    return _dtn(a, b)


_tn.defvjp(lambda a, b: (_dtn(a, b), (a, b)), lambda r, g: (_dnt(r[1], g), _dnn(r[0], g)))


def _split_dot(m, v):
    hi = v.astype(BF16)
    r1 = v - hi.astype(F32)
    mid = r1.astype(BF16)
    lo = (r1 - mid.astype(F32)).astype(BF16)
    mb = m.astype(BF16)
    d = lambda x: lax.dot_general(mb, x, (((1,), (0,)), ((), ())), preferred_element_type=F32)
    return d(hi) + d(mid) + d(lo)


@jax.custom_vjp
def _tri_dot(tri, tri_t, v):
    return _split_dot(tri, v)


_tri_dot.defvjp(lambda tri, tri_t, v: (_split_dot(tri, v), (tri, tri_t)),
                lambda r, g: (jnp.zeros_like(r[0]), jnp.zeros_like(r[1]), _split_dot(r[1], g)))


def _pick_vjp(axis):
    def pick(v, h):
        return v[:, h:h + 1] if axis == 1 else v[h:h + 1, :]

    def fwd(v, h):
        return pick(v, h), v.shape

    def bwd(h, shape, g):
        idx = lax.broadcasted_iota(jnp.int32, shape, axis)
        return (jnp.where(idx == h, g, 0.0),)

    f = jax.custom_vjp(pick, nondiff_argnums=(1,))
    f.defvjp(fwd, bwd)
    return f


_lane_pick = _pick_vjp(1)
_sub_pick = _pick_vjp(0)


def _onehot_lane(v, h):
    lane = lax.broadcasted_iota(jnp.int32, (1, v.shape[1]), 1)
    return jnp.sum(v * (lane == h).astype(F32), axis=1, keepdims=True)


def _onehot_sub(v, h):
    sub = lax.broadcasted_iota(jnp.int32, (v.shape[0], 1), 0)
    return jnp.sum(v * (sub == h).astype(F32), axis=0, keepdims=True)


def _ssd_consts(dirn, picks):
    ri = lax.broadcasted_iota(jnp.int32, (CHUNK, CHUNK), 0)
    ci = lax.broadcasted_iota(jnp.int32, (CHUNK, CHUNK), 1)
    keep = (ci <= ri) if dirn == 0 else (ci >= ri)
    tri = keep.astype(F32)
    tri_t = (~keep | (ci == ri)).astype(F32)
    lane = lax.broadcasted_iota(jnp.int32, (1, LANE), 1)
    sub = lax.broadcasted_iota(jnp.int32, (CHUNK, 1), 0)
    end = (sub == (CHUNK - 1 if dirn == 0 else 0)).astype(F32)
    lo_half = lane < HEAD_DIM
    pick = (_lane_pick, _sub_pick) if picks else (_onehot_lane, _onehot_sub)
    return keep, tri, tri_t, end, lo_half, N_HEADS * dirn, pick


def _ssd_chunk(consts, x_t, b_t, c_t, dtr, bias, alog, h_t):
    keep, tri, tri_t, end, lo_half, h_base, (lane_pick, sub_pick) = consts
    dt = _softplus(dtr + bias)
    a = dt * (-jnp.exp(alog))
    cs = _tri_dot(tri, tri_t, a)
    cs_t = cs.T
    tot = jnp.sum(cs * end, axis=0, keepdims=True)
    ys, hn = [], []
    for g in range(N_GROUPS):
        bm, cm = _silu(b_t[g]), _silu(c_t[g])
        gm = _nt(cm, bm)
        for jj in range(2):
            j = 2 * g + jj
            hh = (h_base + 2 * j, h_base + 2 * j + 1)
            col = [lane_pick(cs, h) for h in hh]
            row = [sub_pick(cs_t, h) for h in hh]
            dth = [lane_pick(dt, h) for h in hh]
            toth = [lane_pick(tot, h) for h in hh]
            xd = _silu(x_t[j]) * jnp.where(lo_half, dth[0], dth[1])
            yd = [_nn(gm * jnp.exp(jnp.where(keep, col[k] - row[k], NEG)), xd) for k in range(2)]
            cp = jnp.where(lo_half, col[0], col[1])
            tp = jnp.where(lo_half, toth[0], toth[1])
            ys.append(jnp.where(lo_half, yd[0], yd[1]) + _nn(cm, h_t[j]) * jnp.exp(cp))
            hn.append(h_t[j] * jnp.exp(tp) + _tn(bm, xd * jnp.exp(tp - cp)))
    return ys, hn


N_PAIR = D_INNER // LANE


def _tiles(ref, n):
    return [ref[:, LANE * j:LANE * (j + 1)].astype(F32) for j in range(n)]


def _ssd_in_specs(cmap):
    return [pl.BlockSpec((CHUNK, D_INNER), lambda i: (cmap(i), 0)),
            pl.BlockSpec((CHUNK, N_GROUPS * D_STATE), lambda i: (cmap(i), 2)),
            pl.BlockSpec((CHUNK, N_GROUPS * D_STATE), lambda i: (cmap(i), 3)),
            pl.BlockSpec((CHUNK, LANE), lambda i: (cmap(i), 0)),
            pl.BlockSpec((1, LANE), lambda i: (0, 0)), pl.BlockSpec((1, LANE), lambda i: (0, 0))]


def ssd_fwd(name, xbc, pdt, bias_row, alog_row, dirn, T):
    nc = T // CHUNK
    cmap = (lambda i: i) if dirn == 0 else (lambda i: nc - 1 - i)

    def body(x_ref, b_ref, c_ref, dt_ref, bias_ref, alog_ref, y_ref, hs_ref, h_scr):
        @pl.when(pl.program_id(0) == 0)
        def _():
            h_scr[...] = jnp.zeros(h_scr.shape, F32)

        hs_ref[0] = h_scr[...]
        ys, hn = _ssd_chunk(_ssd_consts(dirn, True), _tiles(x_ref, N_PAIR), _tiles(b_ref, N_GROUPS), _tiles(c_ref, N_GROUPS),
                            dt_ref[...], bias_ref[...], alog_ref[...], _tiles(h_scr, N_PAIR))
        for j in range(N_PAIR):
            y_ref[:, LANE * j:LANE * (j + 1)] = ys[j]
            h_scr[:, LANE * j:LANE * (j + 1)] = hn[j]

    return pl.pallas_call(
        body, out_shape=[jax.ShapeDtypeStruct((T, D_INNER), F32), jax.ShapeDtypeStruct((nc, D_STATE, D_INNER), F32)],
        grid=(nc,), in_specs=_ssd_in_specs(cmap),
        out_specs=[pl.BlockSpec((CHUNK, D_INNER), lambda i: (cmap(i), 0)),
                   pl.BlockSpec((1, D_STATE, D_INNER), lambda i: (cmap(i), 0, 0))],
        scratch_shapes=[pltpu.VMEM((D_STATE, D_INNER), F32)], name=name,
        compiler_params=_cparams(("arbitrary",)))(xbc, xbc, xbc, pdt, bias_row, alog_row)


def ssd_bwd(name, xbc, pdt, bias_row, alog_row, hs, dy, adds, dirn, T):
    nc = T // CHUNK
    cmap = (lambda i: nc - 1 - i) if dirn == 0 else (lambda i: i)
    GS = N_GROUPS * D_STATE

    n_add = len(adds)

    def body(x_ref, b_ref, c_ref, dt_ref, bias_ref, alog_ref, hs_ref, dy_ref, *rest):
        add_refs, (dx_ref, db_ref, dc_ref, ddt_ref, dbias_ref, dalog_ref, dh_scr) = rest[:n_add], rest[n_add:]
        ax_ref = add_refs[0]
        ab_ref, ac_ref, adt_ref = add_refs[1:] if n_add == 4 else (None, None, None)
        first = pl.program_id(0) == 0

        @pl.when(first)
        def _():
            dh_scr[...] = jnp.zeros(dh_scr.shape, F32)
            dbias_ref[...] = jnp.zeros(dbias_ref.shape, F32)
            dalog_ref[...] = jnp.zeros(dalog_ref.shape, F32)

        consts = _ssd_consts(dirn, False)
        fn = lambda *a: _ssd_chunk(consts, *a)
        _, vjp_fn = jax.vjp(fn, _tiles(x_ref, N_PAIR), _tiles(b_ref, N_GROUPS), _tiles(c_ref, N_GROUPS), dt_ref[...],
                            bias_ref[...], alog_ref[...], [hs_ref[0, :, LANE * j:LANE * (j + 1)] for j in range(N_PAIR)])
        dx, db, dc, ddt, dbias, dalog, dh = vjp_fn((_tiles(dy_ref, N_PAIR), _tiles(dh_scr, N_PAIR)))
        for j in range(N_PAIR):
            s = slice(LANE * j, LANE * (j + 1))
            dx_ref[:, s] = dx[j] + ax_ref[:, s]
            dh_scr[:, s] = dh[j]
        for g in range(N_GROUPS):
            s = slice(LANE * g, LANE * (g + 1))
            db_ref[:, s] = db[g] + (ab_ref[:, s] if n_add == 4 else 0.0)
            dc_ref[:, s] = dc[g] + (ac_ref[:, s] if n_add == 4 else 0.0)
        ddt_ref[...] = ddt + (adt_ref[...] if n_add == 4 else 0.0)
        dbias_ref[...] += dbias
        dalog_ref[...] += dalog

    blk = lambda w: pl.BlockSpec((CHUNK, w), lambda i: (cmap(i), 0))
    row = pl.BlockSpec((1, LANE), lambda i: (0, 0))
    return pl.pallas_call(
        body,
        out_shape=[jax.ShapeDtypeStruct((T, D_INNER), F32), jax.ShapeDtypeStruct((T, GS), F32),
                   jax.ShapeDtypeStruct((T, GS), F32), jax.ShapeDtypeStruct((T, LANE), F32),
                   jax.ShapeDtypeStruct((1, LANE), F32), jax.ShapeDtypeStruct((1, LANE), F32)],
        grid=(nc,),
        in_specs=_ssd_in_specs(cmap) + [pl.BlockSpec((1, D_STATE, D_INNER), lambda i: (cmap(i), 0, 0)), blk(D_INNER)]
        + [blk(D_INNER), blk(GS), blk(GS), blk(LANE)][:n_add],
        out_specs=[blk(D_INNER), blk(GS), blk(GS), blk(LANE), row, row],
        scratch_shapes=[pltpu.VMEM((D_STATE, D_INNER), F32)], name=name,
        compiler_params=_cparams(("arbitrary",)))(xbc, xbc, xbc, pdt, bias_row, alog_row, hs, dy, *adds)


def loss_head(y, target, T, tT=256):
    def body(y_ref, t_ref, dy_ref, sq_ref):
        @pl.when(pl.program_id(0) == 0)
        def _():
            sq_ref[...] = jnp.zeros(sq_ref.shape, F32)
        e = y_ref[...] - t_ref[...]
        dy_ref[...] = e * (1.0 / D_MODEL)
        sq_ref[...] += jnp.sum(e * e, axis=0, keepdims=True)

    spec = pl.BlockSpec((tT, D_MODEL), lambda i: (i, 0))
    return pl.pallas_call(
        body, out_shape=[jax.ShapeDtypeStruct((T, D_MODEL), F32), jax.ShapeDtypeStruct((1, D_MODEL), F32)],
        grid=(T // tT,), in_specs=[spec, spec], out_specs=[spec, pl.BlockSpec((1, D_MODEL), lambda i: (0, 0))],
        name="loss_head", compiler_params=_cparams(("arbitrary",)))(y, target)


MESH_ID = pl.DeviceIdType.MESH


def all_gather(name, v):
    R, W = v.shape

    def body(v_ref, out_ref, send_sems, recv_sems, local_sem):
        x, y, c = lax.axis_index("x"), lax.axis_index("y"), lax.axis_index("c")
        me, sibling = (x, y, c), (x, y, 1 - c)
        chips = [(1 - x, y), (x, 1 - y), (1 - x, 1 - y)]

        def slot(px, py, pc):
            return out_ref.at[4 * px + 2 * py + pc]

        def copy(k, block, to, src=None):
            return pltpu.make_async_remote_copy(
                src_ref=slot(*block) if src is None else src, dst_ref=slot(*block), send_sem=send_sems.at[k],
                recv_sem=recv_sems.at[k], device_id=to, device_id_type=MESH_ID)

        mine = pltpu.make_async_copy(v_ref, slot(*me), local_sem)
        mine.start()
        first = [copy(0, me, sibling, src=v_ref)]
        first += [copy(1 + j, me, (*chip, c), src=v_ref) for j, chip in enumerate(chips)]
        for cp in first:
            cp.start()
        passed = [copy(4 + j, (*chip, c), sibling) for j, chip in enumerate(chips)]
        for j, chip in enumerate(chips):
            copy(1 + j, (*chip, c), me).wait_recv()
            passed[j].start()
        copy(0, sibling, me).wait_recv()
        for j, chip in enumerate(chips):
            copy(4 + j, (*chip, 1 - c), me).wait_recv()
        for cp in first + passed:
            cp.wait_send()
        mine.wait()

    return pl.pallas_call(
        body, out_shape=jax.ShapeDtypeStruct((N_DEV, R, W), v.dtype),
        in_specs=[pl.BlockSpec(memory_space=pl.ANY)], out_specs=pl.BlockSpec(memory_space=pl.ANY),
        scratch_shapes=[pltpu.SemaphoreType.DMA((7,)), pltpu.SemaphoreType.DMA((7,)), pltpu.SemaphoreType.DMA],
        name=name, compiler_params=pltpu.CompilerParams(has_side_effects=True))(v)


def grad_exchange(name, g):
    _, R, W = g.shape

    def body(g_ref, out_ref, send_sems, recv_sems, local_sem):
        x, y, c = lax.axis_index("x"), lax.axis_index("y"), lax.axis_index("c")
        me = 4 * x + 2 * y + c
        mine = pltpu.make_async_copy(g_ref.at[me], out_ref.at[me], local_sem)
        mine.start()
        copies = []
        for k in range(1, N_DEV):
            kx, ky, kc = (k >> 2) & 1, (k >> 1) & 1, k & 1
            px = 1 - x if kx else x
            py = 1 - y if ky else y
            pc = 1 - c if kc else c
            copies.append(pltpu.make_async_remote_copy(
                src_ref=g_ref.at[4 * px + 2 * py + pc], dst_ref=out_ref.at[me], send_sem=send_sems.at[k - 1],
                recv_sem=recv_sems.at[k - 1], device_id=(px, py, pc), device_id_type=MESH_ID))
        for cp in copies:
            cp.start()
        for cp in copies:
            cp.wait_recv()
        for cp in copies:
            cp.wait_send()
        mine.wait()

    return pl.pallas_call(
        body, out_shape=jax.ShapeDtypeStruct(g.shape, g.dtype),
        in_specs=[pl.BlockSpec(memory_space=pl.ANY)], out_specs=pl.BlockSpec(memory_space=pl.ANY),
        scratch_shapes=[pltpu.SemaphoreType.DMA((7,)), pltpu.SemaphoreType.DMA((7,)), pltpu.SemaphoreType.DMA],
        name=name, compiler_params=pltpu.CompilerParams(has_side_effects=True))(g)


def sum_slots(name, parts, tr):
    n_slot, R, W = parts.shape

    def body(p_ref, o_ref):
        g = p_ref[0].astype(F32)
        for s in range(1, n_slot):
            g = g + p_ref[s].astype(F32)
        o_ref[...] = g

    return pl.pallas_call(
        body, out_shape=jax.ShapeDtypeStruct((R, W), F32), grid=(R // tr,),
        in_specs=[pl.BlockSpec((n_slot, tr, W), lambda i: (0, i, 0))], out_specs=pl.BlockSpec((tr, W), lambda i: (i, 0)),
        name=name, compiler_params=_cparams(("parallel",)))(parts)


def adamw(name, parts, w, m, v, tr):
    R, W = w.shape
    n_slot = parts.shape[0]
    c1 = 1.0 / (1.0 - ADAM_B1 ** ADAM_STEP)
    c2 = 1.0 / (1.0 - ADAM_B2 ** ADAM_STEP)

    def body(p_ref, w_ref, m_ref, v_ref, g_ref, d_ref, nm_ref, nv_ref):
        g = p_ref[0]
        for s in range(1, n_slot):
            g = g + p_ref[s]
        nm = ADAM_B1 * m_ref[...] + (1.0 - ADAM_B1) * g
        nv = ADAM_B2 * v_ref[...] + (1.0 - ADAM_B2) * (g * g)
        g_ref[...] = g
        nm_ref[...] = nm
        nv_ref[...] = nv
        d_ref[...] = -ADAM_LR * ((nm * c1) / (jnp.sqrt(nv * c2) + ADAM_EPS) + ADAM_WD * w_ref[...])

    spec = pl.BlockSpec((tr, W), lambda i: (i, 0))
    return pl.pallas_call(
        body, out_shape=[jax.ShapeDtypeStruct((R, W), F32)] * 4, grid=(R // tr,),
        in_specs=[pl.BlockSpec((n_slot, tr, W), lambda i: (0, i, 0)), spec, spec, spec], out_specs=[spec] * 4,
        name=name, compiler_params=_cparams(("parallel",)))(parts, w, m, v)


def _pack(arrs, row_mult):
    flat = jnp.concatenate([a.reshape(-1) for a in arrs])
    n = flat.shape[0]
    rows = -(-n // LANE)
    rows = -(-rows // row_mult) * row_mult
    return jnp.pad(flat, (0, rows * LANE - n)).reshape(rows, LANE)


def _unpack(buf, shapes, lead=()):
    flat = buf.reshape(lead + (-1,))
    out, off = [], 0
    for s in shapes:
        n = math.prod(s)
        out.append(flat[..., off:off + n].reshape(lead + tuple(s)))
        off += n
    return out


def _rows_sharded(name):
    return name in COL_T or BIG_AXIS[name] == 1


def _shard_for_gather(name, w):
    return jnp.swapaxes(w, 1, 2) if name in COL_T else w


def _full_from_gathered(name, g):
    if _rows_sharded(name):
        return jnp.transpose(g, (1, 0, 2, 3)).reshape(g.shape[1], N_DEV * g.shape[2], g.shape[3])
    return jnp.transpose(g, (1, 2, 0, 3)).reshape(g.shape[1], g.shape[2], N_DEV * g.shape[3])


def _pieces_from_full(name, f):
    A, B = f.shape
    if _rows_sharded(name):
        return f.reshape(N_DEV, -1)
    return jnp.transpose(f.reshape(A, N_DEV, B // N_DEV), (1, 0, 2)).reshape(N_DEV, -1)


def _piece_shape(name, shard_shape):
    L, a, b = shard_shape
    return (L, b, a) if name in COL_T else (L, a, b)


def _rows(v):
    return v.reshape(1, -1).astype(F32)


def _head_rows(W):
    bias = jnp.pad(W["dt_bias"].reshape(1, -1), ((0, 0), (0, LANE - 2 * N_HEADS)))
    alog = jnp.pad(W["a_log"].reshape(1, -1), ((0, 0), (0, LANE - 2 * N_HEADS)))
    return bias, alog


def layer_fwd(li, x, p_l, W, T):
    n = lambda s: f"l{li}_{s}"
    S = {"x": x}
    proj = matmul(n("mm_in"), x, W["w_in"], "nt", out_dtype=BF16)
    pdt = matmul(n("mm_dt"), x, W["w_in"][N_IN_PAD - LANE:], "nt")
    (u0,) = rowcall(n("glu"), glu_fn, [(proj, 1024, 0), (proj, 1024, 1)], [], [(1024, F32, None)], T)
    u1 = conv_fwd(n("conv_a"), u0, 0, W["conv_a_w"], _rows(W["conv_a_b"]), T)
    (u3,) = rowcall(n("lnsilu"), lnsilu_fn, [(u1, 1024, 0)], [_rows(W["ln_a_g"]), _rows(W["ln_a_b"])],
                    [(1024, BF16, None)], T)
    y_a = matmul(n("mm_aout"), u3, W["w_a_out"], "nn")
    xbc = conv_fwd(n("conv_s"), proj, 6144 // CONV_CB, W["ssm_conv_w"], _rows(W["ssm_conv_b"]), T)
    bias_row, alog_row = _head_rows(W)
    y_f, hs_f = ssd_fwd(n("ssd_f"), xbc, pdt, bias_row, alog_row, 0, T)
    y_b, hs_b = ssd_fwd(n("ssd_r"), xbc, pdt, bias_row, alog_row, 1, T)
    dsk = jnp.repeat(W["d_skip"], HEAD_DIM).reshape(1, D_INNER)
    (yn,) = rowcall(n("gnorm"), gnorm_fn, [(y_f, 256, 0), (y_b, 256, 0), (xbc, 256, 0), (proj, 256, 16)],
                    [dsk, _rows(W["ssm_norm_g"])], [(256, BF16, None)], T, groups=N_GROUPS)
    y_bo = matmul(n("mm_bout"), yn, W["w_b_out"], "nn")
    (merged,) = rowcall(n("merge"), merge_fn, [(proj, 1024, 2), (proj, 1024, 3), (y_a, 1024, 0), (y_bo, 1024, 0)], [],
                        [(1024, BF16, None)], T)
    mix = matmul(n("mm_o"), merged, W["w_o"], "nn")
    (h,) = rowcall(n("ln1"), resln_fn, [(x, 1024, 0), (mix, 1024, 0)], [_rows(W["ln1_g"]), _rows(W["ln1_b"])],
                   [(1024, F32, None)], T)
    gu = matmul(n("mm_gu"), h, W["w_gate_up"], "nt", out_dtype=BF16)
    (act,) = rowcall(n("swiglu"), swiglu_fn, [(gu, FFN_DIM, 0), (gu, FFN_DIM, 1)], [], [(FFN_DIM, BF16, None)], T)
    dn = matmul(n("mm_down"), act, W["w_down"], "nn")
    (h2,) = rowcall(n("ln2"), resln_fn, [(h, 1024, 0), (dn, 1024, 0)], [_rows(W["ln2_g"]), _rows(W["ln2_b"])],
                    [(1024, F32, None)], T)
    pe = matmul(n("mm_ple"), p_l, W["w_ple"], "nn")
    gl = matmul(n("mm_pg"), h2, W["w_ple_gate"], "nn")
    (xn,) = rowcall(n("pleout"), ple_fn, [(h2, 1024, 0), (pe, 1024, 0), (gl, 1024, 0)], [_rows(W["ple_norm_g"])],
                    [(1024, F32, None)], T)
    S.update(proj=proj, pdt=pdt, u0=u0, u1=u1, u3=u3, y_a=y_a, xbc=xbc, y_f=y_f, y_b=y_b, hs_f=hs_f, hs_b=hs_b, yn=yn, y_bo=y_bo,
             merged=merged, mix=mix, h=h, gu=gu, act=act, dn=dn, h2=h2, pe=pe, gl=gl, dsk=dsk, bias_row=bias_row,
             alog_row=alog_row)
    return xn, S


def layer_bwd(li, dxn, p_l, W, S, T):
    n = lambda s: f"l{li}_{s}"
    G = {}
    x, proj = S["x"], S["proj"]
    (dh2a, dpe, dgl), (dpg,) = rowvjp(
        n("pleout_b"), ple_fn, [(S["h2"], 1024, 0), (S["pe"], 1024, 0), (S["gl"], 1024, 0)], [_rows(W["ple_norm_g"])],
        [(dxn, 1024, 0)], [([0], F32, None), ([1], BF16, None), ([2], BF16, None)], T)
    G["ple_norm_g"] = dpg
    G["w_ple_gate"] = matmul(n("mm_pg_w"), S["h2"], dgl, "tn", out_dtype=BF16)
    G["w_ple"] = matmul(n("mm_ple_w"), p_l, dpe, "tn", out_dtype=BF16)
    dh2 = matmul(n("mm_pg_x"), dgl, W["w_ple_gate"], "nt", add=dh2a)
    (dha, ddn), (G["ln2_g"], G["ln2_b"]) = rowvjp(
        n("ln2_b"), resln_fn, [(S["h"], 1024, 0), (S["dn"], 1024, 0)], [_rows(W["ln2_g"]), _rows(W["ln2_b"])],
        [(dh2, 1024, 0)], [([0], F32, None), ([1], BF16, None)], T)
    G["w_down"] = matmul(n("mm_down_w"), S["act"], ddn, "tn", out_dtype=BF16)
    dact = matmul(n("mm_down_x"), ddn, W["w_down"], "nt", out_dtype=BF16)
    (dgu,), _ = rowvjp(n("swiglu_b"), swiglu_fn, [(S["gu"], FFN_DIM, 0), (S["gu"], FFN_DIM, 1)], [],
                       [(dact, FFN_DIM, 0)], [([0, 1], BF16, None)], T)
    G["w_gate_up"] = matmul(n("mm_gu_w"), dgu, S["h"], "tn", out_dtype=BF16)
    dh = matmul(n("mm_gu_x"), dgu, W["w_gate_up"], "nn", add=dha)
    (dxa, dmix), (G["ln1_g"], G["ln1_b"]) = rowvjp(
        n("ln1_b"), resln_fn, [(x, 1024, 0), (S["mix"], 1024, 0)], [_rows(W["ln1_g"]), _rows(W["ln1_b"])],
        [(dh, 1024, 0)], [([0], F32, None), ([1], BF16, None)], T)
    G["w_o"] = matmul(n("mm_o_w"), S["merged"], dmix, "tn", out_dtype=BF16)
    dmerged = matmul(n("mm_o_x"), dmix, W["w_o"], "nt")
    dproj = jax.ShapeDtypeStruct((T, N_IN_PAD), BF16)
    (dproj, dy_a, dy_bo), _ = rowvjp(
        n("merge_b"), merge_fn, [(proj, 1024, 2), (proj, 1024, 3), (S["y_a"], 1024, 0), (S["y_bo"], 1024, 0)], [],
        [(dmerged, 1024, 0)], [([0, 1], BF16, (dproj, 1)), ([2], BF16, None), ([3], BF16, None)], T)
    G["w_a_out"] = matmul(n("mm_aout_w"), S["u3"], dy_a, "tn", out_dtype=BF16)
    du3 = matmul(n("mm_aout_x"), dy_a, W["w_a_out"], "nt")
    (du1,), (G["ln_a_g"], G["ln_a_b"]) = rowvjp(
        n("lnsilu_b"), lnsilu_fn, [(S["u1"], 1024, 0)], [_rows(W["ln_a_g"]), _rows(W["ln_a_b"])], [(du3, 1024, 0)],
        [([0], F32, None)], T)
    du0, G["conv_a_w"], G["conv_a_b"] = conv_bwd(n("conv_a_b"), du1, S["u0"], 0, W["conv_a_w"], T)
    (dproj,), _ = rowvjp(n("glu_b"), glu_fn, [(proj, 1024, 0), (proj, 1024, 1)], [], [(du0, 1024, 0)],
                         [([0, 1], BF16, (dproj, 0))], T)
    G["w_b_out"] = matmul(n("mm_bout_w"), S["yn"], dy_bo, "tn", out_dtype=BF16)
    dyn = matmul(n("mm_bout_x"), dy_bo, W["w_b_out"], "nt")
    (dys, dxs, dproj), (ddsk, G["ssm_norm_g"]) = rowvjp(
        n("gnorm_b"), gnorm_fn, [(S["y_f"], 256, 0), (S["y_b"], 256, 0), (S["xbc"], 256, 0), (proj, 256, 16)],
        [S["dsk"], _rows(W["ssm_norm_g"])], [(dyn, 256, 0)],
        [([0], F32, None), ([2], F32, None), ([3], BF16, (dproj, 16))], T, groups=N_GROUPS)
    G["d_skip"] = ddsk.reshape(N_HEADS, HEAD_DIM).sum(axis=1)
    dx1, db1, dc1, ddt1, dbias_f, dalog_f = ssd_bwd(
        n("ssd_f_b"), S["xbc"], S["pdt"], S["bias_row"], S["alog_row"], S["hs_f"], dys, (dxs,), 0, T)
    dxx, dbb, dcc, ddt, dbias_r, dalog_r = ssd_bwd(
        n("ssd_r_b"), S["xbc"], S["pdt"], S["bias_row"], S["alog_row"], S["hs_b"], dys, (dx1, db1, dc1, ddt1), 1, T)
    G["dt_bias"] = (dbias_f + dbias_r)[0, :2 * N_HEADS].reshape(2, N_HEADS)
    G["a_log"] = (dalog_f + dalog_r)[0, :2 * N_HEADS].reshape(2, N_HEADS)
    cw = W["ssm_conv_w"]
    b0 = 6144 // CONV_CB
    dproj, dwx, dbx = conv_bwd(n("conv_sx_b"), dxx, proj, b0, cw[:, :D_INNER], T, into=(dproj, b0))
    dproj, dwb, dbb_ = conv_bwd(n("conv_sb_b"), dbb, proj, b0 + 4, cw[:, D_INNER:D_INNER + 1024], T, into=(dproj, b0 + 4))
    dproj, dwc, dbc = conv_bwd(n("conv_sc_b"), dcc, proj, b0 + 6, cw[:, D_INNER + 1024:], T, into=(dproj, b0 + 6))
    G["ssm_conv_w"] = jnp.concatenate([dwx, dwb, dwc], axis=1)
    G["ssm_conv_b"] = jnp.concatenate([dbx, dbb_, dbc], axis=1)
    (dproj,) = rowcall(n("dt_cast"), ident_fn, [(ddt, LANE, 0)], [], [(LANE, BF16, (dproj, (N_IN_PAD - LANE) // LANE))], T)
    G["w_in"] = matmul(n("mm_in_w"), dproj, x, "tn", out_dtype=BF16)[:N_IN]
    dx = matmul(n("mm_in_x"), dproj, W["w_in"], "nn", add=dxa)
    return dx, G


def local_step(x, p, loss_target, FW, T):
    Ws, saves = [], []
    cur = x
    for li in range(DEPTH):
        W = {k: v[li] for k, v in FW.items()}
        Ws.append(W)
        cur, S = layer_fwd(li, cur, p[li], W, T)
        saves.append(S)
    dcur, sq = loss_head(cur, loss_target, T)
    loss = 0.5 * jnp.sum(sq) / D_MODEL
    grads = [None] * DEPTH
    for li in reversed(range(DEPTH)):
        dcur, grads[li] = layer_bwd(li, dcur, p[li], Ws[li], saves[li], T)
    return loss, dcur, grads


def kernel(x, p, w_in, conv_a_w, conv_a_b, ln_a_g, ln_a_b, w_a_out, ssm_conv_w, ssm_conv_b, a_log, dt_bias, d_skip, ssm_norm_g, w_b_out, w_o, ln1_g, ln1_b, w_gate_up, w_down, ln2_g, ln2_b, w_ple, ple_norm_g, w_ple_gate, loss_target, m_w_in, m_conv_a_w, m_conv_a_b, m_ln_a_g, m_ln_a_b, m_w_a_out, m_ssm_conv_w, m_ssm_conv_b, m_a_log, m_dt_bias, m_d_skip, m_ssm_norm_g, m_w_b_out, m_w_o, m_ln1_g, m_ln1_b, m_w_gate_up, m_w_down, m_ln2_g, m_ln2_b, m_w_ple, m_ple_norm_g, m_w_ple_gate, v_w_in, v_conv_a_w, v_conv_a_b, v_ln_a_g, v_ln_a_b, v_w_a_out, v_ssm_conv_w, v_ssm_conv_b, v_a_log, v_dt_bias, v_d_skip, v_ssm_norm_g, v_w_b_out, v_w_o, v_ln1_g, v_ln1_b, v_w_gate_up, v_w_down, v_ln2_g, v_ln2_b, v_w_ple, v_ple_norm_g, v_w_ple_gate):
    A = dict(locals())
    w = {k: A[k] for k in WEIGHTS}
    m = {k: A["m_" + k] for k in WEIGHTS}
    v = {k: A["v_" + k] for k in WEIGHTS}
    T = x.shape[1]
    small_shapes = [w[k].shape for k in SMALL]

    mm_names = [k for k in BIG if k not in CONV_W]
    shards = [_shard_for_gather(k, w[k].astype(BF16)) for k in mm_names]
    gathered = all_gather("gather_weights", _pack(shards, 16))
    FW = {k: _full_from_gathered(k, g) for k, g in zip(mm_names, _unpack(gathered, [t.shape for t in shards], (N_DEV,)))}
    FW["w_in"] = jnp.pad(FW["w_in"], ((0, 0), (0, N_IN_PAD - N_IN), (0, 0)))
    gathered = all_gather("gather_conv_weights", _pack([w[k] for k in CONV_W], SUBLANE))
    FW.update({k: _full_from_gathered(k, g)
               for k, g in zip(CONV_W, _unpack(gathered, [w[k].shape for k in CONV_W], (N_DEV,)))})
    FW.update({k: w[k] for k in SMALL})

    loss, grad_x, gfull = local_step(x[0], p[:, 0], loss_target[0], FW, T)
    loss = lax.psum(loss, ("x", "y", "c"))

    TR = 512
    flat = jnp.concatenate([_pieces_from_full(k, gfull[li][k]).astype(BF16) for k in BIG for li in range(DEPTH)], axis=1)
    rows = -(-flat.shape[1] // (LANE * TR)) * TR
    gpack = jnp.pad(flat, ((0, 0), (0, rows * LANE - flat.shape[1]))).reshape(N_DEV, rows, LANE)
    landed = grad_exchange("grad_exchange", gpack)
    gsum = sum_slots("sum_grad_pieces", landed, TR)
    gshard = [jnp.swapaxes(g, 1, 2) if k in COL_T else g
              for k, g in zip(BIG, _unpack(gsum, [_piece_shape(k, w[k].shape) for k in BIG]))]
    res_big = [{}, {}, {}, {}]
    for k, g in zip(BIG, gshard):
        two_d = lambda t: t.reshape(-1, t.shape[-1])
        rows_k = two_d(w[k]).shape[0]
        tr = max(d for d in range(1, min(rows_k, 256) + 1) if rows_k % d == 0 and (d % SUBLANE == 0 or d == rows_k))
        res = adamw("adamw_" + k, two_d(g)[None], two_d(w[k]), two_d(m[k]), two_d(v[k]), tr)
        for q in range(4):
            res_big[q][k] = res[q].reshape(w[k].shape)

    spack = _pack([jnp.stack([gfull[li][k] for li in range(DEPTH)]).reshape(w[k].shape) for k in SMALL], SUBLANE)
    sall = all_gather("gather_small_grads", spack)
    rs = spack.shape[0]
    res_small = adamw("adamw_small", sall, _pack([w[k] for k in SMALL], SUBLANE), _pack([m[k] for k in SMALL], SUBLANE),
                      _pack([v[k] for k in SMALL], SUBLANE), rs)
    res_small = [dict(zip(SMALL, _unpack(r, small_shapes))) for r in res_small]

    outs = [loss, grad_x[None]]
    for q in range(4):
        for k in WEIGHTS:
            outs.append(res_big[q][k] if k in res_big[q] else res_small[q][k])
    return tuple(outs)
```

```python
import math

import jax
import jax.numpy as jnp
from jax import lax
from jax.experimental import pallas as pl
from jax.experimental.pallas import tpu as pltpu

F32 = jnp.float32
BF16 = jnp.bfloat16

D_MODEL = 1024
CONV_DIM = 1024
CONV_KERNEL = 31
D_INNER = 2048
HEAD_DIM = 64
N_HEADS = 32
N_GROUPS = 8
D_STATE = 128
SSM_CONV = 5
CHUNK = 128
XBC_DIM = D_INNER + 2 * N_GROUPS * D_STATE
FFN_DIM = 2816
PLE_DIM = 256
N_IN = 2 * CONV_DIM + 2 * D_MODEL + D_INNER + XBC_DIM + 2 * N_HEADS
N_IN_PAD = 10368
DEPTH = 2
N_DEV = 8
ALPHA = (2 * DEPTH) ** 0.25
LN_EPS = 1e-5
RMS_EPS = 1e-6
ADAM_LR, ADAM_B1, ADAM_B2, ADAM_EPS, ADAM_WD, ADAM_STEP = 0.001, 0.9, 0.999, 1e-08, 0.01, 10

LANE = 128
SUBLANE = 8
HALO = 16
VMEM_LIMIT = 52 * 1024 * 1024
NEG = -1e30

BIG = ["w_in", "conv_a_w", "w_a_out", "ssm_conv_w", "w_b_out", "w_o", "w_gate_up", "w_down", "w_ple", "w_ple_gate"]
BIG_AXIS = {"w_in": 2, "conv_a_w": 2, "w_a_out": 1, "ssm_conv_w": 2, "w_b_out": 1, "w_o": 1, "w_gate_up": 2,
            "w_down": 1, "w_ple": 2, "w_ple_gate": 1}
COL_T = ["w_in", "w_gate_up"]
CONV_W = ["conv_a_w", "ssm_conv_w"]
SMALL = ["conv_a_b", "ln_a_g", "ln_a_b", "ssm_conv_b", "a_log", "dt_bias", "d_skip", "ssm_norm_g", "ln1_g", "ln1_b",
         "ln2_g", "ln2_b", "ple_norm_g"]
WEIGHTS = ["w_in", "conv_a_w", "conv_a_b", "ln_a_g", "ln_a_b", "w_a_out", "ssm_conv_w", "ssm_conv_b", "a_log", "dt_bias",
           "d_skip", "ssm_norm_g", "w_b_out", "w_o", "ln1_g", "ln1_b", "w_gate_up", "w_down", "ln2_g", "ln2_b", "w_ple",
           "ple_norm_g", "w_ple_gate"]


def _cparams(sem):
    return pltpu.CompilerParams(dimension_semantics=sem, vmem_limit_bytes=VMEM_LIMIT)


def _pick(n, cap):
    if n <= cap:
        return n
    best = None
    for d in range(LANE, cap + 1, LANE):
        if n % d == 0:
            best = d
    assert best is not None, (n, cap)
    return best


def matmul(name, a, b, mode, out_dtype=F32, add=None):
    if mode == "nn":
        (M, K), (K2, N) = a.shape, b.shape
    elif mode == "nt":
        (M, K), (N, K2) = a.shape, b.shape
    else:
        (K, M), (K2, N) = a.shape, b.shape
    assert K == K2, (name, a.shape, b.shape)
    tm = _pick(M, 1024) if mode != "tn" else _pick(M, 1408)
    tn = _pick(N, 1408)
    tk = _pick(K, 512) if mode == "tn" else (K if (mode == "nn" and K <= 2816) else _pick(K, 1408))
    nk = K // tk
    grid = (M // tm, N // tn, nk)
    if mode == "tn":
        a_spec = pl.BlockSpec((tk, tm), lambda i, j, k: (k, i))
    else:
        a_spec = pl.BlockSpec((tm, tk), lambda i, j, k: (i, k))
    if mode == "nt":
        b_spec = pl.BlockSpec((tn, tk), lambda i, j, k: (j, k))
    else:
        b_spec = pl.BlockSpec((tk, tn), lambda i, j, k: (k, j))
    o_spec = pl.BlockSpec((tm, tn), lambda i, j, k: (i, j))
    dims = {"nn": ((1,), (0,)), "nt": ((1,), (1,)), "tn": ((0,), (0,))}[mode]
    has_add = add is not None

    def body(a_ref, b_ref, *rest):
        if has_add:
            add_ref, o_ref, *scr = rest
        else:
            o_ref, *scr = rest
        part = lax.dot_general(a_ref[...].astype(BF16), b_ref[...].astype(BF16), (dims, ((), ())),
                               preferred_element_type=F32)

        def finish(v):
            if has_add:
                v = v + add_ref[...].astype(F32)
            o_ref[...] = v.astype(o_ref.dtype)

        if nk == 1:
            finish(part)
        else:
            acc = scr[0]
            k = pl.program_id(2)

            @pl.when(k == 0)
            def _():
                acc[...] = part

            @pl.when(k > 0)
            def _():
                acc[...] += part

            @pl.when(k == nk - 1)
            def _():
                finish(acc[...])

    in_specs = [a_spec, b_spec] + ([o_spec] if has_add else [])
    args = (a, b) + ((add,) if has_add else ())
    return pl.pallas_call(
        body, out_shape=jax.ShapeDtypeStruct((M, N), out_dtype), grid=grid, in_specs=in_specs, out_specs=o_spec,
        scratch_shapes=[pltpu.VMEM((tm, tn), F32)] if nk > 1 else [], name=name,
        compiler_params=_cparams(("parallel", "parallel", "arbitrary")))(*args)


def _row_specs(items, tT, groups):
    specs = []
    for (_, w, blk) in items:
        assert blk % groups == 0
        specs.append(pl.BlockSpec((tT, w * groups), (lambda i, b=blk // groups: (i, b))))
    return specs


def _slices(v, groups):
    if groups == 1:
        return [v]
    w = v.shape[1] // groups
    return [v[:, w * s:w * (s + 1)] for s in range(groups)]


def _cat(vs):
    return vs[0] if len(vs) == 1 else jnp.concatenate(vs, axis=1)


def rowcall(name, fn, ins, pars, outs, T, tT=256, groups=1):
    n_in, n_par = len(ins), len(pars)
    intos = [o[2] for o in outs if o[2] is not None]
    in_specs = (_row_specs(ins, tT, groups) + [pl.BlockSpec(p.shape, lambda i: (0, 0)) for p in pars]
                + [pl.BlockSpec(memory_space=pl.ANY)] * len(intos))
    out_specs, out_shapes, aliases = [], [], {}
    n_alias = 0
    for oi, (w, dt, into) in enumerate(outs):
        if into is None:
            out_specs.append(pl.BlockSpec((tT, w * groups), lambda i: (i, 0)))
            out_shapes.append(jax.ShapeDtypeStruct((T, w * groups), dt))
        else:
            arr, blk = into
            out_specs.append(pl.BlockSpec((tT, w * groups), lambda i, b=blk // groups: (i, b)))
            out_shapes.append(jax.ShapeDtypeStruct(arr.shape, arr.dtype))
            aliases[n_in + n_par + n_alias] = oi
            n_alias += 1

    def body(*refs):
        xs = [_slices(r[...].astype(F32), groups) for r in refs[:n_in]]
        ps = [_slices(r[...], groups) for r in refs[n_in:n_in + n_par]]
        o_refs = refs[n_in + n_par + n_alias:]
        res = [fn(*[x[s] for x in xs], *[p[s] for p in ps]) for s in range(groups)]
        for k, r in enumerate(o_refs):
            r[...] = _cat([res[s][k] for s in range(groups)]).astype(r.dtype)

    res = pl.pallas_call(
        body, out_shape=out_shapes, grid=(T // tT,), in_specs=in_specs, out_specs=out_specs,
        input_output_aliases=aliases, name=name, compiler_params=_cparams(("parallel",)))(
            *[a for (a, _, _) in ins], *pars, *[a for (a, _) in intos])
    return list(res)


def rowvjp(name, fn, ins, pars, cts, douts, T, tT=256, groups=1):
    n_in, n_par, n_ct = len(ins), len(pars), len(cts)
    intos = [o[2] for o in douts if o[2] is not None and not isinstance(o[2][0], jax.ShapeDtypeStruct)]
    in_specs = (_row_specs(ins, tT, groups) + [pl.BlockSpec(p.shape, lambda i: (0, 0)) for p in pars]
                + _row_specs(cts, tT, groups) + [pl.BlockSpec(memory_space=pl.ANY)] * len(intos))
    out_specs, out_shapes, aliases = [], [], {}
    n_alias = 0
    for oi, (idxs, dt, into) in enumerate(douts):
        w = sum(ins[k][1] for k in idxs) * groups
        if into is None:
            out_specs.append(pl.BlockSpec((tT, w), lambda i: (i, 0)))
            out_shapes.append(jax.ShapeDtypeStruct((T, w), dt))
        else:
            assert len(idxs) == 1 or groups == 1
            arr, blk = into
            out_specs.append(pl.BlockSpec((tT, w), lambda i, b=blk // groups: (i, b)))
            out_shapes.append(jax.ShapeDtypeStruct(arr.shape, arr.dtype))
            if not isinstance(arr, jax.ShapeDtypeStruct):
                aliases[n_in + n_par + n_ct + n_alias] = oi
                n_alias += 1
    n_dout = len(douts)
    for p in pars:
        out_specs.append(pl.BlockSpec(p.shape, lambda i: (0, 0)))
        out_shapes.append(jax.ShapeDtypeStruct(p.shape, F32))

    def body(*refs):
        xs = [_slices(r[...].astype(F32), groups) for r in refs[:n_in]]
        ps = [_slices(r[...], groups) for r in refs[n_in:n_in + n_par]]
        cs = [_slices(r[...].astype(F32), groups) for r in refs[n_in + n_par:n_in + n_par + n_ct]]
        o_refs = refs[n_in + n_par + n_ct + n_alias:]
        grads = []
        for s in range(groups):
            _, vjp_fn = jax.vjp(fn, *[x[s] for x in xs], *[p[s] for p in ps])
            grads.append(vjp_fn(tuple(c[s] for c in cs)))
        for r, (idxs, _, _) in zip(o_refs[:n_dout], douts):
            r[...] = _cat([grads[s][k] for k in idxs for s in range(groups)]).astype(r.dtype)
        for k, r in enumerate(o_refs[n_dout:]):
            @pl.when(pl.program_id(0) == 0)
            def _(r=r):
                r[...] = jnp.zeros(r.shape, F32)
            r[...] += _cat([grads[s][n_in + k] for s in range(groups)])

    res = pl.pallas_call(
        body, out_shape=out_shapes, grid=(T // tT,), in_specs=in_specs, out_specs=out_specs,
        input_output_aliases=aliases, name=name, compiler_params=_cparams(("arbitrary",)))(
            *[a for (a, _, _) in ins], *pars, *[a for (a, _, _) in cts], *[a for (a, _) in intos])
    res = list(res)
    return res[:n_dout], res[n_dout:]


def _sigmoid(x):
    return 1.0 / (1.0 + jnp.exp(-x))


def _silu(x):
    return x * _sigmoid(x)


def _softplus(x):
    return jnp.maximum(x, 0.0) + jnp.log(1.0 + jnp.exp(-jnp.abs(x)))


def _ln(x, g, b):
    mu = jnp.mean(x, axis=-1, keepdims=True)
    xc = x - mu
    var = jnp.mean(xc * xc, axis=-1, keepdims=True)
    return xc * lax.rsqrt(var + LN_EPS) * g + b


def glu_fn(a, gt):
    return (a * _sigmoid(gt),)


def lnsilu_fn(u, g, b):
    return (_silu(_ln(u, g, b)),)


def gnorm_fn(yf, yb, xp, z, dsk, ng):
    y = (yf + yb + _silu(xp) * dsk) * _silu(z)
    return (y * lax.rsqrt(jnp.mean(y * y, axis=-1, keepdims=True) + RMS_EPS) * ng,)


def merge_fn(ga, gb, ya, yb):
    return (_sigmoid(ga) * ya + _sigmoid(gb) * yb,)


def resln_fn(x, r, g, b):
    return (_ln(ALPHA * x + r, g, b),)


def swiglu_fn(g, u):
    return (_silu(g) * u,)


def ple_fn(h2, pe, gl, g):
    e = pe * lax.rsqrt(jnp.mean(pe * pe, axis=-1, keepdims=True) + RMS_EPS) * g
    return (h2 + e * _sigmoid(gl),)


def ident_fn(v):
    return (v,)


CONV_CB = 512
CONV_TT = 512
CONV_TILES = 4
CONV_RB = CONV_TILES * SUBLANE
CONV_RED_TILES = 2
CONV_STATIC_MAX_K = 8


def _conv_specs(blk0, T, tT, cb):
    nh = tT // HALO
    cur = pl.BlockSpec((tT, cb), lambda j, i: (i, blk0 + j))
    prev = pl.BlockSpec((HALO, cb), lambda j, i: (jnp.maximum(i * nh - 1, 0), blk0 + j))
    nxt = pl.BlockSpec((HALO, cb), lambda j, i: (jnp.minimum((i + 1) * nh, T // HALO - 1), blk0 + j))
    return [prev, cur, nxt]


def _phases(offsets):
    return sorted({off % SUBLANE for off in offsets})


def _fill_padded(pad_ref, prev_ref, cur_ref, next_ref, i, n_t, tT):
    pad_ref[pl.ds(0, HALO), :] = prev_ref[...].astype(F32) * (i > 0).astype(F32)
    pad_ref[pl.ds(HALO, tT), :] = cur_ref[...].astype(F32)
    pad_ref[pl.ds(HALO + tT, HALO), :] = next_ref[...].astype(F32) * (i < n_t - 1).astype(F32)


def _fill_shifted(sh_ref, pad_ref, phases, tT):
    for ph in phases:
        sh_ref[ph] = pad_ref[pl.ds(ph, tT + 3 * SUBLANE), :]


class _Shifted:
    def __init__(self, pad_ref, sh_ref, offsets, tT, static):
        self.pad_ref, self.sh_ref, self.static = pad_ref, sh_ref, static
        if not static:
            _fill_shifted(sh_ref, pad_ref, _phases(offsets), tT)

    def tiles(self, base, ls, off, n_tiles):
        if self.static:
            return tuple(self.pad_ref[pl.ds(base + off + SUBLANE * t, SUBLANE), ls] for t in range(n_tiles))
        q, ph = divmod(off, SUBLANE)
        return tuple(self.sh_ref[ph, pl.ds(base + SUBLANE * (q + t), SUBLANE), ls] for t in range(n_tiles))


def _row_loop(static, n, body, init):
    if not static:
        return lax.fori_loop(0, n, body, init)
    carry = init
    for r in range(n):
        carry = body(r, carry)
    return carry


def _conv_rows(src, w_ref, bias, o_ref, offsets, tT, cb):
    K = len(offsets)
    for lt in range(cb // LANE):
        ls = slice(LANE * lt, LANE * (lt + 1))
        wv = [jnp.broadcast_to(w_ref[k:k + 1, ls], (SUBLANE, LANE)) for k in range(K)]
        b0 = jnp.zeros((SUBLANE, LANE), F32) if bias is None else jnp.broadcast_to(bias[:, ls], (SUBLANE, LANE))

        def rows(r, carry, ls=ls, wv=wv, b0=b0):
            base = r * CONV_RB if src.static else pl.multiple_of(r * CONV_RB, CONV_RB)
            accs = [b0] * CONV_TILES
            for k, off in enumerate(offsets):
                accs = [a + d * wv[k] for a, d in zip(accs, src.tiles(base, ls, off, CONV_TILES))]
            o_ref[pl.ds(base, CONV_RB), ls] = jnp.concatenate(accs, axis=0).astype(o_ref.dtype)
            return carry

        _row_loop(src.static, tT // CONV_RB, rows, 0)


def conv_fwd(name, u, blk0, w, b, T):
    K, C = w.shape
    P = (K - 1) // 2
    tT, cb = min(CONV_TT, T), CONV_CB
    n_t = T // tT

    static = K <= CONV_STATIC_MAX_K

    def body(prev_ref, cur_ref, next_ref, w_ref, b_ref, o_ref, pad_ref, sh_ref):
        offsets = [HALO - P + k for k in range(K)]
        _fill_padded(pad_ref, prev_ref, cur_ref, next_ref, pl.program_id(1), n_t, tT)
        _conv_rows(_Shifted(pad_ref, sh_ref, offsets, tT, static), w_ref, b_ref[...], o_ref, offsets, tT, cb)

    return pl.pallas_call(
        body, out_shape=jax.ShapeDtypeStruct((T, C), F32), grid=(C // cb, n_t),
        in_specs=_conv_specs(blk0, T, tT, cb) + [pl.BlockSpec((K, cb), lambda j, i: (0, j)),
                                                  pl.BlockSpec((1, cb), lambda j, i: (0, j))],
        out_specs=pl.BlockSpec((tT, cb), lambda j, i: (i, j)),
        scratch_shapes=[pltpu.VMEM((tT + 2 * HALO, cb), F32), pltpu.VMEM((SUBLANE, tT + 3 * SUBLANE, cb), F32)],
        name=name, compiler_params=_cparams(("parallel", "arbitrary")))(u, u, u, w, b)


def conv_bwd(name, dy, u, blk0, w, T, into=None):
    K, C = w.shape
    P = (K - 1) // 2
    tT, cb = min(CONV_TT, T), CONV_CB
    n_t = T // tT

    static = K <= CONV_STATIC_MAX_K

    def body(dprev, dcur, dnext, uprev, ucur, unext, w_ref, *rest):
        if into is not None:
            rest = rest[1:]
        du_ref, dw_ref, db_ref, padd_ref, padu_ref, shd_ref, shu_ref = rest
        i = pl.program_id(1)
        offsets = [HALO - P + k for k in range(K)]
        back = [HALO + P - k for k in range(K)]
        _fill_padded(padd_ref, dprev, dcur, dnext, i, n_t, tT)
        _fill_padded(padu_ref, uprev, ucur, unext, i, n_t, tT)
        src_d = _Shifted(padd_ref, shd_ref, back + [HALO], tT, static)
        src_u = _Shifted(padu_ref, shu_ref, offsets, tT, static)
        _conv_rows(src_d, w_ref, None, du_ref, back, tT, cb)

        @pl.when(i == 0)
        def _():
            dw_ref[...] = jnp.zeros(dw_ref.shape, F32)
            db_ref[...] = jnp.zeros(db_ref.shape, F32)

        rb = CONV_RED_TILES * SUBLANE
        for lt in range(cb // LANE):
            ls = slice(LANE * lt, LANE * (lt + 1))

            def red(r, accs, ls=ls):
                base = r * rb if static else pl.multiple_of(r * rb, rb)
                d0, d1 = src_d.tiles(base, ls, HALO, CONV_RED_TILES)
                new = []
                for acc, off in zip(accs[:K], offsets):
                    u0, u1 = src_u.tiles(base, ls, off, CONV_RED_TILES)
                    new.append(acc + d0 * u0 + d1 * u1)
                return tuple(new) + (accs[K] + d0 + d1,)

            zero = jnp.zeros((SUBLANE, LANE), F32)
            accs = _row_loop(static, tT // rb, red, (zero,) * (K + 1))
            for k in range(K):
                dw_ref[k:k + 1, ls] += jnp.sum(accs[k], axis=0, keepdims=True)
            db_ref[:, ls] += jnp.sum(accs[K], axis=0, keepdims=True)

    dspecs = _conv_specs(0, T, tT, cb)
    uspecs = _conv_specs(blk0, T, tT, cb)
    in_specs = dspecs + uspecs + [pl.BlockSpec((K, cb), lambda j, i: (0, j))]
    args = [dy, dy, dy, u, u, u, w]
    aliases = {}
    if into is None:
        du_spec = pl.BlockSpec((tT, cb), lambda j, i: (i, j))
        du_shape = jax.ShapeDtypeStruct((T, C), F32)
    else:
        arr, oblk = into
        in_specs.append(pl.BlockSpec(memory_space=pl.ANY))
        args.append(arr)
        aliases = {7: 0}
        du_spec = pl.BlockSpec((tT, cb), lambda j, i: (i, oblk + j))
        du_shape = jax.ShapeDtypeStruct(arr.shape, arr.dtype)
    return pl.pallas_call(
        body, out_shape=[du_shape, jax.ShapeDtypeStruct((K, C), F32), jax.ShapeDtypeStruct((1, C), F32)],
        grid=(C // cb, n_t), in_specs=in_specs,
        out_specs=[du_spec, pl.BlockSpec((K, cb), lambda j, i: (0, j)), pl.BlockSpec((1, cb), lambda j, i: (0, j))],
        scratch_shapes=[pltpu.VMEM((tT + 2 * HALO, cb), F32), pltpu.VMEM((tT + 2 * HALO, cb), F32),
                        pltpu.VMEM((SUBLANE, tT + 3 * SUBLANE, cb), F32), pltpu.VMEM((SUBLANE, tT + 3 * SUBLANE, cb), F32)],
        input_output_aliases=aliases, name=name, compiler_params=_cparams(("arbitrary", "arbitrary")))(*args)


def _dot(a, b, dims):
    return lax.dot_general(a.astype(BF16), b.astype(BF16), (dims, ((), ())), preferred_element_type=F32)


def _dnn(a, b):
    return _dot(a, b, ((1,), (0,)))


def _dnt(a, b):
    return _dot(a, b, ((1,), (1,)))


def _dtn(a, b):
    return _dot(a.T, b, ((1,), (0,)))


@jax.custom_vjp
def _nn(a, b):
    return _dnn(a, b)


_nn.defvjp(lambda a, b: (_dnn(a, b), (a, b)), lambda r, g: (_dnt(g, r[1]), _dtn(r[0], g)))


@jax.custom_vjp
def _nt(a, b):
    return _dnt(a, b)


_nt.defvjp(lambda a, b: (_dnt(a, b), (a, b)), lambda r, g: (_dnn(g, r[1]), _dtn(g, r[0])))


@jax.custom_vjp
def _tn(a, b):
    return _dtn(a, b)


_tn.defvjp(lambda a, b: (_dtn(a, b), (a, b)), lambda r, g: (_dnt(r[1], g), _dnn(r[0], g)))


def _split_dot(m, v):
    hi = v.astype(BF16)
    r1 = v - hi.astype(F32)
    mid = r1.astype(BF16)
    lo = (r1 - mid.astype(F32)).astype(BF16)
    mb = m.astype(BF16)
    d = lambda x: lax.dot_general(mb, x, (((1,), (0,)), ((), ())), preferred_element_type=F32)
    return d(hi) + d(mid) + d(lo)


@jax.custom_vjp
def _tri_dot(tri, tri_t, v):
    return _split_dot(tri, v)


_tri_dot.defvjp(lambda tri, tri_t, v: (_split_dot(tri, v), (tri, tri_t)),
                lambda r, g: (jnp.zeros_like(r[0]), jnp.zeros_like(r[1]), _split_dot(r[1], g)))


def _pick_vjp(axis):
    def pick(v, h):
        return v[:, h:h + 1] if axis == 1 else v[h:h + 1, :]

    def fwd(v, h):
        return pick(v, h), v.shape

    def bwd(h, shape, g):
        idx = lax.broadcasted_iota(jnp.int32, shape, axis)
        return (jnp.where(idx == h, g, 0.0),)

    f = jax.custom_vjp(pick, nondiff_argnums=(1,))
    f.defvjp(fwd, bwd)
    return f


_lane_pick = _pick_vjp(1)
_sub_pick = _pick_vjp(0)


def _onehot_lane(v, h):
    lane = lax.broadcasted_iota(jnp.int32, (1, v.shape[1]), 1)
    return jnp.sum(v * (lane == h).astype(F32), axis=1, keepdims=True)


def _onehot_sub(v, h):
    sub = lax.broadcasted_iota(jnp.int32, (v.shape[0], 1), 0)
    return jnp.sum(v * (sub == h).astype(F32), axis=0, keepdims=True)


def _ssd_consts(dirn, picks):
    ri = lax.broadcasted_iota(jnp.int32, (CHUNK, CHUNK), 0)
    ci = lax.broadcasted_iota(jnp.int32, (CHUNK, CHUNK), 1)
    keep = (ci <= ri) if dirn == 0 else (ci >= ri)
    tri = keep.astype(F32)
    tri_t = (~keep | (ci == ri)).astype(F32)
    lane = lax.broadcasted_iota(jnp.int32, (1, LANE), 1)
    sub = lax.broadcasted_iota(jnp.int32, (CHUNK, 1), 0)
    end = (sub == (CHUNK - 1 if dirn == 0 else 0)).astype(F32)
    lo_half = lane < HEAD_DIM
    pick = (_lane_pick, _sub_pick) if picks else (_onehot_lane, _onehot_sub)
    return keep, tri, tri_t, end, lo_half, N_HEADS * dirn, pick


def _ssd_chunk(consts, x_t, b_t, c_t, dtr, bias, alog, h_t):
    keep, tri, tri_t, end, lo_half, h_base, (lane_pick, sub_pick) = consts
    dt = _softplus(dtr + bias)
    a = dt * (-jnp.exp(alog))
    cs = _tri_dot(tri, tri_t, a)
    cs_t = cs.T
    tot = jnp.sum(cs * end, axis=0, keepdims=True)
    ys, hn = [], []
    for g in range(N_GROUPS):
        bm, cm = _silu(b_t[g]), _silu(c_t[g])
        gm = _nt(cm, bm)
        for jj in range(2):
            j = 2 * g + jj
            hh = (h_base + 2 * j, h_base + 2 * j + 1)
            col = [lane_pick(cs, h) for h in hh]
            row = [sub_pick(cs_t, h) for h in hh]
            dth = [lane_pick(dt, h) for h in hh]
            toth = [lane_pick(tot, h) for h in hh]
            xd = _silu(x_t[j]) * jnp.where(lo_half, dth[0], dth[1])
            yd = [_nn(gm * jnp.exp(jnp.where(keep, col[k] - row[k], NEG)), xd) for k in range(2)]
            cp = jnp.where(lo_half, col[0], col[1])
            tp = jnp.where(lo_half, toth[0], toth[1])
            ys.append(jnp.where(lo_half, yd[0], yd[1]) + _nn(cm, h_t[j]) * jnp.exp(cp))
            hn.append(h_t[j] * jnp.exp(tp) + _tn(bm, xd * jnp.exp(tp - cp)))
    return ys, hn


N_PAIR = D_INNER // LANE


def _tiles(ref, n):
    return [ref[:, LANE * j:LANE * (j + 1)].astype(F32) for j in range(n)]


def _ssd_in_specs(cmap):
    return [pl.BlockSpec((CHUNK, D_INNER), lambda i: (cmap(i), 0)),
            pl.BlockSpec((CHUNK, N_GROUPS * D_STATE), lambda i: (cmap(i), 2)),
            pl.BlockSpec((CHUNK, N_GROUPS * D_STATE), lambda i: (cmap(i), 3)),
            pl.BlockSpec((CHUNK, LANE), lambda i: (cmap(i), 0)),
            pl.BlockSpec((1, LANE), lambda i: (0, 0)), pl.BlockSpec((1, LANE), lambda i: (0, 0))]


def ssd_fwd(name, xbc, pdt, bias_row, alog_row, dirn, T):
    nc = T // CHUNK
    cmap = (lambda i: i) if dirn == 0 else (lambda i: nc - 1 - i)

    def body(x_ref, b_ref, c_ref, dt_ref, bias_ref, alog_ref, y_ref, hs_ref, h_scr):
        @pl.when(pl.program_id(0) == 0)
        def _():
            h_scr[...] = jnp.zeros(h_scr.shape, F32)

        hs_ref[0] = h_scr[...]
        ys, hn = _ssd_chunk(_ssd_consts(dirn, True), _tiles(x_ref, N_PAIR), _tiles(b_ref, N_GROUPS), _tiles(c_ref, N_GROUPS),
                            dt_ref[...], bias_ref[...], alog_ref[...], _tiles(h_scr, N_PAIR))
        for j in range(N_PAIR):
            y_ref[:, LANE * j:LANE * (j + 1)] = ys[j]
            h_scr[:, LANE * j:LANE * (j + 1)] = hn[j]

    return pl.pallas_call(
        body, out_shape=[jax.ShapeDtypeStruct((T, D_INNER), F32), jax.ShapeDtypeStruct((nc, D_STATE, D_INNER), F32)],
        grid=(nc,), in_specs=_ssd_in_specs(cmap),
        out_specs=[pl.BlockSpec((CHUNK, D_INNER), lambda i: (cmap(i), 0)),
                   pl.BlockSpec((1, D_STATE, D_INNER), lambda i: (cmap(i), 0, 0))],
        scratch_shapes=[pltpu.VMEM((D_STATE, D_INNER), F32)], name=name,
        compiler_params=_cparams(("arbitrary",)))(xbc, xbc, xbc, pdt, bias_row, alog_row)


def ssd_bwd(name, xbc, pdt, bias_row, alog_row, hs, dy, adds, dirn, T):
    nc = T // CHUNK
    cmap = (lambda i: nc - 1 - i) if dirn == 0 else (lambda i: i)
    GS = N_GROUPS * D_STATE

    n_add = len(adds)

    def body(x_ref, b_ref, c_ref, dt_ref, bias_ref, alog_ref, hs_ref, dy_ref, *rest):
        add_refs, (dx_ref, db_ref, dc_ref, ddt_ref, dbias_ref, dalog_ref, dh_scr) = rest[:n_add], rest[n_add:]
        ax_ref = add_refs[0]
        ab_ref, ac_ref, adt_ref = add_refs[1:] if n_add == 4 else (None, None, None)
        first = pl.program_id(0) == 0

        @pl.when(first)
        def _():
            dh_scr[...] = jnp.zeros(dh_scr.shape, F32)
            dbias_ref[...] = jnp.zeros(dbias_ref.shape, F32)
            dalog_ref[...] = jnp.zeros(dalog_ref.shape, F32)

        consts = _ssd_consts(dirn, False)
        fn = lambda *a: _ssd_chunk(consts, *a)
        _, vjp_fn = jax.vjp(fn, _tiles(x_ref, N_PAIR), _tiles(b_ref, N_GROUPS), _tiles(c_ref, N_GROUPS), dt_ref[...],
                            bias_ref[...], alog_ref[...], [hs_ref[0, :, LANE * j:LANE * (j + 1)] for j in range(N_PAIR)])
        dx, db, dc, ddt, dbias, dalog, dh = vjp_fn((_tiles(dy_ref, N_PAIR), _tiles(dh_scr, N_PAIR)))
        for j in range(N_PAIR):
            s = slice(LANE * j, LANE * (j + 1))
            dx_ref[:, s] = dx[j] + ax_ref[:, s]
            dh_scr[:, s] = dh[j]
        for g in range(N_GROUPS):
            s = slice(LANE * g, LANE * (g + 1))
            db_ref[:, s] = db[g] + (ab_ref[:, s] if n_add == 4 else 0.0)
            dc_ref[:, s] = dc[g] + (ac_ref[:, s] if n_add == 4 else 0.0)
        ddt_ref[...] = ddt + (adt_ref[...] if n_add == 4 else 0.0)
        dbias_ref[...] += dbias
        dalog_ref[...] += dalog

    blk = lambda w: pl.BlockSpec((CHUNK, w), lambda i: (cmap(i), 0))
    row = pl.BlockSpec((1, LANE), lambda i: (0, 0))
    return pl.pallas_call(
        body,
        out_shape=[jax.ShapeDtypeStruct((T, D_INNER), F32), jax.ShapeDtypeStruct((T, GS), F32),
                   jax.ShapeDtypeStruct((T, GS), F32), jax.ShapeDtypeStruct((T, LANE), F32),
                   jax.ShapeDtypeStruct((1, LANE), F32), jax.ShapeDtypeStruct((1, LANE), F32)],
        grid=(nc,),
        in_specs=_ssd_in_specs(cmap) + [pl.BlockSpec((1, D_STATE, D_INNER), lambda i: (cmap(i), 0, 0)), blk(D_INNER)]
        + [blk(D_INNER), blk(GS), blk(GS), blk(LANE)][:n_add],
        out_specs=[blk(D_INNER), blk(GS), blk(GS), blk(LANE), row, row],
        scratch_shapes=[pltpu.VMEM((D_STATE, D_INNER), F32)], name=name,
        compiler_params=_cparams(("arbitrary",)))(xbc, xbc, xbc, pdt, bias_row, alog_row, hs, dy, *adds)


def loss_head(y, target, T, tT=256):
    def body(y_ref, t_ref, dy_ref, sq_ref):
        @pl.when(pl.program_id(0) == 0)
        def _():
            sq_ref[...] = jnp.zeros(sq_ref.shape, F32)
        e = y_ref[...] - t_ref[...]
        dy_ref[...] = e * (1.0 / D_MODEL)
        sq_ref[...] += jnp.sum(e * e, axis=0, keepdims=True)

    spec = pl.BlockSpec((tT, D_MODEL), lambda i: (i, 0))
    return pl.pallas_call(
        body, out_shape=[jax.ShapeDtypeStruct((T, D_MODEL), F32), jax.ShapeDtypeStruct((1, D_MODEL), F32)],
        grid=(T // tT,), in_specs=[spec, spec], out_specs=[spec, pl.BlockSpec((1, D_MODEL), lambda i: (0, 0))],
        name="loss_head", compiler_params=_cparams(("arbitrary",)))(y, target)


MESH_ID = pl.DeviceIdType.MESH


def all_gather(name, v):
    R, W = v.shape

    def body(v_ref, out_ref, send_sems, recv_sems, local_sem):
        x, y, c = lax.axis_index("x"), lax.axis_index("y"), lax.axis_index("c")
        me, sibling = (x, y, c), (x, y, 1 - c)
        chips = [(1 - x, y), (x, 1 - y), (1 - x, 1 - y)]

        def slot(px, py, pc):
            return out_ref.at[4 * px + 2 * py + pc]

        def copy(k, block, to, src=None):
            return pltpu.make_async_remote_copy(
                src_ref=slot(*block) if src is None else src, dst_ref=slot(*block), send_sem=send_sems.at[k],
                recv_sem=recv_sems.at[k], device_id=to, device_id_type=MESH_ID)

        mine = pltpu.make_async_copy(v_ref, slot(*me), local_sem)
        mine.start()
        first = [copy(0, me, sibling, src=v_ref)]
        first += [copy(1 + j, me, (*chip, c), src=v_ref) for j, chip in enumerate(chips)]
        for cp in first:
            cp.start()
        passed = [copy(4 + j, (*chip, c), sibling) for j, chip in enumerate(chips)]
        for j, chip in enumerate(chips):
            copy(1 + j, (*chip, c), me).wait_recv()
            passed[j].start()
        copy(0, sibling, me).wait_recv()
        for j, chip in enumerate(chips):
            copy(4 + j, (*chip, 1 - c), me).wait_recv()
        for cp in first + passed:
            cp.wait_send()
        mine.wait()

    return pl.pallas_call(
        body, out_shape=jax.ShapeDtypeStruct((N_DEV, R, W), v.dtype),
        in_specs=[pl.BlockSpec(memory_space=pl.ANY)], out_specs=pl.BlockSpec(memory_space=pl.ANY),
        scratch_shapes=[pltpu.SemaphoreType.DMA((7,)), pltpu.SemaphoreType.DMA((7,)), pltpu.SemaphoreType.DMA],
        name=name, compiler_params=pltpu.CompilerParams(has_side_effects=True))(v)


def exchange_sibling(name, g):
    _, R, W = g.shape

    def body(g_ref, out_ref, send_sems, recv_sems):
        x, y, c = lax.axis_index("x"), lax.axis_index("y"), lax.axis_index("c")
        copies = [pltpu.make_async_remote_copy(
            src_ref=g_ref.at[2 * q + (1 - c)], dst_ref=out_ref.at[q], send_sem=send_sems.at[q], recv_sem=recv_sems.at[q],
            device_id=(x, y, 1 - c), device_id_type=MESH_ID) for q in range(4)]
        for cp in copies:
            cp.start()
        for cp in copies:
            cp.wait_recv()
        for cp in copies:
            cp.wait_send()

    return pl.pallas_call(
        body, out_shape=jax.ShapeDtypeStruct((4, R, W), g.dtype),
        in_specs=[pl.BlockSpec(memory_space=pl.ANY)], out_specs=pl.BlockSpec(memory_space=pl.ANY),
        scratch_shapes=[pltpu.SemaphoreType.DMA((4,)), pltpu.SemaphoreType.DMA((4,))],
        name=name, compiler_params=pltpu.CompilerParams(has_side_effects=True))(g)


def pair_sum(name, g, sib, core, tr):
    _, R, W = g.shape

    def body(core_ref, g_ref, s_ref, o_ref):
        o_ref[...] = (g_ref[...].astype(F32) + s_ref[...].astype(F32)).astype(o_ref.dtype)

    blk = pl.BlockSpec((1, tr, W), lambda q, i, core_ref: (q, i, 0))
    return pl.pallas_call(
        body, out_shape=jax.ShapeDtypeStruct((4, R, W), g.dtype),
        grid_spec=pltpu.PrefetchScalarGridSpec(
            num_scalar_prefetch=1, grid=(4, R // tr),
            in_specs=[pl.BlockSpec((1, tr, W), lambda q, i, core_ref: (2 * q + core_ref[0], i, 0)), blk], out_specs=blk),
        name=name, compiler_params=_cparams(("parallel", "parallel")))(core, g, sib)


def exchange_chips(name, s):
    _, R, W = s.shape

    def body(s_ref, out_ref, send_sems, recv_sems, local_sem):
        x, y, c = lax.axis_index("x"), lax.axis_index("y"), lax.axis_index("c")
        mine = 2 * x + y
        own = pltpu.make_async_copy(s_ref.at[mine], out_ref.at[mine], local_sem)
        own.start()
        copies = []
        for k in range(1, 4):
            px = 1 - x if (k >> 1) & 1 else x
            py = 1 - y if k & 1 else y
            copies.append(pltpu.make_async_remote_copy(
                src_ref=s_ref.at[2 * px + py], dst_ref=out_ref.at[mine], send_sem=send_sems.at[k - 1],
                recv_sem=recv_sems.at[k - 1], device_id=(px, py, c), device_id_type=MESH_ID))
        for cp in copies:
            cp.start()
        for cp in copies:
            cp.wait_recv()
        for cp in copies:
            cp.wait_send()
        own.wait()

    return pl.pallas_call(
        body, out_shape=jax.ShapeDtypeStruct(s.shape, s.dtype),
        in_specs=[pl.BlockSpec(memory_space=pl.ANY)], out_specs=pl.BlockSpec(memory_space=pl.ANY),
        scratch_shapes=[pltpu.SemaphoreType.DMA((3,)), pltpu.SemaphoreType.DMA((3,)), pltpu.SemaphoreType.DMA],
        name=name, compiler_params=pltpu.CompilerParams(has_side_effects=True))(s)


def reduce_pieces(g, tr):
    sib = exchange_sibling("grad_exchange_sibling", g)
    core = lax.axis_index("c").astype(jnp.int32).reshape(1)
    pair = pair_sum("grad_pair_sum", g, sib, core, tr)
    landed = exchange_chips("grad_exchange_chips", pair)
    return sum_slots("sum_grad_pieces", landed, tr)


def sum_slots(name, parts, tr):
    n_slot, R, W = parts.shape

    def body(p_ref, o_ref):
        g = p_ref[0].astype(F32)
        for s in range(1, n_slot):
            g = g + p_ref[s].astype(F32)
        o_ref[...] = g

    return pl.pallas_call(
        body, out_shape=jax.ShapeDtypeStruct((R, W), F32), grid=(R // tr,),
        in_specs=[pl.BlockSpec((n_slot, tr, W), lambda i: (0, i, 0))], out_specs=pl.BlockSpec((tr, W), lambda i: (i, 0)),
        name=name, compiler_params=_cparams(("parallel",)))(parts)


def adamw(name, parts, w, m, v, tr):
    R, W = w.shape
    n_slot = parts.shape[0]
    c1 = 1.0 / (1.0 - ADAM_B1 ** ADAM_STEP)
    c2 = 1.0 / (1.0 - ADAM_B2 ** ADAM_STEP)

    def body(p_ref, w_ref, m_ref, v_ref, g_ref, d_ref, nm_ref, nv_ref):
        g = p_ref[0]
        for s in range(1, n_slot):
            g = g + p_ref[s]
        nm = ADAM_B1 * m_ref[...] + (1.0 - ADAM_B1) * g
        nv = ADAM_B2 * v_ref[...] + (1.0 - ADAM_B2) * (g * g)
        g_ref[...] = g
        nm_ref[...] = nm
        nv_ref[...] = nv
        d_ref[...] = -ADAM_LR * ((nm * c1) / (jnp.sqrt(nv * c2) + ADAM_EPS) + ADAM_WD * w_ref[...])

    spec = pl.BlockSpec((tr, W), lambda i: (i, 0))
    return pl.pallas_call(
        body, out_shape=[jax.ShapeDtypeStruct((R, W), F32)] * 4, grid=(R // tr,),
        in_specs=[pl.BlockSpec((n_slot, tr, W), lambda i: (0, i, 0)), spec, spec, spec], out_specs=[spec] * 4,
        name=name, compiler_params=_cparams(("parallel",)))(parts, w, m, v)


def _pack(arrs, row_mult):
    flat = jnp.concatenate([a.reshape(-1) for a in arrs])
    n = flat.shape[0]
    rows = -(-n // LANE)
    rows = -(-rows // row_mult) * row_mult
    return jnp.pad(flat, (0, rows * LANE - n)).reshape(rows, LANE)


def _unpack(buf, shapes, lead=()):
    flat = buf.reshape(lead + (-1,))
    out, off = [], 0
    for s in shapes:
        n = math.prod(s)
        out.append(flat[..., off:off + n].reshape(lead + tuple(s)))
        off += n
    return out


def _rows_sharded(name):
    return name in COL_T or BIG_AXIS[name] == 1


def _shard_for_gather(name, w):
    return jnp.swapaxes(w, 1, 2) if name in COL_T else w


def _full_from_gathered(name, g):
    if _rows_sharded(name):
        return jnp.transpose(g, (1, 0, 2, 3)).reshape(g.shape[1], N_DEV * g.shape[2], g.shape[3])
    return jnp.transpose(g, (1, 2, 0, 3)).reshape(g.shape[1], g.shape[2], N_DEV * g.shape[3])


def _pieces_from_full(name, f):
    A, B = f.shape
    if _rows_sharded(name):
        return f.reshape(N_DEV, -1)
    return jnp.transpose(f.reshape(A, N_DEV, B // N_DEV), (1, 0, 2)).reshape(N_DEV, -1)


def _piece_shape(name, shard_shape):
    L, a, b = shard_shape
    return (L, b, a) if name in COL_T else (L, a, b)


def _rows(v):
    return v.reshape(1, -1).astype(F32)


def _head_rows(W):
    bias = jnp.pad(W["dt_bias"].reshape(1, -1), ((0, 0), (0, LANE - 2 * N_HEADS)))
    alog = jnp.pad(W["a_log"].reshape(1, -1), ((0, 0), (0, LANE - 2 * N_HEADS)))
    return bias, alog


def layer_fwd(li, x, p_l, W, T):
    n = lambda s: f"l{li}_{s}"
    S = {"x": x}
    proj = matmul(n("mm_in"), x, W["w_in"], "nt", out_dtype=BF16)
    pdt = matmul(n("mm_dt"), x, W["w_in"][N_IN_PAD - LANE:], "nt")
    (u0,) = rowcall(n("glu"), glu_fn, [(proj, 1024, 0), (proj, 1024, 1)], [], [(1024, F32, None)], T)
    u1 = conv_fwd(n("conv_a"), u0, 0, W["conv_a_w"], _rows(W["conv_a_b"]), T)
    (u3,) = rowcall(n("lnsilu"), lnsilu_fn, [(u1, 1024, 0)], [_rows(W["ln_a_g"]), _rows(W["ln_a_b"])],
                    [(1024, BF16, None)], T)
    y_a = matmul(n("mm_aout"), u3, W["w_a_out"], "nn")
    xbc = conv_fwd(n("conv_s"), proj, 6144 // CONV_CB, W["ssm_conv_w"], _rows(W["ssm_conv_b"]), T)
    bias_row, alog_row = _head_rows(W)
    y_f, hs_f = ssd_fwd(n("ssd_f"), xbc, pdt, bias_row, alog_row, 0, T)
    y_b, hs_b = ssd_fwd(n("ssd_r"), xbc, pdt, bias_row, alog_row, 1, T)
    dsk = jnp.repeat(W["d_skip"], HEAD_DIM).reshape(1, D_INNER)
    (yn,) = rowcall(n("gnorm"), gnorm_fn, [(y_f, 256, 0), (y_b, 256, 0), (xbc, 256, 0), (proj, 256, 16)],
                    [dsk, _rows(W["ssm_norm_g"])], [(256, BF16, None)], T, groups=N_GROUPS)
    y_bo = matmul(n("mm_bout"), yn, W["w_b_out"], "nn")
    (merged,) = rowcall(n("merge"), merge_fn, [(proj, 1024, 2), (proj, 1024, 3), (y_a, 1024, 0), (y_bo, 1024, 0)], [],
                        [(1024, BF16, None)], T)
    mix = matmul(n("mm_o"), merged, W["w_o"], "nn")
    (h,) = rowcall(n("ln1"), resln_fn, [(x, 1024, 0), (mix, 1024, 0)], [_rows(W["ln1_g"]), _rows(W["ln1_b"])],
                   [(1024, F32, None)], T)
    gu = matmul(n("mm_gu"), h, W["w_gate_up"], "nt", out_dtype=BF16)
    (act,) = rowcall(n("swiglu"), swiglu_fn, [(gu, FFN_DIM, 0), (gu, FFN_DIM, 1)], [], [(FFN_DIM, BF16, None)], T)
    dn = matmul(n("mm_down"), act, W["w_down"], "nn")
    (h2,) = rowcall(n("ln2"), resln_fn, [(h, 1024, 0), (dn, 1024, 0)], [_rows(W["ln2_g"]), _rows(W["ln2_b"])],
                    [(1024, F32, None)], T)
    pe = matmul(n("mm_ple"), p_l, W["w_ple"], "nn")
    gl = matmul(n("mm_pg"), h2, W["w_ple_gate"], "nn")
    (xn,) = rowcall(n("pleout"), ple_fn, [(h2, 1024, 0), (pe, 1024, 0), (gl, 1024, 0)], [_rows(W["ple_norm_g"])],
                    [(1024, F32, None)], T)
    S.update(proj=proj, pdt=pdt, u0=u0, u1=u1, u3=u3, y_a=y_a, xbc=xbc, y_f=y_f, y_b=y_b, hs_f=hs_f, hs_b=hs_b, yn=yn, y_bo=y_bo,
             merged=merged, mix=mix, h=h, gu=gu, act=act, dn=dn, h2=h2, pe=pe, gl=gl, dsk=dsk, bias_row=bias_row,
             alog_row=alog_row)
    return xn, S


def layer_bwd(li, dxn, p_l, W, S, T):
    n = lambda s: f"l{li}_{s}"
    G = {}
    x, proj = S["x"], S["proj"]
    (dh2a, dpe, dgl), (dpg,) = rowvjp(
        n("pleout_b"), ple_fn, [(S["h2"], 1024, 0), (S["pe"], 1024, 0), (S["gl"], 1024, 0)], [_rows(W["ple_norm_g"])],
        [(dxn, 1024, 0)], [([0], F32, None), ([1], BF16, None), ([2], BF16, None)], T)
    G["ple_norm_g"] = dpg
    G["w_ple_gate"] = matmul(n("mm_pg_w"), S["h2"], dgl, "tn", out_dtype=BF16)
    G["w_ple"] = matmul(n("mm_ple_w"), p_l, dpe, "tn", out_dtype=BF16)
    dh2 = matmul(n("mm_pg_x"), dgl, W["w_ple_gate"], "nt", add=dh2a)
    (dha, ddn), (G["ln2_g"], G["ln2_b"]) = rowvjp(
        n("ln2_b"), resln_fn, [(S["h"], 1024, 0), (S["dn"], 1024, 0)], [_rows(W["ln2_g"]), _rows(W["ln2_b"])],
        [(dh2, 1024, 0)], [([0], F32, None), ([1], BF16, None)], T)
    G["w_down"] = matmul(n("mm_down_w"), S["act"], ddn, "tn", out_dtype=BF16)
    dact = matmul(n("mm_down_x"), ddn, W["w_down"], "nt", out_dtype=BF16)
    (dgu,), _ = rowvjp(n("swiglu_b"), swiglu_fn, [(S["gu"], FFN_DIM, 0), (S["gu"], FFN_DIM, 1)], [],
                       [(dact, FFN_DIM, 0)], [([0, 1], BF16, None)], T)
    G["w_gate_up"] = matmul(n("mm_gu_w"), dgu, S["h"], "tn", out_dtype=BF16)
    dh = matmul(n("mm_gu_x"), dgu, W["w_gate_up"], "nn", add=dha)
    (dxa, dmix), (G["ln1_g"], G["ln1_b"]) = rowvjp(
        n("ln1_b"), resln_fn, [(x, 1024, 0), (S["mix"], 1024, 0)], [_rows(W["ln1_g"]), _rows(W["ln1_b"])],
        [(dh, 1024, 0)], [([0], F32, None), ([1], BF16, None)], T)
    G["w_o"] = matmul(n("mm_o_w"), S["merged"], dmix, "tn", out_dtype=BF16)
    dmerged = matmul(n("mm_o_x"), dmix, W["w_o"], "nt")
    dproj = jax.ShapeDtypeStruct((T, N_IN_PAD), BF16)
    (dproj, dy_a, dy_bo), _ = rowvjp(
        n("merge_b"), merge_fn, [(proj, 1024, 2), (proj, 1024, 3), (S["y_a"], 1024, 0), (S["y_bo"], 1024, 0)], [],
        [(dmerged, 1024, 0)], [([0, 1], BF16, (dproj, 1)), ([2], BF16, None), ([3], BF16, None)], T)
    G["w_a_out"] = matmul(n("mm_aout_w"), S["u3"], dy_a, "tn", out_dtype=BF16)
    du3 = matmul(n("mm_aout_x"), dy_a, W["w_a_out"], "nt")
    (du1,), (G["ln_a_g"], G["ln_a_b"]) = rowvjp(
        n("lnsilu_b"), lnsilu_fn, [(S["u1"], 1024, 0)], [_rows(W["ln_a_g"]), _rows(W["ln_a_b"])], [(du3, 1024, 0)],
        [([0], F32, None)], T)
    du0, G["conv_a_w"], G["conv_a_b"] = conv_bwd(n("conv_a_b"), du1, S["u0"], 0, W["conv_a_w"], T)
    (dproj,), _ = rowvjp(n("glu_b"), glu_fn, [(proj, 1024, 0), (proj, 1024, 1)], [], [(du0, 1024, 0)],
                         [([0, 1], BF16, (dproj, 0))], T)
    G["w_b_out"] = matmul(n("mm_bout_w"), S["yn"], dy_bo, "tn", out_dtype=BF16)
    dyn = matmul(n("mm_bout_x"), dy_bo, W["w_b_out"], "nt")
    (dys, dxs, dproj), (ddsk, G["ssm_norm_g"]) = rowvjp(
        n("gnorm_b"), gnorm_fn, [(S["y_f"], 256, 0), (S["y_b"], 256, 0), (S["xbc"], 256, 0), (proj, 256, 16)],
        [S["dsk"], _rows(W["ssm_norm_g"])], [(dyn, 256, 0)],
        [([0], F32, None), ([2], F32, None), ([3], BF16, (dproj, 16))], T, groups=N_GROUPS)
    G["d_skip"] = ddsk.reshape(N_HEADS, HEAD_DIM).sum(axis=1)
    dx1, db1, dc1, ddt1, dbias_f, dalog_f = ssd_bwd(
        n("ssd_f_b"), S["xbc"], S["pdt"], S["bias_row"], S["alog_row"], S["hs_f"], dys, (dxs,), 0, T)
    dxx, dbb, dcc, ddt, dbias_r, dalog_r = ssd_bwd(
        n("ssd_r_b"), S["xbc"], S["pdt"], S["bias_row"], S["alog_row"], S["hs_b"], dys, (dx1, db1, dc1, ddt1), 1, T)
    G["dt_bias"] = (dbias_f + dbias_r)[0, :2 * N_HEADS].reshape(2, N_HEADS)
    G["a_log"] = (dalog_f + dalog_r)[0, :2 * N_HEADS].reshape(2, N_HEADS)
    cw = W["ssm_conv_w"]
    b0 = 6144 // CONV_CB
    dproj, dwx, dbx = conv_bwd(n("conv_sx_b"), dxx, proj, b0, cw[:, :D_INNER], T, into=(dproj, b0))
    dproj, dwb, dbb_ = conv_bwd(n("conv_sb_b"), dbb, proj, b0 + 4, cw[:, D_INNER:D_INNER + 1024], T, into=(dproj, b0 + 4))
    dproj, dwc, dbc = conv_bwd(n("conv_sc_b"), dcc, proj, b0 + 6, cw[:, D_INNER + 1024:], T, into=(dproj, b0 + 6))
    G["ssm_conv_w"] = jnp.concatenate([dwx, dwb, dwc], axis=1)
    G["ssm_conv_b"] = jnp.concatenate([dbx, dbb_, dbc], axis=1)
    (dproj,) = rowcall(n("dt_cast"), ident_fn, [(ddt, LANE, 0)], [], [(LANE, BF16, (dproj, (N_IN_PAD - LANE) // LANE))], T)
    G["w_in"] = matmul(n("mm_in_w"), dproj, x, "tn", out_dtype=BF16)[:N_IN]
    dx = matmul(n("mm_in_x"), dproj, W["w_in"], "nn", add=dxa)
    return dx, G


def local_step(x, p, loss_target, FW, T):
    Ws, saves = [], []
    cur = x
    for li in range(DEPTH):
        W = {k: v[li] for k, v in FW.items()}
        Ws.append(W)
        cur, S = layer_fwd(li, cur, p[li], W, T)
        saves.append(S)
    dcur, sq = loss_head(cur, loss_target, T)
    loss = 0.5 * jnp.sum(sq) / D_MODEL
    grads = [None] * DEPTH
    for li in reversed(range(DEPTH)):
        dcur, grads[li] = layer_bwd(li, dcur, p[li], Ws[li], saves[li], T)
    return loss, dcur, grads


def kernel(x, p, w_in, conv_a_w, conv_a_b, ln_a_g, ln_a_b, w_a_out, ssm_conv_w, ssm_conv_b, a_log, dt_bias, d_skip, ssm_norm_g, w_b_out, w_o, ln1_g, ln1_b, w_gate_up, w_down, ln2_g, ln2_b, w_ple, ple_norm_g, w_ple_gate, loss_target, m_w_in, m_conv_a_w, m_conv_a_b, m_ln_a_g, m_ln_a_b, m_w_a_out, m_ssm_conv_w, m_ssm_conv_b, m_a_log, m_dt_bias, m_d_skip, m_ssm_norm_g, m_w_b_out, m_w_o, m_ln1_g, m_ln1_b, m_w_gate_up, m_w_down, m_ln2_g, m_ln2_b, m_w_ple, m_ple_norm_g, m_w_ple_gate, v_w_in, v_conv_a_w, v_conv_a_b, v_ln_a_g, v_ln_a_b, v_w_a_out, v_ssm_conv_w, v_ssm_conv_b, v_a_log, v_dt_bias, v_d_skip, v_ssm_norm_g, v_w_b_out, v_w_o, v_ln1_g, v_ln1_b, v_w_gate_up, v_w_down, v_ln2_g, v_ln2_b, v_w_ple, v_ple_norm_g, v_w_ple_gate):
    A = dict(locals())
    w = {k: A[k] for k in WEIGHTS}
    m = {k: A["m_" + k] for k in WEIGHTS}
    v = {k: A["v_" + k] for k in WEIGHTS}
    T = x.shape[1]
    small_shapes = [w[k].shape for k in SMALL]

    mm_names = [k for k in BIG if k not in CONV_W]
    shards = [_shard_for_gather(k, w[k].astype(BF16)) for k in mm_names]
    gathered = all_gather("gather_weights", _pack(shards, 16))
    FW = {k: _full_from_gathered(k, g) for k, g in zip(mm_names, _unpack(gathered, [t.shape for t in shards], (N_DEV,)))}
    FW["w_in"] = jnp.pad(FW["w_in"], ((0, 0), (0, N_IN_PAD - N_IN), (0, 0)))
    gathered = all_gather("gather_conv_weights", _pack([w[k] for k in CONV_W], SUBLANE))
    FW.update({k: _full_from_gathered(k, g)
               for k, g in zip(CONV_W, _unpack(gathered, [w[k].shape for k in CONV_W], (N_DEV,)))})
    FW.update({k: w[k] for k in SMALL})

    loss, grad_x, gfull = local_step(x[0], p[:, 0], loss_target[0], FW, T)
    loss = lax.psum(loss, ("x", "y", "c"))

    TR = 512
    flat = jnp.concatenate([_pieces_from_full(k, gfull[li][k]).astype(BF16) for k in BIG for li in range(DEPTH)], axis=1)
    rows = -(-flat.shape[1] // (LANE * TR)) * TR
    gpack = jnp.pad(flat, ((0, 0), (0, rows * LANE - flat.shape[1]))).reshape(N_DEV, rows, LANE)
    gsum = reduce_pieces(gpack, TR)
    gshard = [jnp.swapaxes(g, 1, 2) if k in COL_T else g
              for k, g in zip(BIG, _unpack(gsum, [_piece_shape(k, w[k].shape) for k in BIG]))]
    res_big = [{}, {}, {}, {}]
    for k, g in zip(BIG, gshard):
        two_d = lambda t: t.reshape(-1, t.shape[-1])
        rows_k = two_d(w[k]).shape[0]
        tr = max(d for d in range(1, min(rows_k, 256) + 1) if rows_k % d == 0 and (d % SUBLANE == 0 or d == rows_k))
        res = adamw("adamw_" + k, two_d(g)[None], two_d(w[k]), two_d(m[k]), two_d(v[k]), tr)
        for q in range(4):
            res_big[q][k] = res[q].reshape(w[k].shape)

    spack = _pack([jnp.stack([gfull[li][k] for li in range(DEPTH)]).reshape(w[k].shape) for k in SMALL], SUBLANE)
    sall = all_gather("gather_small_grads", spack)
    rs = spack.shape[0]
    res_small = adamw("adamw_small", sall, _pack([w[k] for k in SMALL], SUBLANE), _pack([m[k] for k in SMALL], SUBLANE),
                      _pack([v[k] for k in SMALL], SUBLANE), rs)
    res_small = [dict(zip(SMALL, _unpack(r, small_shapes))) for r in res_small]

    outs = [loss, grad_x[None]]
    for q in range(4):
        for k in WEIGHTS:
            outs.append(res_big[q][k] if k in res_big[q] else res_small[q][k])
    return tuple(outs)
```

```python
import math

import jax
import jax.numpy as jnp
from jax import lax
from jax.experimental import pallas as pl
from jax.experimental.pallas import tpu as pltpu

F32 = jnp.float32
BF16 = jnp.bfloat16

D_MODEL = 1024
CONV_DIM = 1024
CONV_KERNEL = 31
D_INNER = 2048
HEAD_DIM = 64
N_HEADS = 32
N_GROUPS = 8
D_STATE = 128
SSM_CONV = 5
CHUNK = 128
XBC_DIM = D_INNER + 2 * N_GROUPS * D_STATE
FFN_DIM = 2816
PLE_DIM = 256
N_IN = 2 * CONV_DIM + 2 * D_MODEL + D_INNER + XBC_DIM + 2 * N_HEADS
N_IN_PAD = 10368
DEPTH = 2
N_DEV = 8
ALPHA = (2 * DEPTH) ** 0.25
LN_EPS = 1e-5
RMS_EPS = 1e-6
ADAM_LR, ADAM_B1, ADAM_B2, ADAM_EPS, ADAM_WD, ADAM_STEP = 0.001, 0.9, 0.999, 1e-08, 0.01, 10

LANE = 128
SUBLANE = 8
HALO = 16
VMEM_LIMIT = 52 * 1024 * 1024
NEG = -1e30

BIG = ["w_in", "conv_a_w", "w_a_out", "ssm_conv_w", "w_b_out", "w_o", "w_gate_up", "w_down", "w_ple", "w_ple_gate"]
BIG_AXIS = {"w_in": 2, "conv_a_w": 2, "w_a_out": 1, "ssm_conv_w": 2, "w_b_out": 1, "w_o": 1, "w_gate_up": 2,
            "w_down": 1, "w_ple": 2, "w_ple_gate": 1}
COL_T = ["w_in", "w_gate_up"]
CONV_W = ["conv_a_w", "ssm_conv_w"]
SMALL = ["conv_a_b", "ln_a_g", "ln_a_b", "ssm_conv_b", "a_log", "dt_bias", "d_skip", "ssm_norm_g", "ln1_g", "ln1_b",
         "ln2_g", "ln2_b", "ple_norm_g"]
WEIGHTS = ["w_in", "conv_a_w", "conv_a_b", "ln_a_g", "ln_a_b", "w_a_out", "ssm_conv_w", "ssm_conv_b", "a_log", "dt_bias",
           "d_skip", "ssm_norm_g", "w_b_out", "w_o", "ln1_g", "ln1_b", "w_gate_up", "w_down", "ln2_g", "ln2_b", "w_ple",
           "ple_norm_g", "w_ple_gate"]


def _cparams(sem):
    return pltpu.CompilerParams(dimension_semantics=sem, vmem_limit_bytes=VMEM_LIMIT)


def _pick(n, cap):
    if n <= cap:
        return n
    best = None
    for d in range(LANE, cap + 1, LANE):
        if n % d == 0:
            best = d
    assert best is not None, (n, cap)
    return best


def matmul(name, a, b, mode, out_dtype=F32, add=None):
    if mode == "nn":
        (M, K), (K2, N) = a.shape, b.shape
    elif mode == "nt":
        (M, K), (N, K2) = a.shape, b.shape
    else:
        (K, M), (K2, N) = a.shape, b.shape
    assert K == K2, (name, a.shape, b.shape)
    tm = _pick(M, 1024) if mode != "tn" else _pick(M, 1408)
    tn = _pick(N, 1408)
    tk = _pick(K, 2048) if mode == "tn" else (K if (mode == "nn" and K <= 2816) else _pick(K, 1408))
    nk = K // tk
    grid = (M // tm, N // tn, nk)
    if mode == "tn":
        a_spec = pl.BlockSpec((tk, tm), lambda i, j, k: (k, i))
    else:
        a_spec = pl.BlockSpec((tm, tk), lambda i, j, k: (i, k))
    if mode == "nt":
        b_spec = pl.BlockSpec((tn, tk), lambda i, j, k: (j, k))
    else:
        b_spec = pl.BlockSpec((tk, tn), lambda i, j, k: (k, j))
    o_spec = pl.BlockSpec((tm, tn), lambda i, j, k: (i, j))
    dims = {"nn": ((1,), (0,)), "nt": ((1,), (1,)), "tn": ((0,), (0,))}[mode]
    has_add = add is not None

    def body(a_ref, b_ref, *rest):
        if has_add:
            add_ref, o_ref, *scr = rest
        else:
            o_ref, *scr = rest
        part = lax.dot_general(a_ref[...].astype(BF16), b_ref[...].astype(BF16), (dims, ((), ())),
                               preferred_element_type=F32)

        def finish(v):
            if has_add:
                v = v + add_ref[...].astype(F32)
            o_ref[...] = v.astype(o_ref.dtype)

        if nk == 1:
            finish(part)
        else:
            acc = scr[0]
            k = pl.program_id(2)

            @pl.when(k == 0)
            def _():
                acc[...] = part

            @pl.when(k > 0)
            def _():
                acc[...] += part

            @pl.when(k == nk - 1)
            def _():
                finish(acc[...])

    in_specs = [a_spec, b_spec] + ([o_spec] if has_add else [])
    args = (a, b) + ((add,) if has_add else ())
    return pl.pallas_call(
        body, out_shape=jax.ShapeDtypeStruct((M, N), out_dtype), grid=grid, in_specs=in_specs, out_specs=o_spec,
        scratch_shapes=[pltpu.VMEM((tm, tn), F32)] if nk > 1 else [], name=name,
        compiler_params=_cparams(("parallel", "parallel", "arbitrary")))(*args)


def _row_specs(items, tT, groups):
    specs = []
    for (_, w, blk) in items:
        assert blk % groups == 0
        specs.append(pl.BlockSpec((tT, w * groups), (lambda i, b=blk // groups: (i, b))))
    return specs


def _slices(v, groups):
    if groups == 1:
        return [v]
    w = v.shape[1] // groups
    return [v[:, w * s:w * (s + 1)] for s in range(groups)]


def _cat(vs):
    return vs[0] if len(vs) == 1 else jnp.concatenate(vs, axis=1)


def rowcall(name, fn, ins, pars, outs, T, tT=256, groups=1):
    n_in, n_par = len(ins), len(pars)
    intos = [o[2] for o in outs if o[2] is not None]
    in_specs = (_row_specs(ins, tT, groups) + [pl.BlockSpec(p.shape, lambda i: (0, 0)) for p in pars]
                + [pl.BlockSpec(memory_space=pl.ANY)] * len(intos))
    out_specs, out_shapes, aliases = [], [], {}
    n_alias = 0
    for oi, (w, dt, into) in enumerate(outs):
        if into is None:
            out_specs.append(pl.BlockSpec((tT, w * groups), lambda i: (i, 0)))
            out_shapes.append(jax.ShapeDtypeStruct((T, w * groups), dt))
        else:
            arr, blk = into
            out_specs.append(pl.BlockSpec((tT, w * groups), lambda i, b=blk // groups: (i, b)))
            out_shapes.append(jax.ShapeDtypeStruct(arr.shape, arr.dtype))
            aliases[n_in + n_par + n_alias] = oi
            n_alias += 1

    def body(*refs):
        xs = [_slices(r[...].astype(F32), groups) for r in refs[:n_in]]
        ps = [_slices(r[...], groups) for r in refs[n_in:n_in + n_par]]
        o_refs = refs[n_in + n_par + n_alias:]
        res = [fn(*[x[s] for x in xs], *[p[s] for p in ps]) for s in range(groups)]
        for k, r in enumerate(o_refs):
            r[...] = _cat([res[s][k] for s in range(groups)]).astype(r.dtype)

    res = pl.pallas_call(
        body, out_shape=out_shapes, grid=(T // tT,), in_specs=in_specs, out_specs=out_specs,
        input_output_aliases=aliases, name=name, compiler_params=_cparams(("parallel",)))(
            *[a for (a, _, _) in ins], *pars, *[a for (a, _) in intos])
    return list(res)


def rowvjp(name, fn, ins, pars, cts, douts, T, tT=256, groups=1):
    n_in, n_par, n_ct = len(ins), len(pars), len(cts)
    intos = [o[2] for o in douts if o[2] is not None and not isinstance(o[2][0], jax.ShapeDtypeStruct)]
    in_specs = (_row_specs(ins, tT, groups) + [pl.BlockSpec(p.shape, lambda i: (0, 0)) for p in pars]
                + _row_specs(cts, tT, groups) + [pl.BlockSpec(memory_space=pl.ANY)] * len(intos))
    out_specs, out_shapes, aliases = [], [], {}
    n_alias = 0
    for oi, (idxs, dt, into) in enumerate(douts):
        w = sum(ins[k][1] for k in idxs) * groups
        if into is None:
            out_specs.append(pl.BlockSpec((tT, w), lambda i: (i, 0)))
            out_shapes.append(jax.ShapeDtypeStruct((T, w), dt))
        else:
            assert len(idxs) == 1 or groups == 1
            arr, blk = into
            out_specs.append(pl.BlockSpec((tT, w), lambda i, b=blk // groups: (i, b)))
            out_shapes.append(jax.ShapeDtypeStruct(arr.shape, arr.dtype))
            if not isinstance(arr, jax.ShapeDtypeStruct):
                aliases[n_in + n_par + n_ct + n_alias] = oi
                n_alias += 1
    n_dout = len(douts)
    for p in pars:
        out_specs.append(pl.BlockSpec(p.shape, lambda i: (0, 0)))
        out_shapes.append(jax.ShapeDtypeStruct(p.shape, F32))

    def body(*refs):
        xs = [_slices(r[...].astype(F32), groups) for r in refs[:n_in]]
        ps = [_slices(r[...], groups) for r in refs[n_in:n_in + n_par]]
        cs = [_slices(r[...].astype(F32), groups) for r in refs[n_in + n_par:n_in + n_par + n_ct]]
        o_refs = refs[n_in + n_par + n_ct + n_alias:]
        grads = []
        for s in range(groups):
            _, vjp_fn = jax.vjp(fn, *[x[s] for x in xs], *[p[s] for p in ps])
            grads.append(vjp_fn(tuple(c[s] for c in cs)))
        for r, (idxs, _, _) in zip(o_refs[:n_dout], douts):
            r[...] = _cat([grads[s][k] for k in idxs for s in range(groups)]).astype(r.dtype)
        for k, r in enumerate(o_refs[n_dout:]):
            @pl.when(pl.program_id(0) == 0)
            def _(r=r):
                r[...] = jnp.zeros(r.shape, F32)
            r[...] += _cat([grads[s][n_in + k] for s in range(groups)])

    res = pl.pallas_call(
        body, out_shape=out_shapes, grid=(T // tT,), in_specs=in_specs, out_specs=out_specs,
        input_output_aliases=aliases, name=name, compiler_params=_cparams(("arbitrary",)))(
            *[a for (a, _, _) in ins], *pars, *[a for (a, _, _) in cts], *[a for (a, _) in intos])
    res = list(res)
    return res[:n_dout], res[n_dout:]


def _sigmoid(x):
    return 1.0 / (1.0 + jnp.exp(-x))


def _silu(x):
    return x * _sigmoid(x)


def _softplus(x):
    return jnp.maximum(x, 0.0) + jnp.log(1.0 + jnp.exp(-jnp.abs(x)))


def _ln(x, g, b):
    mu = jnp.mean(x, axis=-1, keepdims=True)
    xc = x - mu
    var = jnp.mean(xc * xc, axis=-1, keepdims=True)
    return xc * lax.rsqrt(var + LN_EPS) * g + b


def glu_fn(a, gt):
    return (a * _sigmoid(gt),)


def lnsilu_fn(u, g, b):
    return (_silu(_ln(u, g, b)),)


def gnorm_fn(yf, yb, xp, z, dsk, ng):
    y = (yf + yb + _silu(xp) * dsk) * _silu(z)
    return (y * lax.rsqrt(jnp.mean(y * y, axis=-1, keepdims=True) + RMS_EPS) * ng,)


def merge_fn(ga, gb, ya, yb):
    return (_sigmoid(ga) * ya + _sigmoid(gb) * yb,)


def resln_fn(x, r, g, b):
    return (_ln(ALPHA * x + r, g, b),)


def swiglu_fn(g, u):
    return (_silu(g) * u,)


def ple_fn(h2, pe, gl, g):
    e = pe * lax.rsqrt(jnp.mean(pe * pe, axis=-1, keepdims=True) + RMS_EPS) * g
    return (h2 + e * _sigmoid(gl),)


def ident_fn(v):
    return (v,)


CONV_CB = 512
CONV_TT = 512
CONV_TILES = 4
CONV_RB = CONV_TILES * SUBLANE
CONV_RED_TILES = 2
CONV_STATIC_MAX_K = 8
CONV_RED_TAPS = 16


def _conv_specs(blk0, T, tT, cb):
    nh = tT // HALO
    cur = pl.BlockSpec((tT, cb), lambda j, i: (i, blk0 + j))
    prev = pl.BlockSpec((HALO, cb), lambda j, i: (jnp.maximum(i * nh - 1, 0), blk0 + j))
    nxt = pl.BlockSpec((HALO, cb), lambda j, i: (jnp.minimum((i + 1) * nh, T // HALO - 1), blk0 + j))
    return [prev, cur, nxt]


def _phases(offsets):
    return sorted({off % SUBLANE for off in offsets})


def _fill_padded(pad_ref, prev_ref, cur_ref, next_ref, i, n_t, tT):
    pad_ref[pl.ds(0, HALO), :] = prev_ref[...].astype(F32) * (i > 0).astype(F32)
    pad_ref[pl.ds(HALO, tT), :] = cur_ref[...].astype(F32)
    pad_ref[pl.ds(HALO + tT, HALO), :] = next_ref[...].astype(F32) * (i < n_t - 1).astype(F32)


def _fill_shifted(sh_ref, pad_ref, phases, tT):
    for ph in phases:
        sh_ref[ph] = pad_ref[pl.ds(ph, tT + 3 * SUBLANE), :]


class _Shifted:
    def __init__(self, pad_ref, sh_ref, offsets, tT, static):
        self.pad_ref, self.sh_ref, self.static = pad_ref, sh_ref, static
        if not static:
            _fill_shifted(sh_ref, pad_ref, _phases(offsets), tT)

    def tiles(self, base, ls, off, n_tiles):
        if self.static:
            return tuple(self.pad_ref[pl.ds(base + off + SUBLANE * t, SUBLANE), ls] for t in range(n_tiles))
        q, ph = divmod(off, SUBLANE)
        return tuple(self.sh_ref[ph, pl.ds(base + SUBLANE * (q + t), SUBLANE), ls] for t in range(n_tiles))


def _row_loop(static, n, body, init):
    if not static:
        return lax.fori_loop(0, n, body, init)
    carry = init
    for r in range(n):
        carry = body(r, carry)
    return carry


def _conv_rows(src, w_ref, bias, o_ref, offsets, tT, cb):
    K = len(offsets)
    for lt in range(cb // LANE):
        ls = slice(LANE * lt, LANE * (lt + 1))
        wv = [jnp.broadcast_to(w_ref[k:k + 1, ls], (SUBLANE, LANE)) for k in range(K)]
        b0 = jnp.zeros((SUBLANE, LANE), F32) if bias is None else jnp.broadcast_to(bias[:, ls], (SUBLANE, LANE))

        def rows(r, carry, ls=ls, wv=wv, b0=b0):
            base = r * CONV_RB if src.static else pl.multiple_of(r * CONV_RB, CONV_RB)
            accs = [b0] * CONV_TILES
            for k, off in enumerate(offsets):
                accs = [a + d * wv[k] for a, d in zip(accs, src.tiles(base, ls, off, CONV_TILES))]
            o_ref[pl.ds(base, CONV_RB), ls] = jnp.concatenate(accs, axis=0).astype(o_ref.dtype)
            return carry

        _row_loop(src.static, tT // CONV_RB, rows, 0)


def conv_fwd(name, u, blk0, w, b, T):
    K, C = w.shape
    P = (K - 1) // 2
    tT, cb = min(CONV_TT, T), CONV_CB
    n_t = T // tT

    static = K <= CONV_STATIC_MAX_K

    def body(prev_ref, cur_ref, next_ref, w_ref, b_ref, o_ref, pad_ref, sh_ref):
        offsets = [HALO - P + k for k in range(K)]
        _fill_padded(pad_ref, prev_ref, cur_ref, next_ref, pl.program_id(1), n_t, tT)
        _conv_rows(_Shifted(pad_ref, sh_ref, offsets, tT, static), w_ref, b_ref[...], o_ref, offsets, tT, cb)

    return pl.pallas_call(
        body, out_shape=jax.ShapeDtypeStruct((T, C), F32), grid=(C // cb, n_t),
        in_specs=_conv_specs(blk0, T, tT, cb) + [pl.BlockSpec((K, cb), lambda j, i: (0, j)),
                                                  pl.BlockSpec((1, cb), lambda j, i: (0, j))],
        out_specs=pl.BlockSpec((tT, cb), lambda j, i: (i, j)),
        scratch_shapes=[pltpu.VMEM((tT + 2 * HALO, cb), F32), pltpu.VMEM((SUBLANE, tT + 3 * SUBLANE, cb), F32)],
        name=name, compiler_params=_cparams(("parallel", "arbitrary")))(u, u, u, w, b)


def conv_bwd(name, dy, u, blk0, w, T, into=None):
    K, C = w.shape
    P = (K - 1) // 2
    tT, cb = min(CONV_TT, T), CONV_CB
    n_t = T // tT

    static = K <= CONV_STATIC_MAX_K

    def body(dprev, dcur, dnext, uprev, ucur, unext, w_ref, *rest):
        if into is not None:
            rest = rest[1:]
        du_ref, dw_ref, db_ref, padd_ref, padu_ref, shd_ref, shu_ref = rest
        i = pl.program_id(1)
        offsets = [HALO - P + k for k in range(K)]
        back = [HALO + P - k for k in range(K)]
        _fill_padded(padd_ref, dprev, dcur, dnext, i, n_t, tT)
        _fill_padded(padu_ref, uprev, ucur, unext, i, n_t, tT)
        src_d = _Shifted(padd_ref, shd_ref, back + [HALO], tT, static)
        src_u = _Shifted(padu_ref, shu_ref, offsets, tT, static)
        _conv_rows(src_d, w_ref, None, du_ref, back, tT, cb)

        @pl.when(i == 0)
        def _():
            dw_ref[...] = jnp.zeros(dw_ref.shape, F32)
            db_ref[...] = jnp.zeros(db_ref.shape, F32)

        rb = CONV_RED_TILES * SUBLANE
        for lt in range(cb // LANE):
            ls = slice(LANE * lt, LANE * (lt + 1))

            zero = jnp.zeros((SUBLANE, LANE), F32)
            for k0 in range(0, K, CONV_RED_TAPS):
                ks = list(range(k0, min(k0 + CONV_RED_TAPS, K)))

                def red(r, accs, ls=ls, ks=ks, k0=k0):
                    base = r * rb if static else pl.multiple_of(r * rb, rb)
                    d0, d1 = src_d.tiles(base, ls, HALO, CONV_RED_TILES)
                    new = []
                    for acc, k in zip(accs, ks):
                        u0, u1 = src_u.tiles(base, ls, offsets[k], CONV_RED_TILES)
                        new.append(acc + d0 * u0 + d1 * u1)
                    if k0 == 0:
                        new.append(accs[-1] + d0 + d1)
                    return tuple(new)

                accs = _row_loop(static, tT // rb, red, (zero,) * (len(ks) + (k0 == 0)))
                for acc, k in zip(accs, ks):
                    dw_ref[k:k + 1, ls] += jnp.sum(acc, axis=0, keepdims=True)
                if k0 == 0:
                    db_ref[:, ls] += jnp.sum(accs[-1], axis=0, keepdims=True)

    dspecs = _conv_specs(0, T, tT, cb)
    uspecs = _conv_specs(blk0, T, tT, cb)
    in_specs = dspecs + uspecs + [pl.BlockSpec((K, cb), lambda j, i: (0, j))]
    args = [dy, dy, dy, u, u, u, w]
    aliases = {}
    if into is None:
        du_spec = pl.BlockSpec((tT, cb), lambda j, i: (i, j))
        du_shape = jax.ShapeDtypeStruct((T, C), F32)
    else:
        arr, oblk = into
        in_specs.append(pl.BlockSpec(memory_space=pl.ANY))
        args.append(arr)
        aliases = {7: 0}
        du_spec = pl.BlockSpec((tT, cb), lambda j, i: (i, oblk + j))
        du_shape = jax.ShapeDtypeStruct(arr.shape, arr.dtype)
    return pl.pallas_call(
        body, out_shape=[du_shape, jax.ShapeDtypeStruct((K, C), F32), jax.ShapeDtypeStruct((1, C), F32)],
        grid=(C // cb, n_t), in_specs=in_specs,
        out_specs=[du_spec, pl.BlockSpec((K, cb), lambda j, i: (0, j)), pl.BlockSpec((1, cb), lambda j, i: (0, j))],
        scratch_shapes=[pltpu.VMEM((tT + 2 * HALO, cb), F32), pltpu.VMEM((tT + 2 * HALO, cb), F32),
                        pltpu.VMEM((SUBLANE, tT + 3 * SUBLANE, cb), F32), pltpu.VMEM((SUBLANE, tT + 3 * SUBLANE, cb), F32)],
        input_output_aliases=aliases, name=name, compiler_params=_cparams(("arbitrary", "arbitrary")))(*args)


def _dot(a, b, dims):
    return lax.dot_general(a.astype(BF16), b.astype(BF16), (dims, ((), ())), preferred_element_type=F32)


def _dnn(a, b):
    return _dot(a, b, ((1,), (0,)))


def _dnt(a, b):
    return _dot(a, b, ((1,), (1,)))


def _dtn(a, b):
    return _dot(a.T, b, ((1,), (0,)))


@jax.custom_vjp
def _nn(a, b):
    return _dnn(a, b)


_nn.defvjp(lambda a, b: (_dnn(a, b), (a, b)), lambda r, g: (_dnt(g, r[1]), _dtn(r[0], g)))


@jax.custom_vjp
def _nt(a, b):
    return _dnt(a, b)


_nt.defvjp(lambda a, b: (_dnt(a, b), (a, b)), lambda r, g: (_dnn(g, r[1]), _dtn(g, r[0])))


@jax.custom_vjp
def _tn(a, b):
    return _dtn(a, b)


_tn.defvjp(lambda a, b: (_dtn(a, b), (a, b)), lambda r, g: (_dnt(r[1], g), _dnn(r[0], g)))


def _split_dot(m, v):
    hi = v.astype(BF16)
    r1 = v - hi.astype(F32)
    mid = r1.astype(BF16)
    lo = (r1 - mid.astype(F32)).astype(BF16)
    mb = m.astype(BF16)
    d = lambda x: lax.dot_general(mb, x, (((1,), (0,)), ((), ())), preferred_element_type=F32)
    return d(hi) + d(mid) + d(lo)


@jax.custom_vjp
def _tri_dot(tri, tri_t, v):
    return _split_dot(tri, v)


_tri_dot.defvjp(lambda tri, tri_t, v: (_split_dot(tri, v), (tri, tri_t)),
                lambda r, g: (jnp.zeros_like(r[0]), jnp.zeros_like(r[1]), _split_dot(r[1], g)))


def _pick_vjp(axis):
    def pick(v, h):
        return v[:, h:h + 1] if axis == 1 else v[h:h + 1, :]

    def fwd(v, h):
        return pick(v, h), v.shape

    def bwd(h, shape, g):
        idx = lax.broadcasted_iota(jnp.int32, shape, axis)
        return (jnp.where(idx == h, g, 0.0),)

    f = jax.custom_vjp(pick, nondiff_argnums=(1,))
    f.defvjp(fwd, bwd)
    return f


_lane_pick = _pick_vjp(1)
_sub_pick = _pick_vjp(0)


def _onehot_lane(v, h):
    lane = lax.broadcasted_iota(jnp.int32, (1, v.shape[1]), 1)
    return jnp.sum(v * (lane == h).astype(F32), axis=1, keepdims=True)


def _onehot_sub(v, h):
    sub = lax.broadcasted_iota(jnp.int32, (v.shape[0], 1), 0)
    return jnp.sum(v * (sub == h).astype(F32), axis=0, keepdims=True)


def _ssd_consts(dirn, picks):
    ri = lax.broadcasted_iota(jnp.int32, (CHUNK, CHUNK), 0)
    ci = lax.broadcasted_iota(jnp.int32, (CHUNK, CHUNK), 1)
    keep = (ci <= ri) if dirn == 0 else (ci >= ri)
    tri = keep.astype(F32)
    tri_t = (~keep | (ci == ri)).astype(F32)
    lane = lax.broadcasted_iota(jnp.int32, (1, LANE), 1)
    sub = lax.broadcasted_iota(jnp.int32, (CHUNK, 1), 0)
    end = (sub == (CHUNK - 1 if dirn == 0 else 0)).astype(F32)
    lo_half = lane < HEAD_DIM
    pick = (_lane_pick, _sub_pick) if picks else (_onehot_lane, _onehot_sub)
    return keep, tri, tri_t, end, lo_half, N_HEADS * dirn, pick


def _ssd_chunk(consts, x_t, b_t, c_t, dtr, bias, alog, h_t):
    keep, tri, tri_t, end, lo_half, h_base, (lane_pick, sub_pick) = consts
    dt = _softplus(dtr + bias)
    a = dt * (-jnp.exp(alog))
    cs = _tri_dot(tri, tri_t, a)
    cs_t = cs.T
    tot = jnp.sum(cs * end, axis=0, keepdims=True)
    ys, hn = [], []
    for g in range(N_GROUPS):
        bm, cm = _silu(b_t[g]), _silu(c_t[g])
        gm = _nt(cm, bm)
        for jj in range(2):
            j = 2 * g + jj
            hh = (h_base + 2 * j, h_base + 2 * j + 1)
            col = [lane_pick(cs, h) for h in hh]
            row = [sub_pick(cs_t, h) for h in hh]
            dth = [lane_pick(dt, h) for h in hh]
            toth = [lane_pick(tot, h) for h in hh]
            xd = _silu(x_t[j]) * jnp.where(lo_half, dth[0], dth[1])
            yd = [_nn(gm * jnp.exp(jnp.where(keep, col[k] - row[k], NEG)), xd) for k in range(2)]
            cp = jnp.where(lo_half, col[0], col[1])
            tp = jnp.where(lo_half, toth[0], toth[1])
            ys.append(jnp.where(lo_half, yd[0], yd[1]) + _nn(cm, h_t[j]) * jnp.exp(cp))
            hn.append(h_t[j] * jnp.exp(tp) + _tn(bm, xd * jnp.exp(tp - cp)))
    return ys, hn


N_PAIR = D_INNER // LANE


def _tiles(ref, n):
    return [ref[:, LANE * j:LANE * (j + 1)].astype(F32) for j in range(n)]


def _ssd_in_specs(cmap):
    return [pl.BlockSpec((CHUNK, D_INNER), lambda i: (cmap(i), 0)),
            pl.BlockSpec((CHUNK, N_GROUPS * D_STATE), lambda i: (cmap(i), 2)),
            pl.BlockSpec((CHUNK, N_GROUPS * D_STATE), lambda i: (cmap(i), 3)),
            pl.BlockSpec((CHUNK, LANE), lambda i: (cmap(i), 0)),
            pl.BlockSpec((1, LANE), lambda i: (0, 0)), pl.BlockSpec((1, LANE), lambda i: (0, 0))]


def ssd_fwd(name, xbc, pdt, bias_row, alog_row, dirn, T):
    nc = T // CHUNK
    cmap = (lambda i: i) if dirn == 0 else (lambda i: nc - 1 - i)

    def body(x_ref, b_ref, c_ref, dt_ref, bias_ref, alog_ref, y_ref, hs_ref, h_scr):
        @pl.when(pl.program_id(0) == 0)
        def _():
            h_scr[...] = jnp.zeros(h_scr.shape, F32)

        hs_ref[0] = h_scr[...]
        ys, hn = _ssd_chunk(_ssd_consts(dirn, True), _tiles(x_ref, N_PAIR), _tiles(b_ref, N_GROUPS), _tiles(c_ref, N_GROUPS),
                            dt_ref[...], bias_ref[...], alog_ref[...], _tiles(h_scr, N_PAIR))
        for j in range(N_PAIR):
            y_ref[:, LANE * j:LANE * (j + 1)] = ys[j]
            h_scr[:, LANE * j:LANE * (j + 1)] = hn[j]

    return pl.pallas_call(
        body, out_shape=[jax.ShapeDtypeStruct((T, D_INNER), F32), jax.ShapeDtypeStruct((nc, D_STATE, D_INNER), F32)],
        grid=(nc,), in_specs=_ssd_in_specs(cmap),
        out_specs=[pl.BlockSpec((CHUNK, D_INNER), lambda i: (cmap(i), 0)),
                   pl.BlockSpec((1, D_STATE, D_INNER), lambda i: (cmap(i), 0, 0))],
        scratch_shapes=[pltpu.VMEM((D_STATE, D_INNER), F32)], name=name,
        compiler_params=_cparams(("arbitrary",)))(xbc, xbc, xbc, pdt, bias_row, alog_row)


def ssd_bwd(name, xbc, pdt, bias_row, alog_row, hs, dy, adds, dirn, T):
    nc = T // CHUNK
    cmap = (lambda i: nc - 1 - i) if dirn == 0 else (lambda i: i)
    GS = N_GROUPS * D_STATE

    n_add = len(adds)

    def body(x_ref, b_ref, c_ref, dt_ref, bias_ref, alog_ref, hs_ref, dy_ref, *rest):
        add_refs, (dx_ref, db_ref, dc_ref, ddt_ref, dbias_ref, dalog_ref, dh_scr) = rest[:n_add], rest[n_add:]
        ax_ref = add_refs[0]
        ab_ref, ac_ref, adt_ref = add_refs[1:] if n_add == 4 else (None, None, None)
        first = pl.program_id(0) == 0

        @pl.when(first)
        def _():
            dh_scr[...] = jnp.zeros(dh_scr.shape, F32)
            dbias_ref[...] = jnp.zeros(dbias_ref.shape, F32)
            dalog_ref[...] = jnp.zeros(dalog_ref.shape, F32)

        consts = _ssd_consts(dirn, False)
        fn = lambda *a: _ssd_chunk(consts, *a)
        _, vjp_fn = jax.vjp(fn, _tiles(x_ref, N_PAIR), _tiles(b_ref, N_GROUPS), _tiles(c_ref, N_GROUPS), dt_ref[...],
                            bias_ref[...], alog_ref[...], [hs_ref[0, :, LANE * j:LANE * (j + 1)] for j in range(N_PAIR)])
        dx, db, dc, ddt, dbias, dalog, dh = vjp_fn((_tiles(dy_ref, N_PAIR), _tiles(dh_scr, N_PAIR)))
        for j in range(N_PAIR):
            s = slice(LANE * j, LANE * (j + 1))
            dx_ref[:, s] = dx[j] + ax_ref[:, s]
            dh_scr[:, s] = dh[j]
        for g in range(N_GROUPS):
            s = slice(LANE * g, LANE * (g + 1))
            db_ref[:, s] = db[g] + (ab_ref[:, s] if n_add == 4 else 0.0)
            dc_ref[:, s] = dc[g] + (ac_ref[:, s] if n_add == 4 else 0.0)
        ddt_ref[...] = ddt + (adt_ref[...] if n_add == 4 else 0.0)
        dbias_ref[...] += dbias
        dalog_ref[...] += dalog

    blk = lambda w: pl.BlockSpec((CHUNK, w), lambda i: (cmap(i), 0))
    row = pl.BlockSpec((1, LANE), lambda i: (0, 0))
    return pl.pallas_call(
        body,
        out_shape=[jax.ShapeDtypeStruct((T, D_INNER), F32), jax.ShapeDtypeStruct((T, GS), F32),
                   jax.ShapeDtypeStruct((T, GS), F32), jax.ShapeDtypeStruct((T, LANE), F32),
                   jax.ShapeDtypeStruct((1, LANE), F32), jax.ShapeDtypeStruct((1, LANE), F32)],
        grid=(nc,),
        in_specs=_ssd_in_specs(cmap) + [pl.BlockSpec((1, D_STATE, D_INNER), lambda i: (cmap(i), 0, 0)), blk(D_INNER)]
        + [blk(D_INNER), blk(GS), blk(GS), blk(LANE)][:n_add],
        out_specs=[blk(D_INNER), blk(GS), blk(GS), blk(LANE), row, row],
        scratch_shapes=[pltpu.VMEM((D_STATE, D_INNER), F32)], name=name,
        compiler_params=_cparams(("arbitrary",)))(xbc, xbc, xbc, pdt, bias_row, alog_row, hs, dy, *adds)


def loss_head(y, target, T, tT=256):
    def body(y_ref, t_ref, dy_ref, sq_ref):
        @pl.when(pl.program_id(0) == 0)
        def _():
            sq_ref[...] = jnp.zeros(sq_ref.shape, F32)
        e = y_ref[...] - t_ref[...]
        dy_ref[...] = e * (1.0 / D_MODEL)
        sq_ref[...] += jnp.sum(e * e, axis=0, keepdims=True)

    spec = pl.BlockSpec((tT, D_MODEL), lambda i: (i, 0))
    return pl.pallas_call(
        body, out_shape=[jax.ShapeDtypeStruct((T, D_MODEL), F32), jax.ShapeDtypeStruct((1, D_MODEL), F32)],
        grid=(T // tT,), in_specs=[spec, spec], out_specs=[spec, pl.BlockSpec((1, D_MODEL), lambda i: (0, 0))],
        name="loss_head", compiler_params=_cparams(("arbitrary",)))(y, target)


MESH_ID = pl.DeviceIdType.MESH


def all_gather(name, v):
    R, W = v.shape

    def body(v_ref, out_ref, send_sems, recv_sems, local_sem):
        x, y, c = lax.axis_index("x"), lax.axis_index("y"), lax.axis_index("c")
        me, sibling = (x, y, c), (x, y, 1 - c)
        chips = [(1 - x, y), (x, 1 - y), (1 - x, 1 - y)]

        def slot(px, py, pc):
            return out_ref.at[4 * px + 2 * py + pc]

        def copy(k, block, to, src=None):
            return pltpu.make_async_remote_copy(
                src_ref=slot(*block) if src is None else src, dst_ref=slot(*block), send_sem=send_sems.at[k],
                recv_sem=recv_sems.at[k], device_id=to, device_id_type=MESH_ID)

        mine = pltpu.make_async_copy(v_ref, slot(*me), local_sem)
        mine.start()
        first = [copy(0, me, sibling, src=v_ref)]
        first += [copy(1 + j, me, (*chip, c), src=v_ref) for j, chip in enumerate(chips)]
        for cp in first:
            cp.start()
        passed = [copy(4 + j, (*chip, c), sibling) for j, chip in enumerate(chips)]
        for j, chip in enumerate(chips):
            copy(1 + j, (*chip, c), me).wait_recv()
            passed[j].start()
        copy(0, sibling, me).wait_recv()
        for j, chip in enumerate(chips):
            copy(4 + j, (*chip, 1 - c), me).wait_recv()
        for cp in first + passed:
            cp.wait_send()
        mine.wait()

    return pl.pallas_call(
        body, out_shape=jax.ShapeDtypeStruct((N_DEV, R, W), v.dtype),
        in_specs=[pl.BlockSpec(memory_space=pl.ANY)], out_specs=pl.BlockSpec(memory_space=pl.ANY),
        scratch_shapes=[pltpu.SemaphoreType.DMA((7,)), pltpu.SemaphoreType.DMA((7,)), pltpu.SemaphoreType.DMA],
        name=name, compiler_params=pltpu.CompilerParams(has_side_effects=True))(v)


def exchange_sibling(name, g):
    _, R, W = g.shape

    def body(g_ref, out_ref, send_sems, recv_sems):
        x, y, c = lax.axis_index("x"), lax.axis_index("y"), lax.axis_index("c")
        copies = [pltpu.make_async_remote_copy(
            src_ref=g_ref.at[2 * q + (1 - c)], dst_ref=out_ref.at[q], send_sem=send_sems.at[q], recv_sem=recv_sems.at[q],
            device_id=(x, y, 1 - c), device_id_type=MESH_ID) for q in range(4)]
        for cp in copies:
            cp.start()
        for cp in copies:
            cp.wait_recv()
        for cp in copies:
            cp.wait_send()

    return pl.pallas_call(
        body, out_shape=jax.ShapeDtypeStruct((4, R, W), g.dtype),
        in_specs=[pl.BlockSpec(memory_space=pl.ANY)], out_specs=pl.BlockSpec(memory_space=pl.ANY),
        scratch_shapes=[pltpu.SemaphoreType.DMA((4,)), pltpu.SemaphoreType.DMA((4,))],
        name=name, compiler_params=pltpu.CompilerParams(has_side_effects=True))(g)


def pair_sum(name, g, sib, core, tr):
    _, R, W = g.shape

    def body(core_ref, g_ref, s_ref, o_ref):
        o_ref[...] = (g_ref[...].astype(F32) + s_ref[...].astype(F32)).astype(o_ref.dtype)

    blk = pl.BlockSpec((1, tr, W), lambda q, i, core_ref: (q, i, 0))
    return pl.pallas_call(
        body, out_shape=jax.ShapeDtypeStruct((4, R, W), g.dtype),
        grid_spec=pltpu.PrefetchScalarGridSpec(
            num_scalar_prefetch=1, grid=(4, R // tr),
            in_specs=[pl.BlockSpec((1, tr, W), lambda q, i, core_ref: (2 * q + core_ref[0], i, 0)), blk], out_specs=blk),
        name=name, compiler_params=_cparams(("parallel", "parallel")))(core, g, sib)


def exchange_chips(name, s):
    _, R, W = s.shape

    def body(s_ref, out_ref, send_sems, recv_sems, local_sem):
        x, y, c = lax.axis_index("x"), lax.axis_index("y"), lax.axis_index("c")
        mine = 2 * x + y
        own = pltpu.make_async_copy(s_ref.at[mine], out_ref.at[mine], local_sem)
        own.start()
        copies = []
        for k in range(1, 4):
            px = 1 - x if (k >> 1) & 1 else x
            py = 1 - y if k & 1 else y
            copies.append(pltpu.make_async_remote_copy(
                src_ref=s_ref.at[2 * px + py], dst_ref=out_ref.at[mine], send_sem=send_sems.at[k - 1],
                recv_sem=recv_sems.at[k - 1], device_id=(px, py, c), device_id_type=MESH_ID))
        for cp in copies:
            cp.start()
        for cp in copies:
            cp.wait_recv()
        for cp in copies:
            cp.wait_send()
        own.wait()

    return pl.pallas_call(
        body, out_shape=jax.ShapeDtypeStruct(s.shape, s.dtype),
        in_specs=[pl.BlockSpec(memory_space=pl.ANY)], out_specs=pl.BlockSpec(memory_space=pl.ANY),
        scratch_shapes=[pltpu.SemaphoreType.DMA((3,)), pltpu.SemaphoreType.DMA((3,)), pltpu.SemaphoreType.DMA],
        name=name, compiler_params=pltpu.CompilerParams(has_side_effects=True))(s)


def reduce_pieces(g, tr):
    sib = exchange_sibling("grad_exchange_sibling", g)
    core = lax.axis_index("c").astype(jnp.int32).reshape(1)
    pair = pair_sum("grad_pair_sum", g, sib, core, _pick(g.shape[1], 4096) if g.shape[1] % LANE == 0 else tr)
    landed = exchange_chips("grad_exchange_chips", pair)
    return sum_slots("sum_grad_pieces", landed, tr)


def sum_slots(name, parts, tr):
    n_slot, R, W = parts.shape

    def body(p_ref, o_ref):
        g = p_ref[0].astype(F32)
        for s in range(1, n_slot):
            g = g + p_ref[s].astype(F32)
        o_ref[...] = g

    return pl.pallas_call(
        body, out_shape=jax.ShapeDtypeStruct((R, W), F32), grid=(R // tr,),
        in_specs=[pl.BlockSpec((n_slot, tr, W), lambda i: (0, i, 0))], out_specs=pl.BlockSpec((tr, W), lambda i: (i, 0)),
        name=name, compiler_params=_cparams(("parallel",)))(parts)


def adamw(name, parts, w, m, v, tr):
    R, W = w.shape
    n_slot = parts.shape[0]
    c1 = 1.0 / (1.0 - ADAM_B1 ** ADAM_STEP)
    c2 = 1.0 / (1.0 - ADAM_B2 ** ADAM_STEP)

    def body(p_ref, w_ref, m_ref, v_ref, g_ref, d_ref, nm_ref, nv_ref):
        g = p_ref[0]
        for s in range(1, n_slot):
            g = g + p_ref[s]
        nm = ADAM_B1 * m_ref[...] + (1.0 - ADAM_B1) * g
        nv = ADAM_B2 * v_ref[...] + (1.0 - ADAM_B2) * (g * g)
        g_ref[...] = g
        nm_ref[...] = nm
        nv_ref[...] = nv
        d_ref[...] = -ADAM_LR * ((nm * c1) / (jnp.sqrt(nv * c2) + ADAM_EPS) + ADAM_WD * w_ref[...])

    spec = pl.BlockSpec((tr, W), lambda i: (i, 0))
    return pl.pallas_call(
        body, out_shape=[jax.ShapeDtypeStruct((R, W), F32)] * 4, grid=(R // tr,),
        in_specs=[pl.BlockSpec((n_slot, tr, W), lambda i: (0, i, 0)), spec, spec, spec], out_specs=[spec] * 4,
        name=name, compiler_params=_cparams(("parallel",)))(parts, w, m, v)


def _pack(arrs, row_mult):
    flat = jnp.concatenate([a.reshape(-1) for a in arrs])
    n = flat.shape[0]
    rows = -(-n // LANE)
    rows = -(-rows // row_mult) * row_mult
    return jnp.pad(flat, (0, rows * LANE - n)).reshape(rows, LANE)


def _unpack(buf, shapes, lead=()):
    flat = buf.reshape(lead + (-1,))
    out, off = [], 0
    for s in shapes:
        n = math.prod(s)
        out.append(flat[..., off:off + n].reshape(lead + tuple(s)))
        off += n
    return out


def _rows_sharded(name):
    return name in COL_T or BIG_AXIS[name] == 1


def _shard_for_gather(name, w):
    return jnp.swapaxes(w, 1, 2) if name in COL_T else w


def _full_from_gathered(name, g):
    if _rows_sharded(name):
        return jnp.transpose(g, (1, 0, 2, 3)).reshape(g.shape[1], N_DEV * g.shape[2], g.shape[3])
    return jnp.transpose(g, (1, 2, 0, 3)).reshape(g.shape[1], g.shape[2], N_DEV * g.shape[3])


def _pieces_from_full(name, f):
    A, B = f.shape
    if _rows_sharded(name):
        return f.reshape(N_DEV, -1)
    return jnp.transpose(f.reshape(A, N_DEV, B // N_DEV), (1, 0, 2)).reshape(N_DEV, -1)


def _piece_shape(name, shard_shape):
    L, a, b = shard_shape
    return (L, b, a) if name in COL_T else (L, a, b)


def _rows(v):
    return v.reshape(1, -1).astype(F32)


def _head_rows(W):
    bias = jnp.pad(W["dt_bias"].reshape(1, -1), ((0, 0), (0, LANE - 2 * N_HEADS)))
    alog = jnp.pad(W["a_log"].reshape(1, -1), ((0, 0), (0, LANE - 2 * N_HEADS)))
    return bias, alog


def layer_fwd(li, x, p_l, W, T):
    n = lambda s: f"l{li}_{s}"
    S = {"x": x}
    proj = matmul(n("mm_in"), x, W["w_in"], "nt", out_dtype=BF16)
    pdt = matmul(n("mm_dt"), x, W["w_in"][N_IN_PAD - LANE:], "nt")
    (u0,) = rowcall(n("glu"), glu_fn, [(proj, 1024, 0), (proj, 1024, 1)], [], [(1024, F32, None)], T)
    u1 = conv_fwd(n("conv_a"), u0, 0, W["conv_a_w"], _rows(W["conv_a_b"]), T)
    (u3,) = rowcall(n("lnsilu"), lnsilu_fn, [(u1, 1024, 0)], [_rows(W["ln_a_g"]), _rows(W["ln_a_b"])],
                    [(1024, BF16, None)], T)
    y_a = matmul(n("mm_aout"), u3, W["w_a_out"], "nn")
    xbc = conv_fwd(n("conv_s"), proj, 6144 // CONV_CB, W["ssm_conv_w"], _rows(W["ssm_conv_b"]), T)
    bias_row, alog_row = _head_rows(W)
    y_f, hs_f = ssd_fwd(n("ssd_f"), xbc, pdt, bias_row, alog_row, 0, T)
    y_b, hs_b = ssd_fwd(n("ssd_r"), xbc, pdt, bias_row, alog_row, 1, T)
    dsk = jnp.repeat(W["d_skip"], HEAD_DIM).reshape(1, D_INNER)
    (yn,) = rowcall(n("gnorm"), gnorm_fn, [(y_f, 256, 0), (y_b, 256, 0), (xbc, 256, 0), (proj, 256, 16)],
                    [dsk, _rows(W["ssm_norm_g"])], [(256, BF16, None)], T, groups=N_GROUPS)
    y_bo = matmul(n("mm_bout"), yn, W["w_b_out"], "nn")
    (merged,) = rowcall(n("merge"), merge_fn, [(proj, 1024, 2), (proj, 1024, 3), (y_a, 1024, 0), (y_bo, 1024, 0)], [],
                        [(1024, BF16, None)], T)
    mix = matmul(n("mm_o"), merged, W["w_o"], "nn")
    (h,) = rowcall(n("ln1"), resln_fn, [(x, 1024, 0), (mix, 1024, 0)], [_rows(W["ln1_g"]), _rows(W["ln1_b"])],
                   [(1024, F32, None)], T)
    gu = matmul(n("mm_gu"), h, W["w_gate_up"], "nt", out_dtype=BF16)
    (act,) = rowcall(n("swiglu"), swiglu_fn, [(gu, FFN_DIM, 0), (gu, FFN_DIM, 1)], [], [(FFN_DIM, BF16, None)], T)
    dn = matmul(n("mm_down"), act, W["w_down"], "nn")
    (h2,) = rowcall(n("ln2"), resln_fn, [(h, 1024, 0), (dn, 1024, 0)], [_rows(W["ln2_g"]), _rows(W["ln2_b"])],
                    [(1024, F32, None)], T)
    pe = matmul(n("mm_ple"), p_l, W["w_ple"], "nn")
    gl = matmul(n("mm_pg"), h2, W["w_ple_gate"], "nn")
    (xn,) = rowcall(n("pleout"), ple_fn, [(h2, 1024, 0), (pe, 1024, 0), (gl, 1024, 0)], [_rows(W["ple_norm_g"])],
                    [(1024, F32, None)], T)
    S.update(proj=proj, pdt=pdt, u0=u0, u1=u1, u3=u3, y_a=y_a, xbc=xbc, y_f=y_f, y_b=y_b, hs_f=hs_f, hs_b=hs_b, yn=yn, y_bo=y_bo,
             merged=merged, mix=mix, h=h, gu=gu, act=act, dn=dn, h2=h2, pe=pe, gl=gl, dsk=dsk, bias_row=bias_row,
             alog_row=alog_row)
    return xn, S


def layer_bwd(li, dxn, p_l, W, S, T):
    n = lambda s: f"l{li}_{s}"
    G = {}
    x, proj = S["x"], S["proj"]
    (dh2a, dpe, dgl), (dpg,) = rowvjp(
        n("pleout_b"), ple_fn, [(S["h2"], 1024, 0), (S["pe"], 1024, 0), (S["gl"], 1024, 0)], [_rows(W["ple_norm_g"])],
        [(dxn, 1024, 0)], [([0], F32, None), ([1], BF16, None), ([2], BF16, None)], T)
    G["ple_norm_g"] = dpg
    G["w_ple_gate"] = matmul(n("mm_pg_w"), S["h2"], dgl, "tn", out_dtype=BF16)
    G["w_ple"] = matmul(n("mm_ple_w"), p_l, dpe, "tn", out_dtype=BF16)
    dh2 = matmul(n("mm_pg_x"), dgl, W["w_ple_gate"], "nt", add=dh2a)
    (dha, ddn), (G["ln2_g"], G["ln2_b"]) = rowvjp(
        n("ln2_b"), resln_fn, [(S["h"], 1024, 0), (S["dn"], 1024, 0)], [_rows(W["ln2_g"]), _rows(W["ln2_b"])],
        [(dh2, 1024, 0)], [([0], F32, None), ([1], BF16, None)], T)
    G["w_down"] = matmul(n("mm_down_w"), S["act"], ddn, "tn", out_dtype=BF16)
    dact = matmul(n("mm_down_x"), ddn, W["w_down"], "nt", out_dtype=BF16)
    (dgu,), _ = rowvjp(n("swiglu_b"), swiglu_fn, [(S["gu"], FFN_DIM, 0), (S["gu"], FFN_DIM, 1)], [],
                       [(dact, FFN_DIM, 0)], [([0, 1], BF16, None)], T)
    G["w_gate_up"] = matmul(n("mm_gu_w"), dgu, S["h"], "tn", out_dtype=BF16)
    dh = matmul(n("mm_gu_x"), dgu, W["w_gate_up"], "nn", add=dha)
    (dxa, dmix), (G["ln1_g"], G["ln1_b"]) = rowvjp(
        n("ln1_b"), resln_fn, [(x, 1024, 0), (S["mix"], 1024, 0)], [_rows(W["ln1_g"]), _rows(W["ln1_b"])],
        [(dh, 1024, 0)], [([0], F32, None), ([1], BF16, None)], T)
    G["w_o"] = matmul(n("mm_o_w"), S["merged"], dmix, "tn", out_dtype=BF16)
    dmerged = matmul(n("mm_o_x"), dmix, W["w_o"], "nt")
    dproj = jax.ShapeDtypeStruct((T, N_IN_PAD), BF16)
    (dproj, dy_a, dy_bo), _ = rowvjp(
        n("merge_b"), merge_fn, [(proj, 1024, 2), (proj, 1024, 3), (S["y_a"], 1024, 0), (S["y_bo"], 1024, 0)], [],
        [(dmerged, 1024, 0)], [([0, 1], BF16, (dproj, 1)), ([2], BF16, None), ([3], BF16, None)], T)
    G["w_a_out"] = matmul(n("mm_aout_w"), S["u3"], dy_a, "tn", out_dtype=BF16)
    du3 = matmul(n("mm_aout_x"), dy_a, W["w_a_out"], "nt")
    (du1,), (G["ln_a_g"], G["ln_a_b"]) = rowvjp(
        n("lnsilu_b"), lnsilu_fn, [(S["u1"], 1024, 0)], [_rows(W["ln_a_g"]), _rows(W["ln_a_b"])], [(du3, 1024, 0)],
        [([0], F32, None)], T)
    du0, G["conv_a_w"], G["conv_a_b"] = conv_bwd(n("conv_a_b"), du1, S["u0"], 0, W["conv_a_w"], T)
    (dproj,), _ = rowvjp(n("glu_b"), glu_fn, [(proj, 1024, 0), (proj, 1024, 1)], [], [(du0, 1024, 0)],
                         [([0, 1], BF16, (dproj, 0))], T)
    G["w_b_out"] = matmul(n("mm_bout_w"), S["yn"], dy_bo, "tn", out_dtype=BF16)
    dyn = matmul(n("mm_bout_x"), dy_bo, W["w_b_out"], "nt")
    (dys, dxs, dproj), (ddsk, G["ssm_norm_g"]) = rowvjp(
        n("gnorm_b"), gnorm_fn, [(S["y_f"], 256, 0), (S["y_b"], 256, 0), (S["xbc"], 256, 0), (proj, 256, 16)],
        [S["dsk"], _rows(W["ssm_norm_g"])], [(dyn, 256, 0)],
        [([0], F32, None), ([2], F32, None), ([3], BF16, (dproj, 16))], T, groups=N_GROUPS)
    G["d_skip"] = ddsk.reshape(N_HEADS, HEAD_DIM).sum(axis=1)
    dx1, db1, dc1, ddt1, dbias_f, dalog_f = ssd_bwd(
        n("ssd_f_b"), S["xbc"], S["pdt"], S["bias_row"], S["alog_row"], S["hs_f"], dys, (dxs,), 0, T)
    dxx, dbb, dcc, ddt, dbias_r, dalog_r = ssd_bwd(
        n("ssd_r_b"), S["xbc"], S["pdt"], S["bias_row"], S["alog_row"], S["hs_b"], dys, (dx1, db1, dc1, ddt1), 1, T)
    G["dt_bias"] = (dbias_f + dbias_r)[0, :2 * N_HEADS].reshape(2, N_HEADS)
    G["a_log"] = (dalog_f + dalog_r)[0, :2 * N_HEADS].reshape(2, N_HEADS)
    cw = W["ssm_conv_w"]
    b0 = 6144 // CONV_CB
    dproj, dwx, dbx = conv_bwd(n("conv_sx_b"), dxx, proj, b0, cw[:, :D_INNER], T, into=(dproj, b0))
    dproj, dwb, dbb_ = conv_bwd(n("conv_sb_b"), dbb, proj, b0 + 4, cw[:, D_INNER:D_INNER + 1024], T, into=(dproj, b0 + 4))
    dproj, dwc, dbc = conv_bwd(n("conv_sc_b"), dcc, proj, b0 + 6, cw[:, D_INNER + 1024:], T, into=(dproj, b0 + 6))
    G["ssm_conv_w"] = jnp.concatenate([dwx, dwb, dwc], axis=1)
    G["ssm_conv_b"] = jnp.concatenate([dbx, dbb_, dbc], axis=1)
    (dproj,) = rowcall(n("dt_cast"), ident_fn, [(ddt, LANE, 0)], [], [(LANE, BF16, (dproj, (N_IN_PAD - LANE) // LANE))], T)
    G["w_in"] = matmul(n("mm_in_w"), dproj, x, "tn", out_dtype=BF16)[:N_IN]
    dx = matmul(n("mm_in_x"), dproj, W["w_in"], "nn", add=dxa)
    return dx, G


def local_step(x, p, loss_target, FW, T):
    Ws, saves = [], []
    cur = x
    for li in range(DEPTH):
        W = {k: v[li] for k, v in FW.items()}
        Ws.append(W)
        cur, S = layer_fwd(li, cur, p[li], W, T)
        saves.append(S)
    dcur, sq = loss_head(cur, loss_target, T)
    loss = 0.5 * jnp.sum(sq) / D_MODEL
    grads = [None] * DEPTH
    for li in reversed(range(DEPTH)):
        dcur, grads[li] = layer_bwd(li, dcur, p[li], Ws[li], saves[li], T)
    return loss, dcur, grads


def kernel(x, p, w_in, conv_a_w, conv_a_b, ln_a_g, ln_a_b, w_a_out, ssm_conv_w, ssm_conv_b, a_log, dt_bias, d_skip, ssm_norm_g, w_b_out, w_o, ln1_g, ln1_b, w_gate_up, w_down, ln2_g, ln2_b, w_ple, ple_norm_g, w_ple_gate, loss_target, m_w_in, m_conv_a_w, m_conv_a_b, m_ln_a_g, m_ln_a_b, m_w_a_out, m_ssm_conv_w, m_ssm_conv_b, m_a_log, m_dt_bias, m_d_skip, m_ssm_norm_g, m_w_b_out, m_w_o, m_ln1_g, m_ln1_b, m_w_gate_up, m_w_down, m_ln2_g, m_ln2_b, m_w_ple, m_ple_norm_g, m_w_ple_gate, v_w_in, v_conv_a_w, v_conv_a_b, v_ln_a_g, v_ln_a_b, v_w_a_out, v_ssm_conv_w, v_ssm_conv_b, v_a_log, v_dt_bias, v_d_skip, v_ssm_norm_g, v_w_b_out, v_w_o, v_ln1_g, v_ln1_b, v_w_gate_up, v_w_down, v_ln2_g, v_ln2_b, v_w_ple, v_ple_norm_g, v_w_ple_gate):
    A = dict(locals())
    w = {k: A[k] for k in WEIGHTS}
    m = {k: A["m_" + k] for k in WEIGHTS}
    v = {k: A["v_" + k] for k in WEIGHTS}
    T = x.shape[1]
    small_shapes = [w[k].shape for k in SMALL]

    mm_names = [k for k in BIG if k not in CONV_W]
    shards = [_shard_for_gather(k, w[k].astype(BF16)) for k in mm_names]
    gathered = all_gather("gather_weights", _pack(shards, 16))
    FW = {k: _full_from_gathered(k, g) for k, g in zip(mm_names, _unpack(gathered, [t.shape for t in shards], (N_DEV,)))}
    FW["w_in"] = jnp.pad(FW["w_in"], ((0, 0), (0, N_IN_PAD - N_IN), (0, 0)))
    gathered = all_gather("gather_conv_weights", _pack([w[k] for k in CONV_W], SUBLANE))
    FW.update({k: _full_from_gathered(k, g)
               for k, g in zip(CONV_W, _unpack(gathered, [w[k].shape for k in CONV_W], (N_DEV,)))})
    FW.update({k: w[k] for k in SMALL})

    loss, grad_x, gfull = local_step(x[0], p[:, 0], loss_target[0], FW, T)
    loss = lax.psum(loss, ("x", "y", "c"))

    TR = 512
    flat = jnp.concatenate([_pieces_from_full(k, gfull[li][k]).astype(BF16) for k in BIG for li in range(DEPTH)], axis=1)
    rows = -(-flat.shape[1] // (LANE * TR)) * TR
    gpack = jnp.pad(flat, ((0, 0), (0, rows * LANE - flat.shape[1]))).reshape(N_DEV, rows, LANE)
    gsum = reduce_pieces(gpack, TR)
    gshard = [jnp.swapaxes(g, 1, 2) if k in COL_T else g
              for k, g in zip(BIG, _unpack(gsum, [_piece_shape(k, w[k].shape) for k in BIG]))]
    res_big = [{}, {}, {}, {}]
    for k, g in zip(BIG, gshard):
        two_d = lambda t: t.reshape(-1, t.shape[-1])
        rows_k = two_d(w[k]).shape[0]
        tr = max(d for d in range(1, min(rows_k, 256) + 1) if rows_k % d == 0 and (d % SUBLANE == 0 or d == rows_k))
        res = adamw("adamw_" + k, two_d(g)[None], two_d(w[k]), two_d(m[k]), two_d(v[k]), tr)
        for q in range(4):
            res_big[q][k] = res[q].reshape(w[k].shape)

    spack = _pack([jnp.stack([gfull[li][k] for li in range(DEPTH)]).reshape(w[k].shape) for k in SMALL], SUBLANE)
    sall = all_gather("gather_small_grads", spack)
    rs = spack.shape[0]
    res_small = adamw("adamw_small", sall, _pack([w[k] for k in SMALL], SUBLANE), _pack([m[k] for k in SMALL], SUBLANE),
                      _pack([v[k] for k in SMALL], SUBLANE), rs)
    res_small = [dict(zip(SMALL, _unpack(r, small_shapes))) for r in res_small]

    outs = [loss, grad_x[None]]
    for q in range(4):
        for k in WEIGHTS:
            outs.append(res_big[q][k] if k in res_big[q] else res_small[q][k])
    return tuple(outs)
```

```python
import math

import jax
import jax.numpy as jnp
from jax import lax
from jax.experimental import pallas as pl
from jax.experimental.pallas import tpu as pltpu

F32 = jnp.float32
BF16 = jnp.bfloat16

D_MODEL = 1024
CONV_DIM = 1024
CONV_KERNEL = 31
D_INNER = 2048
HEAD_DIM = 64
N_HEADS = 32
N_GROUPS = 8
D_STATE = 128
SSM_CONV = 5
CHUNK = 128
XBC_DIM = D_INNER + 2 * N_GROUPS * D_STATE
FFN_DIM = 2816
PLE_DIM = 256
N_IN = 2 * CONV_DIM + 2 * D_MODEL + D_INNER + XBC_DIM + 2 * N_HEADS
N_IN_PAD = 10368
DEPTH = 2
N_DEV = 8
ALPHA = (2 * DEPTH) ** 0.25
LN_EPS = 1e-5
RMS_EPS = 1e-6
ADAM_LR, ADAM_B1, ADAM_B2, ADAM_EPS, ADAM_WD, ADAM_STEP = 0.001, 0.9, 0.999, 1e-08, 0.01, 10

LANE = 128
SUBLANE = 8
HALO = 16
VMEM_LIMIT = 52 * 1024 * 1024
NEG = -1e30

BIG = ["w_in", "conv_a_w", "w_a_out", "ssm_conv_w", "w_b_out", "w_o", "w_gate_up", "w_down", "w_ple", "w_ple_gate"]
BIG_AXIS = {"w_in": 2, "conv_a_w": 2, "w_a_out": 1, "ssm_conv_w": 2, "w_b_out": 1, "w_o": 1, "w_gate_up": 2,
            "w_down": 1, "w_ple": 2, "w_ple_gate": 1}
COL_T = ["w_in", "w_gate_up"]
CONV_W = ["conv_a_w", "ssm_conv_w"]
SMALL = ["conv_a_b", "ln_a_g", "ln_a_b", "ssm_conv_b", "a_log", "dt_bias", "d_skip", "ssm_norm_g", "ln1_g", "ln1_b",
         "ln2_g", "ln2_b", "ple_norm_g"]
WEIGHTS = ["w_in", "conv_a_w", "conv_a_b", "ln_a_g", "ln_a_b", "w_a_out", "ssm_conv_w", "ssm_conv_b", "a_log", "dt_bias",
           "d_skip", "ssm_norm_g", "w_b_out", "w_o", "ln1_g", "ln1_b", "w_gate_up", "w_down", "ln2_g", "ln2_b", "w_ple",
           "ple_norm_g", "w_ple_gate"]


def _cparams(sem):
    return pltpu.CompilerParams(dimension_semantics=sem, vmem_limit_bytes=VMEM_LIMIT)


def _pick(n, cap):
    if n <= cap:
        return n
    best = None
    for d in range(LANE, cap + 1, LANE):
        if n % d == 0:
            best = d
    assert best is not None, (n, cap)
    return best


def matmul(name, a, b, mode, out_dtype=F32, add=None):
    if mode == "nn":
        (M, K), (K2, N) = a.shape, b.shape
    elif mode == "nt":
        (M, K), (N, K2) = a.shape, b.shape
    else:
        (K, M), (K2, N) = a.shape, b.shape
    assert K == K2, (name, a.shape, b.shape)
    tm = _pick(M, 1024) if mode != "tn" else _pick(M, 1408)
    tn = _pick(N, 1408)
    tk = _pick(K, 2048) if mode == "tn" else (K if (mode == "nn" and K <= 2816) else _pick(K, 1408))
    if mode == "nn" and K > 2816:
        tk, tn = _pick(K, 3456), _pick(N, 512)
    nk = K // tk
    grid = (M // tm, N // tn, nk)
    if mode == "tn":
        a_spec = pl.BlockSpec((tk, tm), lambda i, j, k: (k, i))
    else:
        a_spec = pl.BlockSpec((tm, tk), lambda i, j, k: (i, k))
    if mode == "nt":
        b_spec = pl.BlockSpec((tn, tk), lambda i, j, k: (j, k))
    else:
        b_spec = pl.BlockSpec((tk, tn), lambda i, j, k: (k, j))
    o_spec = pl.BlockSpec((tm, tn), lambda i, j, k: (i, j))
    dims = {"nn": ((1,), (0,)), "nt": ((1,), (1,)), "tn": ((0,), (0,))}[mode]
    has_add = add is not None

    def body(a_ref, b_ref, *rest):
        if has_add:
            add_ref, o_ref, *scr = rest
        else:
            o_ref, *scr = rest
        part = lax.dot_general(a_ref[...].astype(BF16), b_ref[...].astype(BF16), (dims, ((), ())),
                               preferred_element_type=F32)

        def finish(v):
            if has_add:
                v = v + add_ref[...].astype(F32)
            o_ref[...] = v.astype(o_ref.dtype)

        if nk == 1:
            finish(part)
        else:
            acc = scr[0]
            k = pl.program_id(2)

            @pl.when(k == 0)
            def _():
                acc[...] = part

            @pl.when(k > 0)
            def _():
                acc[...] += part

            @pl.when(k == nk - 1)
            def _():
                finish(acc[...])

    in_specs = [a_spec, b_spec] + ([o_spec] if has_add else [])
    args = (a, b) + ((add,) if has_add else ())
    return pl.pallas_call(
        body, out_shape=jax.ShapeDtypeStruct((M, N), out_dtype), grid=grid, in_specs=in_specs, out_specs=o_spec,
        scratch_shapes=[pltpu.VMEM((tm, tn), F32)] if nk > 1 else [], name=name,
        compiler_params=_cparams(("parallel", "parallel", "arbitrary")))(*args)


def _row_specs(items, tT, groups):
    specs = []
    for (_, w, blk) in items:
        assert blk % groups == 0
        specs.append(pl.BlockSpec((tT, w * groups), (lambda i, b=blk // groups: (i, b))))
    return specs


def _slices(v, groups):
    if groups == 1:
        return [v]
    w = v.shape[1] // groups
    return [v[:, w * s:w * (s + 1)] for s in range(groups)]


def _cat(vs):
    return vs[0] if len(vs) == 1 else jnp.concatenate(vs, axis=1)


def rowcall(name, fn, ins, pars, outs, T, tT=256, groups=1):
    n_in, n_par = len(ins), len(pars)
    intos = [o[2] for o in outs if o[2] is not None]
    in_specs = (_row_specs(ins, tT, groups) + [pl.BlockSpec(p.shape, lambda i: (0, 0)) for p in pars]
                + [pl.BlockSpec(memory_space=pl.ANY)] * len(intos))
    out_specs, out_shapes, aliases = [], [], {}
    n_alias = 0
    for oi, (w, dt, into) in enumerate(outs):
        if into is None:
            out_specs.append(pl.BlockSpec((tT, w * groups), lambda i: (i, 0)))
            out_shapes.append(jax.ShapeDtypeStruct((T, w * groups), dt))
        else:
            arr, blk = into
            out_specs.append(pl.BlockSpec((tT, w * groups), lambda i, b=blk // groups: (i, b)))
            out_shapes.append(jax.ShapeDtypeStruct(arr.shape, arr.dtype))
            aliases[n_in + n_par + n_alias] = oi
            n_alias += 1

    def body(*refs):
        xs = [_slices(r[...].astype(F32), groups) for r in refs[:n_in]]
        ps = [_slices(r[...], groups) for r in refs[n_in:n_in + n_par]]
        o_refs = refs[n_in + n_par + n_alias:]
        res = [fn(*[x[s] for x in xs], *[p[s] for p in ps]) for s in range(groups)]
        for k, r in enumerate(o_refs):
            r[...] = _cat([res[s][k] for s in range(groups)]).astype(r.dtype)

    res = pl.pallas_call(
        body, out_shape=out_shapes, grid=(T // tT,), in_specs=in_specs, out_specs=out_specs,
        input_output_aliases=aliases, name=name, compiler_params=_cparams(("parallel",)))(
            *[a for (a, _, _) in ins], *pars, *[a for (a, _) in intos])
    return list(res)


def rowvjp(name, fn, ins, pars, cts, douts, T, tT=256, groups=1):
    n_in, n_par, n_ct = len(ins), len(pars), len(cts)
    intos = [o[2] for o in douts if o[2] is not None and not isinstance(o[2][0], jax.ShapeDtypeStruct)]
    in_specs = (_row_specs(ins, tT, groups) + [pl.BlockSpec(p.shape, lambda i: (0, 0)) for p in pars]
                + _row_specs(cts, tT, groups) + [pl.BlockSpec(memory_space=pl.ANY)] * len(intos))
    out_specs, out_shapes, aliases = [], [], {}
    n_alias = 0
    for oi, (idxs, dt, into) in enumerate(douts):
        w = sum(ins[k][1] for k in idxs) * groups
        if into is None:
            out_specs.append(pl.BlockSpec((tT, w), lambda i: (i, 0)))
            out_shapes.append(jax.ShapeDtypeStruct((T, w), dt))
        else:
            assert len(idxs) == 1 or groups == 1
            arr, blk = into
            out_specs.append(pl.BlockSpec((tT, w), lambda i, b=blk // groups: (i, b)))
            out_shapes.append(jax.ShapeDtypeStruct(arr.shape, arr.dtype))
            if not isinstance(arr, jax.ShapeDtypeStruct):
                aliases[n_in + n_par + n_ct + n_alias] = oi
                n_alias += 1
    n_dout = len(douts)
    for p in pars:
        out_specs.append(pl.BlockSpec(p.shape, lambda i: (0, 0)))
        out_shapes.append(jax.ShapeDtypeStruct(p.shape, F32))

    def body(*refs):
        xs = [_slices(r[...].astype(F32), groups) for r in refs[:n_in]]
        ps = [_slices(r[...], groups) for r in refs[n_in:n_in + n_par]]
        cs = [_slices(r[...].astype(F32), groups) for r in refs[n_in + n_par:n_in + n_par + n_ct]]
        o_refs = refs[n_in + n_par + n_ct + n_alias:]
        grads = []
        for s in range(groups):
            _, vjp_fn = jax.vjp(fn, *[x[s] for x in xs], *[p[s] for p in ps])
            grads.append(vjp_fn(tuple(c[s] for c in cs)))
        for r, (idxs, _, _) in zip(o_refs[:n_dout], douts):
            r[...] = _cat([grads[s][k] for k in idxs for s in range(groups)]).astype(r.dtype)
        for k, r in enumerate(o_refs[n_dout:]):
            @pl.when(pl.program_id(0) == 0)
            def _(r=r):
                r[...] = jnp.zeros(r.shape, F32)
            r[...] += _cat([grads[s][n_in + k] for s in range(groups)])

    res = pl.pallas_call(
        body, out_shape=out_shapes, grid=(T // tT,), in_specs=in_specs, out_specs=out_specs,
        input_output_aliases=aliases, name=name, compiler_params=_cparams(("arbitrary",)))(
            *[a for (a, _, _) in ins], *pars, *[a for (a, _, _) in cts], *[a for (a, _) in intos])
    res = list(res)
    return res[:n_dout], res[n_dout:]


def _sigmoid(x):
    return 1.0 / (1.0 + jnp.exp(-x))


def _silu(x):
    return x * _sigmoid(x)


def _softplus(x):
    return jnp.maximum(x, 0.0) + jnp.log(1.0 + jnp.exp(-jnp.abs(x)))


def _ln(x, g, b):
    mu = jnp.mean(x, axis=-1, keepdims=True)
    xc = x - mu
    var = jnp.mean(xc * xc, axis=-1, keepdims=True)
    return xc * lax.rsqrt(var + LN_EPS) * g + b


def glu_fn(a, gt):
    return (a * _sigmoid(gt),)


def lnsilu_fn(u, g, b):
    return (_silu(_ln(u, g, b)),)


def gnorm_fn(yf, yb, xp, z, dsk, ng):
    y = (yf + yb + _silu(xp) * dsk) * _silu(z)
    return (y * lax.rsqrt(jnp.mean(y * y, axis=-1, keepdims=True) + RMS_EPS) * ng,)


def merge_fn(ga, gb, ya, yb):
    return (_sigmoid(ga) * ya + _sigmoid(gb) * yb,)


def resln_fn(x, r, g, b):
    return (_ln(ALPHA * x + r, g, b),)


def swiglu_fn(g, u):
    return (_silu(g) * u,)


def ple_fn(h2, pe, gl, g):
    e = pe * lax.rsqrt(jnp.mean(pe * pe, axis=-1, keepdims=True) + RMS_EPS) * g
    return (h2 + e * _sigmoid(gl),)


def ident_fn(v):
    return (v,)


CONV_CB = 512
CONV_TT = 512
CONV_TILES = 4
CONV_RB = CONV_TILES * SUBLANE
CONV_RED_TILES = 2
CONV_STATIC_MAX_K = 8
CONV_RED_TAPS = 32


def _conv_specs(blk0, T, tT, cb):
    nh = tT // HALO
    cur = pl.BlockSpec((tT, cb), lambda j, i: (i, blk0 + j))
    prev = pl.BlockSpec((HALO, cb), lambda j, i: (jnp.maximum(i * nh - 1, 0), blk0 + j))
    nxt = pl.BlockSpec((HALO, cb), lambda j, i: (jnp.minimum((i + 1) * nh, T // HALO - 1), blk0 + j))
    return [prev, cur, nxt]


def _phases(offsets):
    return sorted({off % SUBLANE for off in offsets})


def _fill_padded(pad_ref, prev_ref, cur_ref, next_ref, i, n_t, tT):
    pad_ref[pl.ds(0, HALO), :] = prev_ref[...].astype(F32) * (i > 0).astype(F32)
    pad_ref[pl.ds(HALO, tT), :] = cur_ref[...].astype(F32)
    pad_ref[pl.ds(HALO + tT, HALO), :] = next_ref[...].astype(F32) * (i < n_t - 1).astype(F32)


def _fill_shifted(sh_ref, pad_ref, phases, tT):
    for ph in phases:
        sh_ref[ph] = pad_ref[pl.ds(ph, tT + 3 * SUBLANE), :]


class _Shifted:
    def __init__(self, pad_ref, sh_ref, offsets, tT, static):
        self.pad_ref, self.sh_ref, self.static = pad_ref, sh_ref, static
        if not static:
            _fill_shifted(sh_ref, pad_ref, _phases(offsets), tT)

    def tiles(self, base, ls, off, n_tiles):
        if self.static:
            return tuple(self.pad_ref[pl.ds(base + off + SUBLANE * t, SUBLANE), ls] for t in range(n_tiles))
        q, ph = divmod(off, SUBLANE)
        return tuple(self.sh_ref[ph, pl.ds(base + SUBLANE * (q + t), SUBLANE), ls] for t in range(n_tiles))


def _row_loop(static, n, body, init):
    if not static:
        return lax.fori_loop(0, n, body, init)
    carry = init
    for r in range(n):
        carry = body(r, carry)
    return carry


def _conv_rows(src, w_ref, bias, o_ref, offsets, tT, cb):
    K = len(offsets)
    for lt in range(cb // LANE):
        ls = slice(LANE * lt, LANE * (lt + 1))
        wv = [jnp.broadcast_to(w_ref[k:k + 1, ls], (SUBLANE, LANE)) for k in range(K)]
        b0 = jnp.zeros((SUBLANE, LANE), F32) if bias is None else jnp.broadcast_to(bias[:, ls], (SUBLANE, LANE))

        def rows(r, carry, ls=ls, wv=wv, b0=b0):
            base = r * CONV_RB if src.static else pl.multiple_of(r * CONV_RB, CONV_RB)
            accs = [b0] * CONV_TILES
            for k, off in enumerate(offsets):
                accs = [a + d * wv[k] for a, d in zip(accs, src.tiles(base, ls, off, CONV_TILES))]
            o_ref[pl.ds(base, CONV_RB), ls] = jnp.concatenate(accs, axis=0).astype(o_ref.dtype)
            return carry

        _row_loop(src.static, tT // CONV_RB, rows, 0)


def conv_fwd(name, u, blk0, w, b, T):
    K, C = w.shape
    P = (K - 1) // 2
    tT, cb = min(CONV_TT, T), CONV_CB
    n_t = T // tT

    static = K <= CONV_STATIC_MAX_K

    def body(prev_ref, cur_ref, next_ref, w_ref, b_ref, o_ref, pad_ref, sh_ref):
        offsets = [HALO - P + k for k in range(K)]
        _fill_padded(pad_ref, prev_ref, cur_ref, next_ref, pl.program_id(1), n_t, tT)
        _conv_rows(_Shifted(pad_ref, sh_ref, offsets, tT, static), w_ref, b_ref[...], o_ref, offsets, tT, cb)

    return pl.pallas_call(
        body, out_shape=jax.ShapeDtypeStruct((T, C), F32), grid=(C // cb, n_t),
        in_specs=_conv_specs(blk0, T, tT, cb) + [pl.BlockSpec((K, cb), lambda j, i: (0, j)),
                                                  pl.BlockSpec((1, cb), lambda j, i: (0, j))],
        out_specs=pl.BlockSpec((tT, cb), lambda j, i: (i, j)),
        scratch_shapes=[pltpu.VMEM((tT + 2 * HALO, cb), F32), pltpu.VMEM((SUBLANE, tT + 3 * SUBLANE, cb), F32)],
        name=name, compiler_params=_cparams(("parallel", "arbitrary")))(u, u, u, w, b)


def conv_bwd(name, dy, u, blk0, w, T, into=None):
    K, C = w.shape
    P = (K - 1) // 2
    tT, cb = min(CONV_TT, T), CONV_CB
    n_t = T // tT

    static = K <= CONV_STATIC_MAX_K

    def body(dprev, dcur, dnext, uprev, ucur, unext, w_ref, *rest):
        if into is not None:
            rest = rest[1:]
        du_ref, dw_ref, db_ref, padd_ref, padu_ref, shd_ref, shu_ref = rest
        i = pl.program_id(1)
        offsets = [HALO - P + k for k in range(K)]
        back = [HALO + P - k for k in range(K)]
        _fill_padded(padd_ref, dprev, dcur, dnext, i, n_t, tT)
        _fill_padded(padu_ref, uprev, ucur, unext, i, n_t, tT)
        src_d = _Shifted(padd_ref, shd_ref, back + [HALO], tT, static)
        src_u = _Shifted(padu_ref, shu_ref, offsets, tT, static)
        _conv_rows(src_d, w_ref, None, du_ref, back, tT, cb)

        @pl.when(i == 0)
        def _():
            dw_ref[...] = jnp.zeros(dw_ref.shape, F32)
            db_ref[...] = jnp.zeros(db_ref.shape, F32)

        rb = CONV_RED_TILES * SUBLANE
        for lt in range(cb // LANE):
            ls = slice(LANE * lt, LANE * (lt + 1))

            zero = jnp.zeros((SUBLANE, LANE), F32)
            for k0 in range(0, K, CONV_RED_TAPS):
                ks = list(range(k0, min(k0 + CONV_RED_TAPS, K)))

                def red(r, accs, ls=ls, ks=ks, k0=k0):
                    base = r * rb if static else pl.multiple_of(r * rb, rb)
                    d0, d1 = src_d.tiles(base, ls, HALO, CONV_RED_TILES)
                    new = []
                    for acc, k in zip(accs, ks):
                        u0, u1 = src_u.tiles(base, ls, offsets[k], CONV_RED_TILES)
                        new.append(acc + d0 * u0 + d1 * u1)
                    if k0 == 0:
                        new.append(accs[-1] + d0 + d1)
                    return tuple(new)

                accs = _row_loop(static, tT // rb, red, (zero,) * (len(ks) + (k0 == 0)))
                for acc, k in zip(accs, ks):
                    dw_ref[k:k + 1, ls] += jnp.sum(acc, axis=0, keepdims=True)
                if k0 == 0:
                    db_ref[:, ls] += jnp.sum(accs[-1], axis=0, keepdims=True)

    dspecs = _conv_specs(0, T, tT, cb)
    uspecs = _conv_specs(blk0, T, tT, cb)
    in_specs = dspecs + uspecs + [pl.BlockSpec((K, cb), lambda j, i: (0, j))]
    args = [dy, dy, dy, u, u, u, w]
    aliases = {}
    if into is None:
        du_spec = pl.BlockSpec((tT, cb), lambda j, i: (i, j))
        du_shape = jax.ShapeDtypeStruct((T, C), F32)
    else:
        arr, oblk = into
        in_specs.append(pl.BlockSpec(memory_space=pl.ANY))
        args.append(arr)
        aliases = {7: 0}
        du_spec = pl.BlockSpec((tT, cb), lambda j, i: (i, oblk + j))
        du_shape = jax.ShapeDtypeStruct(arr.shape, arr.dtype)
    return pl.pallas_call(
        body, out_shape=[du_shape, jax.ShapeDtypeStruct((K, C), F32), jax.ShapeDtypeStruct((1, C), F32)],
        grid=(C // cb, n_t), in_specs=in_specs,
        out_specs=[du_spec, pl.BlockSpec((K, cb), lambda j, i: (0, j)), pl.BlockSpec((1, cb), lambda j, i: (0, j))],
        scratch_shapes=[pltpu.VMEM((tT + 2 * HALO, cb), F32), pltpu.VMEM((tT + 2 * HALO, cb), F32),
                        pltpu.VMEM((SUBLANE, tT + 3 * SUBLANE, cb), F32), pltpu.VMEM((SUBLANE, tT + 3 * SUBLANE, cb), F32)],
        input_output_aliases=aliases, name=name, compiler_params=_cparams(("arbitrary", "arbitrary")))(*args)


def _dot(a, b, dims):
    return lax.dot_general(a.astype(BF16), b.astype(BF16), (dims, ((), ())), preferred_element_type=F32)


def _dnn(a, b):
    return _dot(a, b, ((1,), (0,)))


def _dnt(a, b):
    return _dot(a, b, ((1,), (1,)))


def _dtn(a, b):
    return _dot(a.T, b, ((1,), (0,)))


@jax.custom_vjp
def _nn(a, b):
    return _dnn(a, b)


_nn.defvjp(lambda a, b: (_dnn(a, b), (a, b)), lambda r, g: (_dnt(g, r[1]), _dtn(r[0], g)))


@jax.custom_vjp
def _nt(a, b):
    return _dnt(a, b)


_nt.defvjp(lambda a, b: (_dnt(a, b), (a, b)), lambda r, g: (_dnn(g, r[1]), _dtn(g, r[0])))


@jax.custom_vjp
def _tn(a, b):
    return _dtn(a, b)


_tn.defvjp(lambda a, b: (_dtn(a, b), (a, b)), lambda r, g: (_dnt(r[1], g), _dnn(r[0], g)))


def _split_dot(m, v):
    hi = v.astype(BF16)
    r1 = v - hi.astype(F32)
    mid = r1.astype(BF16)
    lo = (r1 - mid.astype(F32)).astype(BF16)
    mb = m.astype(BF16)
    d = lambda x: lax.dot_general(mb, x, (((1,), (0,)), ((), ())), preferred_element_type=F32)
    return d(hi) + d(mid) + d(lo)


@jax.custom_vjp
def _tri_dot(tri, tri_t, v):
    return _split_dot(tri, v)


_tri_dot.defvjp(lambda tri, tri_t, v: (_split_dot(tri, v), (tri, tri_t)),
                lambda r, g: (jnp.zeros_like(r[0]), jnp.zeros_like(r[1]), _split_dot(r[1], g)))


def _pick_vjp(axis):
    def pick(v, h):
        return v[:, h:h + 1] if axis == 1 else v[h:h + 1, :]

    def fwd(v, h):
        return pick(v, h), v.shape

    def bwd(h, shape, g):
        idx = lax.broadcasted_iota(jnp.int32, shape, axis)
        return (jnp.where(idx == h, g, 0.0),)

    f = jax.custom_vjp(pick, nondiff_argnums=(1,))
    f.defvjp(fwd, bwd)
    return f


_lane_pick = _pick_vjp(1)
_sub_pick = _pick_vjp(0)


def _onehot_lane(v, h):
    lane = lax.broadcasted_iota(jnp.int32, (1, v.shape[1]), 1)
    return jnp.sum(v * (lane == h).astype(F32), axis=1, keepdims=True)


def _onehot_sub(v, h):
    sub = lax.broadcasted_iota(jnp.int32, (v.shape[0], 1), 0)
    return jnp.sum(v * (sub == h).astype(F32), axis=0, keepdims=True)


def _ssd_consts(dirn, picks):
    ri = lax.broadcasted_iota(jnp.int32, (CHUNK, CHUNK), 0)
    ci = lax.broadcasted_iota(jnp.int32, (CHUNK, CHUNK), 1)
    keep = (ci <= ri) if dirn == 0 else (ci >= ri)
    tri = keep.astype(F32)
    tri_t = (~keep | (ci == ri)).astype(F32)
    lane = lax.broadcasted_iota(jnp.int32, (1, LANE), 1)
    sub = lax.broadcasted_iota(jnp.int32, (CHUNK, 1), 0)
    end = (sub == (CHUNK - 1 if dirn == 0 else 0)).astype(F32)
    lo_half = lane < HEAD_DIM
    pick = (_lane_pick, _sub_pick) if picks else (_onehot_lane, _onehot_sub)
    return keep, tri, tri_t, end, lo_half, N_HEADS * dirn, pick


def _ssd_chunk(consts, x_t, b_t, c_t, dtr, bias, alog, h_t):
    keep, tri, tri_t, end, lo_half, h_base, (lane_pick, sub_pick) = consts
    dt = _softplus(dtr + bias)
    a = dt * (-jnp.exp(alog))
    cs = _tri_dot(tri, tri_t, a)
    cs_t = cs.T
    tot = jnp.sum(cs * end, axis=0, keepdims=True)
    ys, hn = [], []
    for g in range(N_GROUPS):
        bm, cm = _silu(b_t[g]), _silu(c_t[g])
        gm = _nt(cm, bm)
        for jj in range(2):
            j = 2 * g + jj
            hh = (h_base + 2 * j, h_base + 2 * j + 1)
            col = [lane_pick(cs, h) for h in hh]
            row = [sub_pick(cs_t, h) for h in hh]
            dth = [lane_pick(dt, h) for h in hh]
            toth = [lane_pick(tot, h) for h in hh]
            xd = _silu(x_t[j]) * jnp.where(lo_half, dth[0], dth[1])
            yd = [_nn(gm * jnp.exp(jnp.where(keep, col[k] - row[k], NEG)), xd) for k in range(2)]
            cp = jnp.where(lo_half, col[0], col[1])
            tp = jnp.where(lo_half, toth[0], toth[1])
            ys.append(jnp.where(lo_half, yd[0], yd[1]) + _nn(cm, h_t[j]) * jnp.exp(cp))
            hn.append(h_t[j] * jnp.exp(tp) + _tn(bm, xd * jnp.exp(tp - cp)))
    return ys, hn


N_PAIR = D_INNER // LANE


def _tiles(ref, n):
    return [ref[:, LANE * j:LANE * (j + 1)].astype(F32) for j in range(n)]


def _ssd_in_specs(cmap):
    return [pl.BlockSpec((CHUNK, D_INNER), lambda i: (cmap(i), 0)),
            pl.BlockSpec((CHUNK, N_GROUPS * D_STATE), lambda i: (cmap(i), 2)),
            pl.BlockSpec((CHUNK, N_GROUPS * D_STATE), lambda i: (cmap(i), 3)),
            pl.BlockSpec((CHUNK, LANE), lambda i: (cmap(i), 0)),
            pl.BlockSpec((1, LANE), lambda i: (0, 0)), pl.BlockSpec((1, LANE), lambda i: (0, 0))]


def ssd_fwd(name, xbc, pdt, bias_row, alog_row, dirn, T):
    nc = T // CHUNK
    cmap = (lambda i: i) if dirn == 0 else (lambda i: nc - 1 - i)

    def body(x_ref, b_ref, c_ref, dt_ref, bias_ref, alog_ref, y_ref, hs_ref, h_scr):
        @pl.when(pl.program_id(0) == 0)
        def _():
            h_scr[...] = jnp.zeros(h_scr.shape, F32)

        hs_ref[0] = h_scr[...]
        ys, hn = _ssd_chunk(_ssd_consts(dirn, True), _tiles(x_ref, N_PAIR), _tiles(b_ref, N_GROUPS), _tiles(c_ref, N_GROUPS),
                            dt_ref[...], bias_ref[...], alog_ref[...], _tiles(h_scr, N_PAIR))
        for j in range(N_PAIR):
            y_ref[:, LANE * j:LANE * (j + 1)] = ys[j]
            h_scr[:, LANE * j:LANE * (j + 1)] = hn[j]

    return pl.pallas_call(
        body, out_shape=[jax.ShapeDtypeStruct((T, D_INNER), F32), jax.ShapeDtypeStruct((nc, D_STATE, D_INNER), F32)],
        grid=(nc,), in_specs=_ssd_in_specs(cmap),
        out_specs=[pl.BlockSpec((CHUNK, D_INNER), lambda i: (cmap(i), 0)),
                   pl.BlockSpec((1, D_STATE, D_INNER), lambda i: (cmap(i), 0, 0))],
        scratch_shapes=[pltpu.VMEM((D_STATE, D_INNER), F32)], name=name,
        compiler_params=_cparams(("arbitrary",)))(xbc, xbc, xbc, pdt, bias_row, alog_row)


def ssd_bwd(name, xbc, pdt, bias_row, alog_row, hs, dy, adds, dirn, T):
    nc = T // CHUNK
    cmap = (lambda i: nc - 1 - i) if dirn == 0 else (lambda i: i)
    GS = N_GROUPS * D_STATE

    n_add = len(adds)

    def body(x_ref, b_ref, c_ref, dt_ref, bias_ref, alog_ref, hs_ref, dy_ref, *rest):
        add_refs, (dx_ref, db_ref, dc_ref, ddt_ref, dbias_ref, dalog_ref, dh_scr) = rest[:n_add], rest[n_add:]
        ax_ref = add_refs[0]
        ab_ref, ac_ref, adt_ref = add_refs[1:] if n_add == 4 else (None, None, None)
        first = pl.program_id(0) == 0

        @pl.when(first)
        def _():
            dh_scr[...] = jnp.zeros(dh_scr.shape, F32)
            dbias_ref[...] = jnp.zeros(dbias_ref.shape, F32)
            dalog_ref[...] = jnp.zeros(dalog_ref.shape, F32)

        consts = _ssd_consts(dirn, False)
        fn = lambda *a: _ssd_chunk(consts, *a)
        _, vjp_fn = jax.vjp(fn, _tiles(x_ref, N_PAIR), _tiles(b_ref, N_GROUPS), _tiles(c_ref, N_GROUPS), dt_ref[...],
                            bias_ref[...], alog_ref[...], [hs_ref[0, :, LANE * j:LANE * (j + 1)] for j in range(N_PAIR)])
        dx, db, dc, ddt, dbias, dalog, dh = vjp_fn((_tiles(dy_ref, N_PAIR), _tiles(dh_scr, N_PAIR)))
        for j in range(N_PAIR):
            s = slice(LANE * j, LANE * (j + 1))
            dx_ref[:, s] = dx[j] + ax_ref[:, s]
            dh_scr[:, s] = dh[j]
        for g in range(N_GROUPS):
            s = slice(LANE * g, LANE * (g + 1))
            db_ref[:, s] = db[g] + (ab_ref[:, s] if n_add == 4 else 0.0)
            dc_ref[:, s] = dc[g] + (ac_ref[:, s] if n_add == 4 else 0.0)
        ddt_ref[...] = ddt + (adt_ref[...] if n_add == 4 else 0.0)
        dbias_ref[...] += dbias
        dalog_ref[...] += dalog

    blk = lambda w: pl.BlockSpec((CHUNK, w), lambda i: (cmap(i), 0))
    row = pl.BlockSpec((1, LANE), lambda i: (0, 0))
    return pl.pallas_call(
        body,
        out_shape=[jax.ShapeDtypeStruct((T, D_INNER), F32), jax.ShapeDtypeStruct((T, GS), F32),
                   jax.ShapeDtypeStruct((T, GS), F32), jax.ShapeDtypeStruct((T, LANE), F32),
                   jax.ShapeDtypeStruct((1, LANE), F32), jax.ShapeDtypeStruct((1, LANE), F32)],
        grid=(nc,),
        in_specs=_ssd_in_specs(cmap) + [pl.BlockSpec((1, D_STATE, D_INNER), lambda i: (cmap(i), 0, 0)), blk(D_INNER)]
        + [blk(D_INNER), blk(GS), blk(GS), blk(LANE)][:n_add],
        out_specs=[blk(D_INNER), blk(GS), blk(GS), blk(LANE), row, row],
        scratch_shapes=[pltpu.VMEM((D_STATE, D_INNER), F32)], name=name,
        compiler_params=_cparams(("arbitrary",)))(xbc, xbc, xbc, pdt, bias_row, alog_row, hs, dy, *adds)


def loss_head(y, target, T, tT=256):
    def body(y_ref, t_ref, dy_ref, sq_ref):
        @pl.when(pl.program_id(0) == 0)
        def _():
            sq_ref[...] = jnp.zeros(sq_ref.shape, F32)
        e = y_ref[...] - t_ref[...]
        dy_ref[...] = e * (1.0 / D_MODEL)
        sq_ref[...] += jnp.sum(e * e, axis=0, keepdims=True)

    spec = pl.BlockSpec((tT, D_MODEL), lambda i: (i, 0))
    return pl.pallas_call(
        body, out_shape=[jax.ShapeDtypeStruct((T, D_MODEL), F32), jax.ShapeDtypeStruct((1, D_MODEL), F32)],
        grid=(T // tT,), in_specs=[spec, spec], out_specs=[spec, pl.BlockSpec((1, D_MODEL), lambda i: (0, 0))],
        name="loss_head", compiler_params=_cparams(("arbitrary",)))(y, target)


MESH_ID = pl.DeviceIdType.MESH


def all_gather(name, v):
    R, W = v.shape

    def body(v_ref, out_ref, send_sems, recv_sems, local_sem):
        x, y, c = lax.axis_index("x"), lax.axis_index("y"), lax.axis_index("c")
        me, sibling = (x, y, c), (x, y, 1 - c)
        chips = [(1 - x, y), (x, 1 - y), (1 - x, 1 - y)]

        def slot(px, py, pc):
            return out_ref.at[4 * px + 2 * py + pc]

        def copy(k, block, to, src=None):
            return pltpu.make_async_remote_copy(
                src_ref=slot(*block) if src is None else src, dst_ref=slot(*block), send_sem=send_sems.at[k],
                recv_sem=recv_sems.at[k], device_id=to, device_id_type=MESH_ID)

        mine = pltpu.make_async_copy(v_ref, slot(*me), local_sem)
        mine.start()
        first = [copy(0, me, sibling, src=v_ref)]
        first += [copy(1 + j, me, (*chip, c), src=v_ref) for j, chip in enumerate(chips)]
        for cp in first:
            cp.start()
        passed = [copy(4 + j, (*chip, c), sibling) for j, chip in enumerate(chips)]
        for j, chip in enumerate(chips):
            copy(1 + j, (*chip, c), me).wait_recv()
            passed[j].start()
        copy(0, sibling, me).wait_recv()
        for j, chip in enumerate(chips):
            copy(4 + j, (*chip, 1 - c), me).wait_recv()
        for cp in first + passed:
            cp.wait_send()
        mine.wait()

    return pl.pallas_call(
        body, out_shape=jax.ShapeDtypeStruct((N_DEV, R, W), v.dtype),
        in_specs=[pl.BlockSpec(memory_space=pl.ANY)], out_specs=pl.BlockSpec(memory_space=pl.ANY),
        scratch_shapes=[pltpu.SemaphoreType.DMA((7,)), pltpu.SemaphoreType.DMA((7,)), pltpu.SemaphoreType.DMA],
        name=name, compiler_params=pltpu.CompilerParams(has_side_effects=True))(v)


def exchange_sibling(name, g):
    _, R, W = g.shape

    def body(g_ref, out_ref, send_sems, recv_sems):
        x, y, c = lax.axis_index("x"), lax.axis_index("y"), lax.axis_index("c")
        copies = [pltpu.make_async_remote_copy(
            src_ref=g_ref.at[2 * q + (1 - c)], dst_ref=out_ref.at[q], send_sem=send_sems.at[q], recv_sem=recv_sems.at[q],
            device_id=(x, y, 1 - c), device_id_type=MESH_ID) for q in range(4)]
        for cp in copies:
            cp.start()
        for cp in copies:
            cp.wait_recv()
        for cp in copies:
            cp.wait_send()

    return pl.pallas_call(
        body, out_shape=jax.ShapeDtypeStruct((4, R, W), g.dtype),
        in_specs=[pl.BlockSpec(memory_space=pl.ANY)], out_specs=pl.BlockSpec(memory_space=pl.ANY),
        scratch_shapes=[pltpu.SemaphoreType.DMA((4,)), pltpu.SemaphoreType.DMA((4,))],
        name=name, compiler_params=pltpu.CompilerParams(has_side_effects=True))(g)


def pair_sum(name, g, sib, core, tr):
    _, R, W = g.shape

    def body(core_ref, g_ref, s_ref, o_ref):
        o_ref[...] = (g_ref[...].astype(F32) + s_ref[...].astype(F32)).astype(o_ref.dtype)

    blk = pl.BlockSpec((1, tr, W), lambda q, i, core_ref: (q, i, 0))
    return pl.pallas_call(
        body, out_shape=jax.ShapeDtypeStruct((4, R, W), g.dtype),
        grid_spec=pltpu.PrefetchScalarGridSpec(
            num_scalar_prefetch=1, grid=(4, R // tr),
            in_specs=[pl.BlockSpec((1, tr, W), lambda q, i, core_ref: (2 * q + core_ref[0], i, 0)), blk], out_specs=blk),
        name=name, compiler_params=_cparams(("parallel", "parallel")))(core, g, sib)


def exchange_chips(name, s):
    _, R, W = s.shape

    def body(s_ref, out_ref, send_sems, recv_sems, local_sem):
        x, y, c = lax.axis_index("x"), lax.axis_index("y"), lax.axis_index("c")
        mine = 2 * x + y
        own = pltpu.make_async_copy(s_ref.at[mine], out_ref.at[mine], local_sem)
        own.start()
        copies = []
        for k in range(1, 4):
            px = 1 - x if (k >> 1) & 1 else x
            py = 1 - y if k & 1 else y
            copies.append(pltpu.make_async_remote_copy(
                src_ref=s_ref.at[2 * px + py], dst_ref=out_ref.at[mine], send_sem=send_sems.at[k - 1],
                recv_sem=recv_sems.at[k - 1], device_id=(px, py, c), device_id_type=MESH_ID))
        for cp in copies:
            cp.start()
        for cp in copies:
            cp.wait_recv()
        for cp in copies:
            cp.wait_send()
        own.wait()

    return pl.pallas_call(
        body, out_shape=jax.ShapeDtypeStruct(s.shape, s.dtype),
        in_specs=[pl.BlockSpec(memory_space=pl.ANY)], out_specs=pl.BlockSpec(memory_space=pl.ANY),
        scratch_shapes=[pltpu.SemaphoreType.DMA((3,)), pltpu.SemaphoreType.DMA((3,)), pltpu.SemaphoreType.DMA],
        name=name, compiler_params=pltpu.CompilerParams(has_side_effects=True))(s)


def reduce_pieces(g, tr):
    sib = exchange_sibling("grad_exchange_sibling", g)
    core = lax.axis_index("c").astype(jnp.int32).reshape(1)
    pair = pair_sum("grad_pair_sum", g, sib, core, _pick(g.shape[1], 4096) if g.shape[1] % LANE == 0 else tr)
    landed = exchange_chips("grad_exchange_chips", pair)
    return sum_slots("sum_grad_pieces", landed, tr)


def sum_slots(name, parts, tr):
    n_slot, R, W = parts.shape

    def body(p_ref, o_ref):
        g = p_ref[0].astype(F32)
        for s in range(1, n_slot):
            g = g + p_ref[s].astype(F32)
        o_ref[...] = g

    return pl.pallas_call(
        body, out_shape=jax.ShapeDtypeStruct((R, W), F32), grid=(R // tr,),
        in_specs=[pl.BlockSpec((n_slot, tr, W), lambda i: (0, i, 0))], out_specs=pl.BlockSpec((tr, W), lambda i: (i, 0)),
        name=name, compiler_params=_cparams(("parallel",)))(parts)


def adamw(name, parts, w, m, v, tr):
    R, W = w.shape
    n_slot = parts.shape[0]
    c1 = 1.0 / (1.0 - ADAM_B1 ** ADAM_STEP)
    c2 = 1.0 / (1.0 - ADAM_B2 ** ADAM_STEP)

    def body(p_ref, w_ref, m_ref, v_ref, g_ref, d_ref, nm_ref, nv_ref):
        g = p_ref[0]
        for s in range(1, n_slot):
            g = g + p_ref[s]
        nm = ADAM_B1 * m_ref[...] + (1.0 - ADAM_B1) * g
        nv = ADAM_B2 * v_ref[...] + (1.0 - ADAM_B2) * (g * g)
        g_ref[...] = g
        nm_ref[...] = nm
        nv_ref[...] = nv
        d_ref[...] = -ADAM_LR * ((nm * c1) / (jnp.sqrt(nv * c2) + ADAM_EPS) + ADAM_WD * w_ref[...])

    spec = pl.BlockSpec((tr, W), lambda i: (i, 0))
    return pl.pallas_call(
        body, out_shape=[jax.ShapeDtypeStruct((R, W), F32)] * 4, grid=(R // tr,),
        in_specs=[pl.BlockSpec((n_slot, tr, W), lambda i: (0, i, 0)), spec, spec, spec], out_specs=[spec] * 4,
        name=name, compiler_params=_cparams(("parallel",)))(parts, w, m, v)


def _pack(arrs, row_mult):
    flat = jnp.concatenate([a.reshape(-1) for a in arrs])
    n = flat.shape[0]
    rows = -(-n // LANE)
    rows = -(-rows // row_mult) * row_mult
    return jnp.pad(flat, (0, rows * LANE - n)).reshape(rows, LANE)


def _unpack(buf, shapes, lead=()):
    flat = buf.reshape(lead + (-1,))
    out, off = [], 0
    for s in shapes:
        n = math.prod(s)
        out.append(flat[..., off:off + n].reshape(lead + tuple(s)))
        off += n
    return out


def _rows_sharded(name):
    return name in COL_T or BIG_AXIS[name] == 1


def _shard_for_gather(name, w):
    return jnp.swapaxes(w, 1, 2) if name in COL_T else w


def _full_from_gathered(name, g):
    if _rows_sharded(name):
        return jnp.transpose(g, (1, 0, 2, 3)).reshape(g.shape[1], N_DEV * g.shape[2], g.shape[3])
    return jnp.transpose(g, (1, 2, 0, 3)).reshape(g.shape[1], g.shape[2], N_DEV * g.shape[3])


def _pieces_from_full(name, f):
    A, B = f.shape
    if _rows_sharded(name):
        return f.reshape(N_DEV, -1)
    return jnp.transpose(f.reshape(A, N_DEV, B // N_DEV), (1, 0, 2)).reshape(N_DEV, -1)


def _piece_shape(name, shard_shape):
    L, a, b = shard_shape
    return (L, b, a) if name in COL_T else (L, a, b)


def _rows(v):
    return v.reshape(1, -1).astype(F32)


def _head_rows(W):
    bias = jnp.pad(W["dt_bias"].reshape(1, -1), ((0, 0), (0, LANE - 2 * N_HEADS)))
    alog = jnp.pad(W["a_log"].reshape(1, -1), ((0, 0), (0, LANE - 2 * N_HEADS)))
    return bias, alog


def layer_fwd(li, x, p_l, W, T):
    n = lambda s: f"l{li}_{s}"
    S = {"x": x}
    proj = matmul(n("mm_in"), x, W["w_in"], "nt", out_dtype=BF16)
    pdt = matmul(n("mm_dt"), x, W["w_in"][N_IN_PAD - LANE:], "nt")
    (u0,) = rowcall(n("glu"), glu_fn, [(proj, 1024, 0), (proj, 1024, 1)], [], [(1024, F32, None)], T)
    u1 = conv_fwd(n("conv_a"), u0, 0, W["conv_a_w"], _rows(W["conv_a_b"]), T)
    (u3,) = rowcall(n("lnsilu"), lnsilu_fn, [(u1, 1024, 0)], [_rows(W["ln_a_g"]), _rows(W["ln_a_b"])],
                    [(1024, BF16, None)], T)
    y_a = matmul(n("mm_aout"), u3, W["w_a_out"], "nn")
    xbc = conv_fwd(n("conv_s"), proj, 6144 // CONV_CB, W["ssm_conv_w"], _rows(W["ssm_conv_b"]), T)
    bias_row, alog_row = _head_rows(W)
    y_f, hs_f = ssd_fwd(n("ssd_f"), xbc, pdt, bias_row, alog_row, 0, T)
    y_b, hs_b = ssd_fwd(n("ssd_r"), xbc, pdt, bias_row, alog_row, 1, T)
    dsk = jnp.repeat(W["d_skip"], HEAD_DIM).reshape(1, D_INNER)
    (yn,) = rowcall(n("gnorm"), gnorm_fn, [(y_f, 256, 0), (y_b, 256, 0), (xbc, 256, 0), (proj, 256, 16)],
                    [dsk, _rows(W["ssm_norm_g"])], [(256, BF16, None)], T, groups=N_GROUPS)
    y_bo = matmul(n("mm_bout"), yn, W["w_b_out"], "nn")
    (merged,) = rowcall(n("merge"), merge_fn, [(proj, 1024, 2), (proj, 1024, 3), (y_a, 1024, 0), (y_bo, 1024, 0)], [],
                        [(1024, BF16, None)], T)
    mix = matmul(n("mm_o"), merged, W["w_o"], "nn")
    (h,) = rowcall(n("ln1"), resln_fn, [(x, 1024, 0), (mix, 1024, 0)], [_rows(W["ln1_g"]), _rows(W["ln1_b"])],
                   [(1024, F32, None)], T)
    gu = matmul(n("mm_gu"), h, W["w_gate_up"], "nt", out_dtype=BF16)
    (act,) = rowcall(n("swiglu"), swiglu_fn, [(gu, FFN_DIM, 0), (gu, FFN_DIM, 1)], [], [(FFN_DIM, BF16, None)], T)
    dn = matmul(n("mm_down"), act, W["w_down"], "nn")
    (h2,) = rowcall(n("ln2"), resln_fn, [(h, 1024, 0), (dn, 1024, 0)], [_rows(W["ln2_g"]), _rows(W["ln2_b"])],
                    [(1024, F32, None)], T)
    pe = matmul(n("mm_ple"), p_l, W["w_ple"], "nn")
    gl = matmul(n("mm_pg"), h2, W["w_ple_gate"], "nn")
    (xn,) = rowcall(n("pleout"), ple_fn, [(h2, 1024, 0), (pe, 1024, 0), (gl, 1024, 0)], [_rows(W["ple_norm_g"])],
                    [(1024, F32, None)], T)
    S.update(proj=proj, pdt=pdt, u0=u0, u1=u1, u3=u3, y_a=y_a, xbc=xbc, y_f=y_f, y_b=y_b, hs_f=hs_f, hs_b=hs_b, yn=yn, y_bo=y_bo,
             merged=merged, mix=mix, h=h, gu=gu, act=act, dn=dn, h2=h2, pe=pe, gl=gl, dsk=dsk, bias_row=bias_row,
             alog_row=alog_row)
    return xn, S


def layer_bwd(li, dxn, p_l, W, S, T):
    n = lambda s: f"l{li}_{s}"
    G = {}
    x, proj = S["x"], S["proj"]
    (dh2a, dpe, dgl), (dpg,) = rowvjp(
        n("pleout_b"), ple_fn, [(S["h2"], 1024, 0), (S["pe"], 1024, 0), (S["gl"], 1024, 0)], [_rows(W["ple_norm_g"])],
        [(dxn, 1024, 0)], [([0], F32, None), ([1], BF16, None), ([2], BF16, None)], T)
    G["ple_norm_g"] = dpg
    G["w_ple_gate"] = matmul(n("mm_pg_w"), S["h2"], dgl, "tn", out_dtype=BF16)
    G["w_ple"] = matmul(n("mm_ple_w"), p_l, dpe, "tn", out_dtype=BF16)
    dh2 = matmul(n("mm_pg_x"), dgl, W["w_ple_gate"], "nt", add=dh2a)
    (dha, ddn), (G["ln2_g"], G["ln2_b"]) = rowvjp(
        n("ln2_b"), resln_fn, [(S["h"], 1024, 0), (S["dn"], 1024, 0)], [_rows(W["ln2_g"]), _rows(W["ln2_b"])],
        [(dh2, 1024, 0)], [([0], F32, None), ([1], BF16, None)], T)
    G["w_down"] = matmul(n("mm_down_w"), S["act"], ddn, "tn", out_dtype=BF16)
    dact = matmul(n("mm_down_x"), ddn, W["w_down"], "nt", out_dtype=BF16)
    (dgu,), _ = rowvjp(n("swiglu_b"), swiglu_fn, [(S["gu"], FFN_DIM, 0), (S["gu"], FFN_DIM, 1)], [],
                       [(dact, FFN_DIM, 0)], [([0, 1], BF16, None)], T)
    G["w_gate_up"] = matmul(n("mm_gu_w"), dgu, S["h"], "tn", out_dtype=BF16)
    dh = matmul(n("mm_gu_x"), dgu, W["w_gate_up"], "nn", add=dha)
    (dxa, dmix), (G["ln1_g"], G["ln1_b"]) = rowvjp(
        n("ln1_b"), resln_fn, [(x, 1024, 0), (S["mix"], 1024, 0)], [_rows(W["ln1_g"]), _rows(W["ln1_b"])],
        [(dh, 1024, 0)], [([0], F32, None), ([1], BF16, None)], T)
    G["w_o"] = matmul(n("mm_o_w"), S["merged"], dmix, "tn", out_dtype=BF16)
    dmerged = matmul(n("mm_o_x"), dmix, W["w_o"], "nt")
    dproj = jax.ShapeDtypeStruct((T, N_IN_PAD), BF16)
    (dproj, dy_a, dy_bo), _ = rowvjp(
        n("merge_b"), merge_fn, [(proj, 1024, 2), (proj, 1024, 3), (S["y_a"], 1024, 0), (S["y_bo"], 1024, 0)], [],
        [(dmerged, 1024, 0)], [([0, 1], BF16, (dproj, 1)), ([2], BF16, None), ([3], BF16, None)], T)
    G["w_a_out"] = matmul(n("mm_aout_w"), S["u3"], dy_a, "tn", out_dtype=BF16)
    du3 = matmul(n("mm_aout_x"), dy_a, W["w_a_out"], "nt")
    (du1,), (G["ln_a_g"], G["ln_a_b"]) = rowvjp(
        n("lnsilu_b"), lnsilu_fn, [(S["u1"], 1024, 0)], [_rows(W["ln_a_g"]), _rows(W["ln_a_b"])], [(du3, 1024, 0)],
        [([0], F32, None)], T)
    du0, G["conv_a_w"], G["conv_a_b"] = conv_bwd(n("conv_a_b"), du1, S["u0"], 0, W["conv_a_w"], T)
    (dproj,), _ = rowvjp(n("glu_b"), glu_fn, [(proj, 1024, 0), (proj, 1024, 1)], [], [(du0, 1024, 0)],
                         [([0, 1], BF16, (dproj, 0))], T)
    G["w_b_out"] = matmul(n("mm_bout_w"), S["yn"], dy_bo, "tn", out_dtype=BF16)
    dyn = matmul(n("mm_bout_x"), dy_bo, W["w_b_out"], "nt")
    (dys, dxs, dproj), (ddsk, G["ssm_norm_g"]) = rowvjp(
        n("gnorm_b"), gnorm_fn, [(S["y_f"], 256, 0), (S["y_b"], 256, 0), (S["xbc"], 256, 0), (proj, 256, 16)],
        [S["dsk"], _rows(W["ssm_norm_g"])], [(dyn, 256, 0)],
        [([0], F32, None), ([2], F32, None), ([3], BF16, (dproj, 16))], T, groups=N_GROUPS)
    G["d_skip"] = ddsk.reshape(N_HEADS, HEAD_DIM).sum(axis=1)
    dx1, db1, dc1, ddt1, dbias_f, dalog_f = ssd_bwd(
        n("ssd_f_b"), S["xbc"], S["pdt"], S["bias_row"], S["alog_row"], S["hs_f"], dys, (dxs,), 0, T)
    dxx, dbb, dcc, ddt, dbias_r, dalog_r = ssd_bwd(
        n("ssd_r_b"), S["xbc"], S["pdt"], S["bias_row"], S["alog_row"], S["hs_b"], dys, (dx1, db1, dc1, ddt1), 1, T)
    G["dt_bias"] = (dbias_f + dbias_r)[0, :2 * N_HEADS].reshape(2, N_HEADS)
    G["a_log"] = (dalog_f + dalog_r)[0, :2 * N_HEADS].reshape(2, N_HEADS)
    cw = W["ssm_conv_w"]
    b0 = 6144 // CONV_CB
    dproj, dwx, dbx = conv_bwd(n("conv_sx_b"), dxx, proj, b0, cw[:, :D_INNER], T, into=(dproj, b0))
    dproj, dwb, dbb_ = conv_bwd(n("conv_sb_b"), dbb, proj, b0 + 4, cw[:, D_INNER:D_INNER + 1024], T, into=(dproj, b0 + 4))
    dproj, dwc, dbc = conv_bwd(n("conv_sc_b"), dcc, proj, b0 + 6, cw[:, D_INNER + 1024:], T, into=(dproj, b0 + 6))
    G["ssm_conv_w"] = jnp.concatenate([dwx, dwb, dwc], axis=1)
    G["ssm_conv_b"] = jnp.concatenate([dbx, dbb_, dbc], axis=1)
    (dproj,) = rowcall(n("dt_cast"), ident_fn, [(ddt, LANE, 0)], [], [(LANE, BF16, (dproj, (N_IN_PAD - LANE) // LANE))], T)
    G["w_in"] = matmul(n("mm_in_w"), dproj, x, "tn", out_dtype=BF16)[:N_IN]
    dx = matmul(n("mm_in_x"), dproj, W["w_in"], "nn", add=dxa)
    return dx, G


def local_step(x, p, loss_target, FW, T):
    Ws, saves = [], []
    cur = x
    for li in range(DEPTH):
        W = {k: v[li] for k, v in FW.items()}
        Ws.append(W)
        cur, S = layer_fwd(li, cur, p[li], W, T)
        saves.append(S)
    dcur, sq = loss_head(cur, loss_target, T)
    loss = 0.5 * jnp.sum(sq) / D_MODEL
    grads = [None] * DEPTH
    for li in reversed(range(DEPTH)):
        dcur, grads[li] = layer_bwd(li, dcur, p[li], Ws[li], saves[li], T)
    return loss, dcur, grads


def kernel(x, p, w_in, conv_a_w, conv_a_b, ln_a_g, ln_a_b, w_a_out, ssm_conv_w, ssm_conv_b, a_log, dt_bias, d_skip, ssm_norm_g, w_b_out, w_o, ln1_g, ln1_b, w_gate_up, w_down, ln2_g, ln2_b, w_ple, ple_norm_g, w_ple_gate, loss_target, m_w_in, m_conv_a_w, m_conv_a_b, m_ln_a_g, m_ln_a_b, m_w_a_out, m_ssm_conv_w, m_ssm_conv_b, m_a_log, m_dt_bias, m_d_skip, m_ssm_norm_g, m_w_b_out, m_w_o, m_ln1_g, m_ln1_b, m_w_gate_up, m_w_down, m_ln2_g, m_ln2_b, m_w_ple, m_ple_norm_g, m_w_ple_gate, v_w_in, v_conv_a_w, v_conv_a_b, v_ln_a_g, v_ln_a_b, v_w_a_out, v_ssm_conv_w, v_ssm_conv_b, v_a_log, v_dt_bias, v_d_skip, v_ssm_norm_g, v_w_b_out, v_w_o, v_ln1_g, v_ln1_b, v_w_gate_up, v_w_down, v_ln2_g, v_ln2_b, v_w_ple, v_ple_norm_g, v_w_ple_gate):
    A = dict(locals())
    w = {k: A[k] for k in WEIGHTS}
    m = {k: A["m_" + k] for k in WEIGHTS}
    v = {k: A["v_" + k] for k in WEIGHTS}
    T = x.shape[1]
    small_shapes = [w[k].shape for k in SMALL]

    mm_names = [k for k in BIG if k not in CONV_W]
    shards = [_shard_for_gather(k, w[k].astype(BF16)) for k in mm_names]
    gathered = all_gather("gather_weights", _pack(shards, 16))
    FW = {k: _full_from_gathered(k, g) for k, g in zip(mm_names, _unpack(gathered, [t.shape for t in shards], (N_DEV,)))}
    FW["w_in"] = jnp.pad(FW["w_in"], ((0, 0), (0, N_IN_PAD - N_IN), (0, 0)))
    gathered = all_gather("gather_conv_weights", _pack([w[k] for k in CONV_W], SUBLANE))
    FW.update({k: _full_from_gathered(k, g)
               for k, g in zip(CONV_W, _unpack(gathered, [w[k].shape for k in CONV_W], (N_DEV,)))})
    FW.update({k: w[k] for k in SMALL})

    loss, grad_x, gfull = local_step(x[0], p[:, 0], loss_target[0], FW, T)
    loss = lax.psum(loss, ("x", "y", "c"))

    TR = 512
    flat = jnp.concatenate([_pieces_from_full(k, gfull[li][k]).astype(BF16) for k in BIG for li in range(DEPTH)], axis=1)
    rows = -(-flat.shape[1] // (LANE * TR)) * TR
    gpack = jnp.pad(flat, ((0, 0), (0, rows * LANE - flat.shape[1]))).reshape(N_DEV, rows, LANE)
    gsum = reduce_pieces(gpack, TR)
    gshard = [jnp.swapaxes(g, 1, 2) if k in COL_T else g
              for k, g in zip(BIG, _unpack(gsum, [_piece_shape(k, w[k].shape) for k in BIG]))]
    res_big = [{}, {}, {}, {}]
    for k, g in zip(BIG, gshard):
        two_d = lambda t: t.reshape(-1, t.shape[-1])
        rows_k = two_d(w[k]).shape[0]
        tr = max(d for d in range(1, min(rows_k, 256) + 1) if rows_k % d == 0 and (d % SUBLANE == 0 or d == rows_k))
        res = adamw("adamw_" + k, two_d(g)[None], two_d(w[k]), two_d(m[k]), two_d(v[k]), tr)
        for q in range(4):
            res_big[q][k] = res[q].reshape(w[k].shape)

    spack = _pack([jnp.stack([gfull[li][k] for li in range(DEPTH)]).reshape(w[k].shape) for k in SMALL], SUBLANE)
    sall = all_gather("gather_small_grads", spack)
    rs = spack.shape[0]
    res_small = adamw("adamw_small", sall, _pack([w[k] for k in SMALL], SUBLANE), _pack([m[k] for k in SMALL], SUBLANE),
                      _pack([v[k] for k in SMALL], SUBLANE), rs)
    res_small = [dict(zip(SMALL, _unpack(r, small_shapes))) for r in res_small]

    outs = [loss, grad_x[None]]
    for q in range(4):
        for k in WEIGHTS:
            outs.append(res_big[q][k] if k in res_big[q] else res_small[q][k])
    return tuple(outs)
```
